```python
import jax, jax.numpy as jnp
from jax import lax
import numpy as np

D_MODEL = 1024
BATCH = 8
SEQ = 2048
DEPTH = 2
DEC_BATCH = 32
DEC_SEQ = 4
PAST_LEN = 8192
PAGE_SIZE = 128

NSA_HEADS = 8
NSA_KV_HEADS = 2
NSA_HEAD_DIM = 64
CMP_STRIDE = 16
CMP_BLOCK = 32
CMP_RATIO = CMP_BLOCK // CMP_STRIDE
CMP_HIDDEN = 64
SEL_BLOCK = 64
N_SEL = 16
WINDOW = 512
NSA_QBLK = 64
RET_HEADS = 4
RET_DK = 128
RET_DV = 256
RET_CHUNK = 128
ROPE_BASE = 10000.0
D_FF = 2816
CONV_W = 3
RMS_EPS = 1e-6
GN_EPS = 1e-5
NEG = -1e30
FORCE_BONUS = 1e4

A_Q = NSA_HEADS * NSA_HEAD_DIM
A_KV = 3 * 2 * NSA_KV_HEADS * NSA_HEAD_DIM
A_G = 3 * NSA_HEADS
B_QK = RET_HEADS * RET_DK
B_V = RET_HEADS * RET_DV
IN_SPLITS = (A_Q, A_KV, A_G, B_QK, B_QK, B_V, B_V, D_MODEL, D_MODEL)
N_IN = A_Q + A_KV + A_G + 2 * B_QK + 2 * B_V + 2 * D_MODEL

kernel_name = 'nsa_retention_convffn_hybrid_step'


def split_points():
    pts, acc = [], 0
    for s in IN_SPLITS[:-1]:
        acc += s
        pts.append(acc)
    return pts


def rmsnorm(x, w):
    xf = x.astype(jnp.float32)
    y = xf * lax.rsqrt(jnp.mean(xf * xf, axis=-1, keepdims=True) + RMS_EPS)
    return (y * w.astype(jnp.float32)).astype(x.dtype)


def rotary(x, pos):
    half = x.shape[-1] // 2
    inv = jnp.exp(-jnp.log(ROPE_BASE) * jnp.arange(half, dtype=jnp.float32) / half)
    ang = pos.astype(jnp.float32)[:, None] * inv[None, :]
    cos = jnp.cos(ang)[None, :, None, :]
    sin = jnp.sin(ang)[None, :, None, :]
    xf = x.astype(jnp.float32)
    x1, x2 = xf[..., :half], xf[..., half:]
    return jnp.concatenate([x1 * cos - x2 * sin, x1 * sin + x2 * cos], axis=-1).astype(x.dtype)


def gather_pages(pool, page_table):
    g = pool[page_table]
    return g.reshape(g.shape[0], g.shape[1] * g.shape[2], *g.shape[3:])


def keep_last(rows, n):
    t = rows.shape[1]
    if t >= n:
        return rows[:, t - n:]
    return jnp.pad(rows, ((0, 0), (n - t, 0)) + ((0, 0),) * (rows.ndim - 2))


def compress_kv(kv, cmp_pos, cmp_w1, cmp_w2):
    B, L, _, G, dh = kv.shape
    n_chunk = L // CMP_STRIDE
    n_cmp = n_chunk - CMP_RATIO + 1
    chunks = kv[:, :n_chunk * CMP_STRIDE].reshape(B, n_chunk, CMP_STRIDE, 2, G, dh)
    w1 = cmp_w1.reshape(2, CMP_RATIO, CMP_STRIDE, dh, CMP_HIDDEN)
    bias = jnp.einsum('eld,eldh->eh', cmp_pos, cmp_w1.reshape(2, CMP_BLOCK, dh, CMP_HIDDEN))
    hid = bias[None, None, :, None, :]
    for m in range(CMP_RATIO):
        hid = hid + jnp.einsum('bnsegd,esdh->bnegh', chunks[:, m:m + n_cmp], w1[:, m])
    return jnp.einsum('bnegh,ehd->bnegd', jax.nn.gelu(hid), cmp_w2)


def sel_overlap(n_cmp, n_sel):
    cs = jnp.arange(n_cmp) * CMP_STRIDE
    ss = jnp.arange(n_sel) * SEL_BLOCK
    ok = (cs[:, None] < ss[None, :] + SEL_BLOCK) & (cs[:, None] + CMP_BLOCK > ss[None, :])
    return ok.astype(jnp.float32)


def nsa_block(q, qpos, gates, kc, vc, cmp_end, overlap, ks_blk, vs_blk, kw, vw, kwpos):
    f32 = jnp.float32
    B, Tb, H, dh = q.shape
    G = kc.shape[2]
    scale = dh ** -0.5
    qg = q.reshape(B, Tb, G, H // G, dh)
    s_c = jnp.einsum('btgqd,bngd->bgqtn', qg, kc).astype(f32) * scale
    c_ok = cmp_end[None, :] <= qpos[:, None]
    p_c = jnp.where(c_ok, jax.nn.softmax(jnp.where(c_ok, s_c, NEG), axis=-1), 0.0)
    o_c = jnp.einsum('bgqtn,bngd->btgqd', p_c.astype(vc.dtype), vc)
    imp = jnp.einsum('bgqtn,nj->bgtj', p_c, overlap)
    n_sel = ks_blk.shape[2]
    j = jnp.arange(n_sel)[None, :]
    cur = (qpos // SEL_BLOCK)[:, None]
    s_ok = j * SEL_BLOCK <= qpos[:, None]
    forced = (j == 0) | (j == cur) | (j == cur - 1)
    score = jnp.where(s_ok, imp + jnp.where(forced, FORCE_BONUS, 0.0), NEG)
    top_s, idx = lax.top_k(score, min(N_SEL, n_sel))
    take = jax.vmap(jax.vmap(lambda blk, ix: blk[ix]))
    k_s = take(ks_blk, idx)
    v_s = take(vs_blk, idx)
    tok = idx[..., None] * SEL_BLOCK + jnp.arange(SEL_BLOCK)
    t_ok = (top_s > 0.5 * NEG)[..., None] & (tok <= qpos[None, None, :, None, None])
    s_s = jnp.einsum('btgqd,bgtksd->bgqtks', qg, k_s).astype(f32) * scale
    s_s = jnp.where(t_ok[:, :, None], s_s, NEG)
    shp = s_s.shape
    p_s = jax.nn.softmax(s_s.reshape(*shp[:4], -1), axis=-1).reshape(shp)
    o_s = jnp.einsum('bgqtks,bgtksd->btgqd', p_s.astype(v_s.dtype), v_s)
    s_w = jnp.einsum('btgqd,bsgd->bgqts', qg, kw).astype(f32) * scale
    rel = qpos[:, None] - kwpos[None, :]
    w_ok = (rel >= 0) & (rel < WINDOW) & (kwpos[None, :] >= 0)
    p_w = jax.nn.softmax(jnp.where(w_ok, s_w, NEG), axis=-1)
    o_w = jnp.einsum('bgqts,bsgd->btgqd', p_w.astype(vw.dtype), vw)
    g = jax.nn.sigmoid(gates.astype(f32)).reshape(B, Tb, G, H // G, 3, 1).astype(q.dtype)
    o = g[..., 0, :] * o_c + g[..., 1, :] * o_s + g[..., 2, :] * o_w
    return o.reshape(B, Tb, H * dh)


def nsa_attention(q, gates, kv_cmp, kv_sel, kv_win, q_pos0, w_pos0, cmp_pos, cmp_w1, cmp_w2):
    B, Tq, H, dh = q.shape
    L = kv_cmp.shape[1]
    G = kv_cmp.shape[3]
    comp = compress_kv(kv_cmp, cmp_pos, cmp_w1, cmp_w2)
    kc, vc = comp[:, :, 0], comp[:, :, 1]
    n_cmp = comp.shape[1]
    cmp_end = jnp.arange(n_cmp) * CMP_STRIDE + (CMP_BLOCK - 1)
    n_sel = -(-L // SEL_BLOCK)
    overlap = sel_overlap(n_cmp, n_sel)
    kvs = jnp.pad(kv_sel, ((0, 0), (0, n_sel * SEL_BLOCK - L), (0, 0), (0, 0), (0, 0)))
    kvs = kvs.reshape(B, n_sel, SEL_BLOCK, 2, G, dh).transpose(3, 0, 4, 1, 2, 5)
    past_w = kv_win.shape[1] - Tq
    pad_n = max(WINDOW - past_w, 0)
    kvw = jnp.pad(kv_win, ((0, 0), (pad_n, 0), (0, 0), (0, 0), (0, 0)))
    base = past_w + pad_n - WINDOW
    kw_pos0 = w_pos0 - pad_n
    qb_len = NSA_QBLK if Tq % NSA_QBLK == 0 else Tq
    nb = Tq // qb_len
    qb = q.reshape(B, nb, qb_len, H, dh).swapaxes(0, 1)
    gb = gates.reshape(B, nb, qb_len, H, 3).swapaxes(0, 1)

    def one(args):
        i, q_i, g_i = args
        offs = i * qb_len
        qpos = q_pos0 + offs + jnp.arange(qb_len)
        start = base + offs
        w_i = lax.dynamic_slice_in_dim(kvw, start, WINDOW + qb_len, axis=1)
        kwpos = kw_pos0 + start + jnp.arange(WINDOW + qb_len)
        return nsa_block(q_i, qpos, g_i, kc, vc, cmp_end, overlap, kvs[0], kvs[1],
                         w_i[:, :, 0], w_i[:, :, 1], kwpos)

    out = lax.map(one, (jnp.arange(nb), qb, gb))
    return out.swapaxes(0, 1).reshape(B, Tq, H * dh)


def retention(q, k, v, state0):
    f32 = jnp.float32
    B, T, H, _ = q.shape
    dv = v.shape[-1]
    C = RET_CHUNK if T % RET_CHUNK == 0 else T
    n = T // C
    log_g = jnp.log(1.0 - jnp.exp2(-5.0 - jnp.arange(H, dtype=f32)))
    i = jnp.arange(C, dtype=f32)
    diff = i[:, None] - i[None, :]
    dmask = jnp.where(diff >= 0, jnp.exp(jnp.maximum(diff, 0.0)[None] * log_g[:, None, None]), 0.0)
    q_dec = jnp.exp((i + 1.0)[None, :] * log_g[:, None])[..., None]
    k_dec = jnp.exp((C - 1.0 - i)[None, :] * log_g[:, None])[..., None]
    s_dec = jnp.exp(C * log_g)[:, None, None]

    def chunks(a):
        return a.astype(f32).reshape(B, n, C, H, a.shape[-1]).transpose(1, 0, 3, 2, 4)

    def step(S, qkv):
        qc, kc, vc = qkv
        inner = jnp.einsum('bhcd,bhed->bhce', qc, kc) * dmask
        o = jnp.einsum('bhce,bhev->bhcv', inner, vc) + jnp.einsum('bhcd,bhdv->bhcv', qc * q_dec, S)
        S = S * s_dec + jnp.einsum('bhcd,bhcv->bhdv', kc * k_dec, vc)
        return S, o

    S, o = lax.scan(step, state0.astype(f32), (chunks(q), chunks(k), chunks(v)))
    return o.transpose(1, 0, 3, 2, 4).reshape(B, T, H, dv), S


def group_norm(o, w):
    mu = jnp.mean(o, axis=-1, keepdims=True)
    var = jnp.mean(jnp.square(o - mu), axis=-1, keepdims=True)
    return (o - mu) * lax.rsqrt(var + GN_EPS) * w.astype(jnp.float32)


def conv_ffn(h, prev, w_in, conv_w, conv_b, w_out):
    T = h.shape[1]
    a, b = jnp.split(h @ w_in, 2, axis=-1)
    a_ext = jnp.concatenate([prev.astype(a.dtype), a], axis=1)
    u = conv_b
    for j in range(CONV_W):
        u = u + a_ext[:, j:j + T] * conv_w[j]
    y = (jax.nn.gelu(u) * b) @ w_out
    return y, a_ext[:, T:]


def decoder_layer(x, c, pos0, past_cmp, past_sel, win_buf, ret_state, conv_prev, win_keep,
                  norm1_w, ada_w, ada_b, w_in, cmp_pos, cmp_w1, cmp_w2, w_oa, ret_gn_w, w_ob, w_o,
                  norm2_w, ffn_w_in, ffn_conv_w, ffn_conv_b, ffn_w_out):
    B, T, _ = x.shape
    mod = jax.nn.silu(c) @ ada_w + ada_b
    sh1, sc1, g1, sh2, sc2, g2 = jnp.split(mod[:, None, :], 6, axis=-1)
    h = rmsnorm(x, norm1_w) * (1 + sc1) + sh1
    aq, akv, ag, rq, rk, rv, rg, ga, gb = jnp.split(h @ w_in, split_points(), axis=-1)
    akv = akv.reshape(B, T, 3, 2, NSA_KV_HEADS, NSA_HEAD_DIM)
    kv_cmp, kv_sel, kv_win = akv[:, :, 0], akv[:, :, 1], akv[:, :, 2]
    win_all = jnp.concatenate([win_buf.astype(x.dtype), kv_win], axis=1)
    y_a = nsa_attention(aq.reshape(B, T, NSA_HEADS, NSA_HEAD_DIM), ag.reshape(B, T, NSA_HEADS, 3),
                        jnp.concatenate([past_cmp.astype(x.dtype), kv_cmp], axis=1),
                        jnp.concatenate([past_sel.astype(x.dtype), kv_sel], axis=1),
                        win_all, pos0, pos0 - win_buf.shape[1], cmp_pos, cmp_w1, cmp_w2) @ w_oa
    pos = pos0 + jnp.arange(T)
    rq = rotary(rq.reshape(B, T, RET_HEADS, RET_DK), pos)
    rk = rotary(rk.reshape(B, T, RET_HEADS, RET_DK), pos) * (RET_DK ** -0.5)
    ro, ret_new = retention(rq, rk, rv.reshape(B, T, RET_HEADS, RET_DV), ret_state)
    ro = group_norm(ro, ret_gn_w.reshape(RET_HEADS, RET_DV)).reshape(B, T, B_V).astype(x.dtype)
    y_b = (jax.nn.silu(rg) * ro) @ w_ob
    x = x + g1 * ((jax.nn.sigmoid(ga) * y_a + jax.nn.sigmoid(gb) * y_b) @ w_o)
    h2 = rmsnorm(x, norm2_w) * (1 + sc2) + sh2
    f, conv_new = conv_ffn(h2, conv_prev, ffn_w_in, ffn_conv_w, ffn_conv_b, ffn_w_out)
    x = x + g2 * f
    return x, (kv_cmp, kv_sel, keep_last(win_all, win_keep), ret_new, conv_new)


def setup_inputs(seed: int = 0) -> dict:
    key = jax.random.key(seed)
    ks = jax.random.split(key, 32)
    f32 = jnp.float32
    n_pages = PAST_LEN // PAGE_SIZE
    used = DEC_BATCH * n_pages
    n_phys = used + max(1, used // 4)
    win_keep = min(WINDOW, PAST_LEN)
    kvs = (2, NSA_KV_HEADS, NSA_HEAD_DIM)

    def nrm(k, shape, s):
        return jax.random.normal(k, shape, f32) * s

    page_table = jax.random.permutation(ks[9], n_phys)[:used].reshape(DEC_BATCH, n_pages).astype(jnp.int32)
    return {
        'x_prompt': nrm(ks[0], (BATCH, SEQ, D_MODEL), 1.0),
        'x_sample': nrm(ks[1], (DEC_BATCH, DEC_SEQ, D_MODEL), 1.0),
        'c_prompt': nrm(ks[2], (BATCH, D_MODEL), 1.0),
        'c_sample': nrm(ks[3], (DEC_BATCH, D_MODEL), 1.0),
        'cache_cmp_kv': nrm(ks[4], (DEPTH, n_phys, PAGE_SIZE) + kvs, 1.0),
        'cache_sel_kv': nrm(ks[5], (DEPTH, n_phys, PAGE_SIZE) + kvs, 1.0),
        'state_win_kv': nrm(ks[6], (DEPTH, DEC_BATCH, win_keep) + kvs, 1.0),
        'state_ret': nrm(ks[7], (DEPTH, DEC_BATCH, RET_HEADS, RET_DK, RET_DV), 0.5),
        'state_conv': nrm(ks[8], (DEPTH, DEC_BATCH, CONV_W - 1, D_FF), 1.0),
        'page_table': page_table,
        'norm1_w': 1.0 + nrm(ks[10], (DEPTH, D_MODEL), 0.02),
        'ada_w': nrm(ks[11], (DEPTH, D_MODEL, 6 * D_MODEL), D_MODEL ** -0.5),
        'ada_b': nrm(ks[12], (DEPTH, 6 * D_MODEL), 0.01),
        'w_in': nrm(ks[13], (DEPTH, D_MODEL, N_IN), D_MODEL ** -0.5),
        'cmp_pos': nrm(ks[14], (DEPTH, 2, CMP_BLOCK, NSA_HEAD_DIM), 0.1),
        'cmp_w1': nrm(ks[15], (DEPTH, 2, CMP_BLOCK * NSA_HEAD_DIM, CMP_HIDDEN), (CMP_BLOCK * NSA_HEAD_DIM) ** -0.5),
        'cmp_w2': nrm(ks[16], (DEPTH, 2, CMP_HIDDEN, NSA_HEAD_DIM), CMP_HIDDEN ** -0.5),
        'w_oa': nrm(ks[17], (DEPTH, A_Q, D_MODEL), A_Q ** -0.5),
        'ret_gn_w': 1.0 + nrm(ks[18], (DEPTH, B_V), 0.02),
        'w_ob': nrm(ks[19], (DEPTH, B_V, D_MODEL), B_V ** -0.5),
        'w_o': nrm(ks[20], (DEPTH, D_MODEL, D_MODEL), D_MODEL ** -0.5),
        'norm2_w': 1.0 + nrm(ks[21], (DEPTH, D_MODEL), 0.02),
        'ffn_w_in': nrm(ks[22], (DEPTH, D_MODEL, 2 * D_FF), D_MODEL ** -0.5),
        'ffn_conv_w': nrm(ks[23], (DEPTH, CONV_W, D_FF), CONV_W ** -0.5),
        'ffn_conv_b': nrm(ks[24], (DEPTH, D_FF), 0.01),
        'ffn_w_out': nrm(ks[25], (DEPTH, D_FF, D_MODEL), D_FF ** -0.5),
        'normf_w': 1.0 + nrm(ks[26], (D_MODEL,), 0.02),
    }


def reference(x_prompt, x_sample, c_prompt, c_sample, cache_cmp_kv, cache_sel_kv, state_win_kv,
              state_ret, state_conv, page_table, norm1_w, ada_w, ada_b, w_in, cmp_pos, cmp_w1,
              cmp_w2, w_oa, ret_gn_w, w_ob, w_o, norm2_w, ffn_w_in, ffn_conv_w, ffn_conv_b,
              ffn_w_out, normf_w):
    B = x_prompt.shape[0]
    win_keep = state_win_kv.shape[2]
    past_len = page_table.shape[1] * cache_cmp_kv.shape[2]
    empty = jnp.zeros((B, 0, 2, NSA_KV_HEADS, NSA_HEAD_DIM), x_prompt.dtype)
    ret0 = jnp.zeros((B, RET_HEADS, RET_DK, RET_DV), jnp.float32)
    conv0 = jnp.zeros((B, CONV_W - 1, D_FF), x_prompt.dtype)
    xp, xs = x_prompt, x_sample
    outs_p, outs_s = [], []
    for l in range(DEPTH):
        w = (norm1_w[l], ada_w[l], ada_b[l], w_in[l], cmp_pos[l], cmp_w1[l], cmp_w2[l], w_oa[l],
             ret_gn_w[l], w_ob[l], w_o[l], norm2_w[l], ffn_w_in[l], ffn_conv_w[l], ffn_conv_b[l],
             ffn_w_out[l])
        xp, sp = decoder_layer(xp, c_prompt, 0, empty, empty, empty, ret0, conv0, win_keep, *w)
        xs, ss = decoder_layer(xs, c_sample, past_len,
                               gather_pages(cache_cmp_kv[l], page_table),
                               gather_pages(cache_sel_kv[l], page_table),
                               state_win_kv[l], state_ret[l], state_conv[l], win_keep, *w)
        outs_p.append(sp)
        outs_s.append(ss)
    y_prompt = rmsnorm(xp, normf_w)
    y_sample = rmsnorm(xs, normf_w)

    def stk(outs, i, dt):
        return jnp.stack([o[i] for o in outs]).astype(dt)

    return (y_prompt, y_sample,
            stk(outs_p, 0, cache_cmp_kv.dtype), stk(outs_s, 0, cache_cmp_kv.dtype),
            stk(outs_p, 1, cache_sel_kv.dtype), stk(outs_s, 1, cache_sel_kv.dtype),
            stk(outs_p, 2, state_win_kv.dtype), stk(outs_s, 2, state_win_kv.dtype),
            stk(outs_p, 3, state_ret.dtype), stk(outs_s, 3, state_ret.dtype),
            stk(outs_p, 4, state_conv.dtype), stk(outs_s, 4, state_conv.dtype))
```

```python
import functools

import jax
import jax.numpy as jnp
from jax import lax
from jax.experimental import pallas as pl
from jax.experimental.pallas import tpu as pltpu

f32 = jnp.float32
bf16 = jnp.bfloat16

NSA_HEADS = 8
NSA_KV_HEADS = 2
HPG = NSA_HEADS // NSA_KV_HEADS
HD = 64
CMP_STRIDE = 16
CMP_BLOCK = 32
CMP_HIDDEN = 64
SEL_BLOCK = 64
SEL_SHIFT = 6
N_SEL = 16
WINDOW = 512
RET_HEADS = 4
RET_DK = 128
RET_DV = 256
RET_CHUNK = 128
ROPE_BASE = 10000.0
CONV_W = 3
RMS_EPS = 1e-6
GN_EPS = 1e-5
NEG = -1e30
FORCE_BONUS = 1e4

A_Q = NSA_HEADS * HD
KV_ROWS = 2 * NSA_KV_HEADS * HD
A_KV = 3 * KV_ROWS
A_G = 3 * NSA_HEADS
B_QK = RET_HEADS * RET_DK
B_V = RET_HEADS * RET_DV

OFF_RG, OFF_GA, OFF_GB, OFF_RV = 0, 1024, 2048, 3072
OFF_AQ, OFF_RQ, OFF_RK, OFF_KVC, OFF_AG = 4096, 4608, 5120, 5632, 5888
N_TOK = 6144

LANE = 128
VMEM_LIMIT = 56 * 1024 * 1024


def _cparams(sem):
    return pltpu.CompilerParams(dimension_semantics=sem, vmem_limit_bytes=VMEM_LIMIT)


def _dot(a, b):
    return jnp.dot(a, b, preferred_element_type=f32)


def _dot_nt(a, b):
    return lax.dot_general(a, b, (((1,), (1,)), ((), ())), preferred_element_type=f32)


def _dot_tn(a, b):
    return lax.dot_general(a, b, (((0,), (0,)), ((), ())), preferred_element_type=f32)


def _split_dot(a, b_bf16):
    hi = a.astype(bf16)
    lo = (a - hi.astype(f32)).astype(bf16)
    return _dot(hi, b_bf16) + _dot(lo, b_bf16)


def _ada_body(c_ref, w_ref, b_ref, o_ref):
    c = c_ref[...]
    s = c * jax.nn.sigmoid(c)
    o_ref[...] = _dot(s.astype(bf16), w_ref[...].astype(bf16)) + b_ref[...]


def ada_mod(c_all, ada_w, ada_b):
    depth, d, n = ada_w.shape
    r = c_all.shape[0]
    tn = 1024
    return pl.pallas_call(
        _ada_body,
        grid=(depth, n // tn),
        in_specs=[pl.BlockSpec((r, d), lambda l, j: (0, 0)),
                  pl.BlockSpec((None, d, tn), lambda l, j: (l, 0, j)),
                  pl.BlockSpec((None, 1, tn), lambda l, j: (l, 0, j))],
        out_specs=pl.BlockSpec((None, r, tn), lambda l, j: (l, 0, j)),
        out_shape=jax.ShapeDtypeStruct((depth, r, n), f32),
        compiler_params=_cparams(("parallel", "parallel")),
        name="ada_mod",
    )(c_all, ada_w, ada_b.reshape(depth, 1, n))


def _norm_proj_body(x_ref, sc_ref, sh_ref, nw_ref, w_ref, *rest, has_t):
    if has_t:
        wt_ref, o_ref, ot_ref, h_scr = rest
    else:
        o_ref, h_scr = rest

    @pl.when(pl.program_id(2) == 0)
    def _():
        x = x_ref[...]
        y = x * lax.rsqrt(jnp.mean(x * x, axis=-1, keepdims=True) + RMS_EPS) * nw_ref[...]
        hb = (y * (1.0 + sc_ref[...]) + sh_ref[...]).astype(bf16)
        h_scr[...] = hb
        if has_t:
            ot_ref[...] = _dot_nt(wt_ref[...], hb)

    o_ref[...] = _dot(h_scr[...], w_ref[...])


def norm_proj(x, sc, sh, nw, w, wt, *, tm, tn):
    G, T, D = x.shape
    N = w.shape[1]
    per_row = sc.shape[1] != 1
    mr = tm if per_row else 1
    mod_spec = pl.BlockSpec((None, mr, D), (lambda g, i, j: (g, i, 0)) if per_row else (lambda g, i, j: (g, 0, 0)))
    in_specs = [pl.BlockSpec((None, tm, D), lambda g, i, j: (g, i, 0)), mod_spec, mod_spec,
                pl.BlockSpec((1, D), lambda g, i, j: (0, 0)),
                pl.BlockSpec((D, tn), lambda g, i, j: (0, j))]
    out_specs = [pl.BlockSpec((None, tm, tn), lambda g, i, j: (g, i, j))]
    out_shape = [jax.ShapeDtypeStruct((G, T, N), f32)]
    args = [x, sc, sh, nw.reshape(1, D), w]
    if wt is not None:
        NT = wt.shape[0]
        in_specs.append(pl.BlockSpec((NT, D), lambda g, i, j: (0, 0)))
        out_specs.append(pl.BlockSpec((None, NT, tm), lambda g, i, j: (g, 0, i)))
        out_shape.append(jax.ShapeDtypeStruct((G, NT, T), f32))
        args.append(wt)
    res = pl.pallas_call(
        functools.partial(_norm_proj_body, has_t=wt is not None),
        grid=(G, T // tm, N // tn),
        in_specs=in_specs, out_specs=out_specs, out_shape=out_shape,
        scratch_shapes=[pltpu.VMEM((tm, D), bf16)],
        compiler_params=_cparams(("parallel", "parallel", "arbitrary")),
        name="norm_proj_t" if wt is not None else "norm_proj",
    )(*args)
    return res if wt is not None else res[0]


def _compress_finish(xs_ref, pos_ref, w1_ref, w2_ref, o_ref):
    n = o_ref.shape[0]
    xs_ref[:, pl.ds(n * CMP_STRIDE, CMP_BLOCK), :] = jnp.zeros((2, CMP_BLOCK, LANE), f32)
    out = jnp.zeros((n, 2 * LANE), f32)
    for e in range(2):
        acc = jnp.zeros((n, LANE), f32)
        for l in range(CMP_BLOCK):
            a = xs_ref[e, pl.ds(l, n, stride=CMP_STRIDE), :] + pos_ref[e, l:l + 1, :]
            acc = acc + _dot(a.astype(bf16), w1_ref[e, l])
        out = out + _dot(jax.nn.gelu(acc).astype(bf16), w2_ref[e])
    o_ref[...] = out


def _compress_prompt_body(x_ref, pos_ref, w1_ref, w2_ref, o_ref, xs_ref):
    T = x_ref.shape[0]
    xs_ref[0, 0:T, :] = x_ref[:, 0:LANE]
    xs_ref[1, 0:T, :] = x_ref[:, LANE:2 * LANE]
    _compress_finish(xs_ref, pos_ref, w1_ref, w2_ref, o_ref)


def _cmp_weight_specs(nargs):
    z3 = (lambda *a: (0, 0, 0))
    z4 = (lambda *a: (0, 0, 0, 0))
    return [pl.BlockSpec((2, CMP_BLOCK, LANE), z3),
            pl.BlockSpec((2, CMP_BLOCK, LANE, LANE), z4),
            pl.BlockSpec((2, LANE, 2 * LANE), z3)]


def compress_prompt(proj, cw):
    B, T, _ = proj.shape
    n = T // CMP_STRIDE
    return pl.pallas_call(
        _compress_prompt_body,
        grid=(B,),
        in_specs=[pl.BlockSpec((None, T, KV_ROWS), lambda b: (b, 0, OFF_KVC // KV_ROWS))] + _cmp_weight_specs(1),
        out_specs=pl.BlockSpec((None, n, 2 * LANE), lambda b: (b, 0, 0)),
        out_shape=jax.ShapeDtypeStruct((B, n, 2 * LANE), f32),
        scratch_shapes=[pltpu.VMEM((2, T + CMP_BLOCK, LANE), f32)],
        compiler_params=_cparams(("parallel",)),
        name="compress_prompt",
    )(proj, *cw)


def _compress_paged_body(pt_ref, pg_ref, pos_ref, w1_ref, w2_ref, o_ref, xs_ref):
    p = pl.program_id(1)
    r0 = pl.multiple_of(p * LANE, LANE)
    xs_ref[0, pl.ds(r0, LANE), :] = pg_ref[0:LANE, :].T
    xs_ref[1, pl.ds(r0, LANE), :] = pg_ref[LANE:2 * LANE, :].T

    @pl.when(p == pl.num_programs(1) - 1)
    def _():
        _compress_finish(xs_ref, pos_ref, w1_ref, w2_ref, o_ref)


def compress_paged(cacheT, layer, page_table, cw):
    Bd, n_pages = page_table.shape
    page = cacheT.shape[-1]
    assert page == LANE
    L = n_pages * page
    n = L // CMP_STRIDE
    return pl.pallas_call(
        _compress_paged_body,
        grid_spec=pltpu.PrefetchScalarGridSpec(
            num_scalar_prefetch=1,
            grid=(Bd, n_pages),
            in_specs=[pl.BlockSpec((None, None, KV_ROWS, page), lambda b, p, pt: (layer, pt[b, p], 0, 0))]
            + _cmp_weight_specs(3),
            out_specs=pl.BlockSpec((None, n, 2 * LANE), lambda b, p, pt: (b, 0, 0)),
            scratch_shapes=[pltpu.VMEM((2, L + CMP_BLOCK, LANE), f32)],
        ),
        out_shape=jax.ShapeDtypeStruct((Bd, n, 2 * LANE), f32),
        compiler_params=_cparams(("parallel", "arbitrary")),
        name="compress_paged",
    )(page_table, cacheT, *cw)


def _select_blocks(imp, s_ok, forced, n_sel, jj):
    k = min(N_SEL, n_sel)
    score = jnp.where(s_ok, imp + jnp.where(forced, FORCE_BONUS, 0.0), NEG)
    rank = jnp.zeros(score.shape, f32)
    for j2 in range(n_sel):
        col = score[:, j2:j2 + 1]
        ge = jnp.where(col >= score, 1.0, 0.0)
        gt = jnp.where(col > score, 1.0, 0.0)
        rank = rank + jnp.where(jj > j2, ge, gt)
    return jnp.where((rank < k) & s_ok, 1.0, 0.0)


def _softmax_rows(sm, ok):
    m = jnp.max(sm, axis=-1, keepdims=True)
    e = jnp.where(ok, jnp.exp(sm - m), 0.0)
    den = jnp.sum(e, axis=-1, keepdims=True)
    return e / jnp.maximum(den, 1e-30)


def _nsa_prompt_body(q_ref, ag_ref, comp_ref, ksT_ref, vsT_ref, kwT_ref, vwT_ref, ov_ref, ex_ref, o_ref,
                     *, tq, T, n_cmp, n_sel, wk):
    i = pl.program_id(1)
    t0 = i * tq
    scale = HD ** -0.5
    NC = comp_ref.shape[0]
    SW = ov_ref.shape[1]
    qpos = t0 + lax.broadcasted_iota(jnp.int32, (tq, 1), 0)
    col_n = lax.broadcasted_iota(jnp.int32, (tq, NC), 1)
    c_ok = ((col_n * CMP_STRIDE + (CMP_BLOCK - 1)) <= qpos) & (col_n < n_cmp)
    jj = lax.broadcasted_iota(jnp.int32, (tq, SW), 1)
    cur = jnp.right_shift(qpos, SEL_SHIFT)
    s_ok = (jj * SEL_BLOCK <= qpos) & (jj < n_sel)
    forced = (jj == 0) | (jj == cur) | (jj == cur - 1)
    kpos =lax.broadcasted_iota(jnp.int32, (tq, T), 1)
    causal = kpos <= qpos
    wstart = pl.multiple_of(jnp.maximum(t0 + tq - wk, 0), LANE)
    kposw = wstart + lax.broadcasted_iota(jnp.int32, (tq, wk), 1)
    rel = qpos - kposw
    w_ok = (rel >= 0) & (rel < WINDOW)
    sg = jax.nn.sigmoid(ag_ref[...])

    for g in range(NSA_KV_HEADS):
        kc = comp_ref[:, g * LANE:g * LANE + HD].astype(bf16)
        vc = comp_ref[:, g * LANE + HD:(g + 1) * LANE].astype(bf16)
        qs = [q_ref[:, (g * HPG + h) * HD:(g * HPG + h + 1) * HD].astype(bf16) for h in range(HPG)]
        psum = jnp.zeros((tq, NC), f32)
        o_c = []
        for h in range(HPG):
            s = _dot_nt(qs[h], kc) * scale
            p = _softmax_rows(jnp.where(c_ok, s, NEG), c_ok)
            psum = psum + p
            o_c.append(_dot(p.astype(bf16), vc))
        imp = _split_dot(psum, ov_ref[...])
        sel = _select_blocks(imp, s_ok, forced, n_sel, jj)
        t_ok = (_dot(sel.astype(bf16), ex_ref[...]) > 0.5) & causal
        ks = ksT_ref[g * HD:(g + 1) * HD, :].astype(bf16)
        vs = vsT_ref[g * HD:(g + 1) * HD, :].astype(bf16)
        kw = kwT_ref[g * HD:(g + 1) * HD, pl.ds(wstart, wk)].astype(bf16)
        vw = vwT_ref[g * HD:(g + 1) * HD, pl.ds(wstart, wk)].astype(bf16)
        for h in range(HPG):
            hh = g * HPG + h
            s = _dot(qs[h], ks) * scale
            p = _softmax_rows(jnp.where(t_ok, s, NEG), t_ok)
            o_s = _dot_nt(p.astype(bf16), vs)
            s = _dot(qs[h], kw) * scale
            p = _softmax_rows(jnp.where(w_ok, s, NEG), w_ok)
            o_w = _dot_nt(p.astype(bf16), vw)
            o = (sg[:, 3 * hh:3 * hh + 1] * o_c[h] + sg[:, 3 * hh + 1:3 * hh + 2] * o_s
                 + sg[:, 3 * hh + 2:3 * hh + 3] * o_w)
            o_ref[:, hh * HD:(hh + 1) * HD] = o


def _overlap_matrix(nc_rows, n_cmp, sw_cols, n_sel):
    cs = jnp.arange(nc_rows) * CMP_STRIDE
    ss = jnp.arange(sw_cols) * SEL_BLOCK
    ok = (cs[:, None] < ss[None, :] + SEL_BLOCK) & (cs[:, None] + CMP_BLOCK > ss[None, :])
    ok = ok & (jnp.arange(nc_rows)[:, None] < n_cmp) & (jnp.arange(sw_cols)[None, :] < n_sel)
    return ok.astype(bf16)


def nsa_prompt(proj, kvT, comp):
    B, T, _ = proj.shape
    tq = min(128, T)
    n_cmp = T // CMP_STRIDE - (CMP_BLOCK // CMP_STRIDE) + 1
    n_sel = -(-T // SEL_BLOCK)
    NC = comp.shape[1]
    SW = LANE
    assert n_sel <= SW and T % tq == 0
    wk = min(WINDOW + tq, T)
    ov = _overlap_matrix(NC, n_cmp, SW, n_sel)
    ex = (jnp.arange(SW)[:, None] == (jnp.arange(T)[None, :] // SEL_BLOCK)).astype(bf16)
    kv_spec = lambda blk: pl.BlockSpec((None, LANE, T), lambda b, i: (b, blk, 0))
    return pl.pallas_call(
        functools.partial(_nsa_prompt_body, tq=tq, T=T, n_cmp=n_cmp, n_sel=n_sel, wk=wk),
        grid=(B, T // tq),
        in_specs=[pl.BlockSpec((None, tq, A_Q), lambda b, i: (b, i, OFF_AQ // A_Q)),
                  pl.BlockSpec((None, tq, LANE), lambda b, i: (b, i, OFF_AG // LANE)),
                  pl.BlockSpec((None, NC, 2 * LANE), lambda b, i: (b, 0, 0)),
                  kv_spec(2), kv_spec(3), kv_spec(4), kv_spec(5),
                  pl.BlockSpec((NC, SW), lambda b, i: (0, 0)),
                  pl.BlockSpec((SW, T), lambda b, i: (0, 0))],
        out_specs=pl.BlockSpec((None, tq, A_Q), lambda b, i: (b, i, 0)),
        out_shape=jax.ShapeDtypeStruct((B, T, A_Q), f32),
        compiler_params=_cparams(("parallel", "arbitrary")),
        name="nsa_prompt",
    )(proj, proj, comp, kvT, kvT, kvT, kvT, ov, ex)


def _nsa_sample_body(pt_ref, q_ref, ag_ref, comp_ref, pg_ref, kwT_ref, newT_ref, ov_ref, rsum_ref, rexp_ref, o_ref,
                     sel_scr, m_scr, l_scr, acc_scr, ocw_scr, *, tn, past, n_cmp, n_sel):
    p = pl.program_id(1)
    scale = HD ** -0.5
    R = HPG * tn
    NC = comp_ref.shape[0]
    SW = ov_ref.shape[1]
    tok = lax.rem(lax.broadcasted_iota(jnp.int32, (R, 1), 0), tn)
    qpos = past + tok
    page = pg_ref.shape[1]
    bpp = page // SEL_BLOCK

    @pl.when(p == 0)
    def _():
        col_n = lax.broadcasted_iota(jnp.int32, (R, NC), 1)
        c_ok = ((col_n * CMP_STRIDE + (CMP_BLOCK - 1)) <= qpos) & (col_n < n_cmp)
        jj = lax.broadcasted_iota(jnp.int32, (tn, SW), 1)
        qpos_t = past + lax.broadcasted_iota(jnp.int32, (tn, 1), 0)
        cur = jnp.right_shift(qpos_t, SEL_SHIFT)
        s_ok = (jj * SEL_BLOCK <= qpos_t) & (jj < n_sel)
        forced = (jj == 0) | (jj == cur) | (jj == cur - 1)
        wlen = kwT_ref.shape[1]
        kposw = (past - wlen) + lax.broadcasted_iota(jnp.int32, (R, wlen), 1)
        relw = qpos - kposw
        w_ok = (relw >= 0) & (relw < WINDOW)
        nl = newT_ref.shape[1]
        u = lax.broadcasted_iota(jnp.int32, (R, nl), 1)
        reln = tok - u
        n_ok = (reln >= 0) & (reln < WINDOW) & (u < tn)
        sg = jax.nn.sigmoid(ag_ref[...])
        for g in range(NSA_KV_HEADS):
            q = q_ref[g * R:(g + 1) * R, :].astype(bf16)
            kc = comp_ref[:, g * LANE:g * LANE + HD].astype(bf16)
            vc = comp_ref[:, g * LANE + HD:(g + 1) * LANE].astype(bf16)
            s = _dot_nt(q, kc) * scale
            pc = _softmax_rows(jnp.where(c_ok, s, NEG), c_ok)
            o_c = _dot(pc.astype(bf16), vc)
            psum = _rsum_exact(rsum_ref, pc)
            imp = _split_dot(psum, ov_ref[...])
            sel = _select_blocks(imp, s_ok, forced, n_sel, jj)
            sel_scr[g] = _dot(rexp_ref[...], sel.astype(bf16))
            kwp = kwT_ref[g * HD:(g + 1) * HD, :].astype(bf16)
            vwp = kwT_ref[2 * HD + g * HD:2 * HD + (g + 1) * HD, :].astype(bf16)
            base = 2 * KV_ROWS
            kwn = newT_ref[base + g * HD:base + (g + 1) * HD, :].astype(bf16)
            vwn = newT_ref[base + 2 * HD + g * HD:base + 2 * HD + (g + 1) * HD, :].astype(bf16)
            s1 = jnp.where(w_ok, _dot(q, kwp) * scale, NEG)
            s2 = jnp.where(n_ok, _dot(q, kwn) * scale, NEG)
            m = jnp.maximum(jnp.max(s1, axis=-1, keepdims=True), jnp.max(s2, axis=-1, keepdims=True))
            e1 = jnp.where(w_ok, jnp.exp(s1 - m), 0.0)
            e2 = jnp.where(n_ok, jnp.exp(s2 - m), 0.0)
            den = jnp.sum(e1, axis=-1, keepdims=True) + jnp.sum(e2, axis=-1, keepdims=True)
            o_w = (_dot_nt(e1.astype(bf16), vwp) + _dot_nt(e2.astype(bf16), vwn)) / den
            ocw_scr[g] = sg[g * R:(g + 1) * R, 0:1] * o_c + sg[g * R:(g + 1) * R, 2:3] * o_w
            base = KV_ROWS
            ksn = newT_ref[base + g * HD:base + (g + 1) * HD, :].astype(bf16)
            vsn = newT_ref[base + 2 * HD + g * HD:base + 2 * HD + (g + 1) * HD, :].astype(bf16)
            selr = sel_scr[g]
            ok = (u <= tok) & (u < tn)
            if past // SEL_BLOCK < SW:
                ok = ok & (selr[:, past // SEL_BLOCK:past // SEL_BLOCK + 1] > 0.5)
            s3 = jnp.where(ok, _dot(q, ksn) * scale, NEG)
            m3 = jnp.max(s3, axis=-1, keepdims=True)
            e3 = jnp.where(ok, jnp.exp(s3 - m3), 0.0)
            m_scr[g] = m3
            l_scr[g] = jnp.sum(e3, axis=-1, keepdims=True)
            acc_scr[g] = _dot_nt(e3.astype(bf16), vsn)

    jrow = lax.broadcasted_iota(jnp.int32, (SW, page), 0)
    kcol = lax.broadcasted_iota(jnp.int32, (SW, page), 1)
    ex = jnp.where(jrow == p * bpp + jnp.right_shift(kcol, SEL_SHIFT), 1.0, 0.0).astype(bf16)
    for g in range(NSA_KV_HEADS):
        q = q_ref[g * R:(g + 1) * R, :].astype(bf16)
        ok = _dot(sel_scr[g].astype(bf16), ex) > 0.5
        ks = pg_ref[g * HD:(g + 1) * HD, :].astype(bf16)
        vs = pg_ref[2 * HD + g * HD:2 * HD + (g + 1) * HD, :].astype(bf16)
        s = jnp.where(ok, _dot(q, ks) * scale, NEG)
        m_old = m_scr[g]
        m_new = jnp.maximum(m_old, jnp.max(s, axis=-1, keepdims=True))
        e = jnp.where(ok, jnp.exp(s - m_new), 0.0)
        alpha = jnp.exp(m_old - m_new)
        l_scr[g] = alpha * l_scr[g] + jnp.sum(e, axis=-1, keepdims=True)
        acc_scr[g] = alpha * acc_scr[g] + _dot_nt(e.astype(bf16), vs)
        m_scr[g] = m_new

    @pl.when(p == pl.num_programs(1) - 1)
    def _():
        sg = jax.nn.sigmoid(ag_ref[...])
        for g in range(NSA_KV_HEADS):
            o_s = acc_scr[g] / l_scr[g]
            o_ref[g * R:(g + 1) * R, :] = ocw_scr[g] + sg[g * R:(g + 1) * R, 1:2] * o_s


def _rsum_exact(rsum_ref, pc):
    r = rsum_ref[...]
    hi = pc.astype(bf16)
    lo = (pc - hi.astype(f32)).astype(bf16)
    return _dot(r, hi) + _dot(r, lo)


def nsa_sample(q_s, ag_s, comp, selT, layer, page_table, winT, newT, *, tn, past):
    Bd, n_pages = page_table.shape
    page = selT.shape[-1]
    R = HPG * tn
    L = past + tn
    n_cmp = L // CMP_STRIDE - (CMP_BLOCK // CMP_STRIDE) + 1
    n_sel = -(-L // SEL_BLOCK)
    NC = comp.shape[1]
    SW = -(-n_sel // LANE) * LANE
    assert past % page == 0 and page % SEL_BLOCK == 0 and tn <= SEL_BLOCK and n_cmp <= NC
    wlen = winT.shape[-1]
    ov = _overlap_matrix(NC, n_cmp, SW, n_sel)
    rsum = (jnp.arange(tn)[:, None] == (jnp.arange(R)[None, :] % tn)).astype(bf16)
    const = lambda *shape: pl.BlockSpec(shape, lambda b, p, pt: (0,) * len(shape))
    return pl.pallas_call(
        functools.partial(_nsa_sample_body, tn=tn, past=past, n_cmp=n_cmp, n_sel=n_sel),
        grid_spec=pltpu.PrefetchScalarGridSpec(
            num_scalar_prefetch=1,
            grid=(Bd, n_pages),
            in_specs=[pl.BlockSpec((None, 2 * R, HD), lambda b, p, pt: (b, 0, 0)),
                      pl.BlockSpec((None, 2 * R, 3), lambda b, p, pt: (b, 0, 0)),
                      pl.BlockSpec((None, NC, 2 * LANE), lambda b, p, pt: (b, 0, 0)),
                      pl.BlockSpec((None, None, KV_ROWS, page), lambda b, p, pt: (layer, pt[b, p], 0, 0)),
                      pl.BlockSpec((None, KV_ROWS, wlen), lambda b, p, pt: (b, 0, 0)),
                      pl.BlockSpec((None, A_KV, LANE), lambda b, p, pt: (b, 0, 0)),
                      const(NC, SW), const(tn, R), const(R, tn)],
            out_specs=pl.BlockSpec((None, 2 * R, HD), lambda b, p, pt: (b, 0, 0)),
            scratch_shapes=[pltpu.VMEM((NSA_KV_HEADS, R, SW), f32), pltpu.VMEM((NSA_KV_HEADS, R, 1), f32),
                            pltpu.VMEM((NSA_KV_HEADS, R, 1), f32), pltpu.VMEM((NSA_KV_HEADS, R, HD), f32),
                            pltpu.VMEM((NSA_KV_HEADS, R, HD), f32)],
        ),
        out_shape=jax.ShapeDtypeStruct((Bd, 2 * R, HD), f32),
        compiler_params=_cparams(("parallel", "arbitrary")),
        name="nsa_sample",
    )(page_table, q_s, ag_s, comp, selT, winT, newT, ov, rsum, rsum.T)


def _retention_body(q_ref, k_ref, v_ref, cos_ref, sin_ref, dm_ref, qd_ref, kd_ref, sd_ref, s0_ref, o_ref, sn_ref,
                    s_scr):
    c = pl.program_id(2)

    @pl.when(c == 0)
    def _():
        s_scr[...] = s0_ref[...]

    cos = cos_ref[...]
    sin = sin_ref[...]

    def rot(x):
        return x * cos + pltpu.roll(x, RET_DK // 2, axis=1) * sin

    q = rot(q_ref[...])
    k = rot(k_ref[...]) * (RET_DK ** -0.5)
    v = v_ref[...].astype(bf16)
    S = s_scr[...]
    inner = _dot_nt(q.astype(bf16), k.astype(bf16)) * dm_ref[...]
    o_ref[...] = _dot(inner.astype(bf16), v) + _dot((q * qd_ref[...]).astype(bf16), S.astype(bf16))
    S = S * sd_ref[...] + _dot_tn((k * kd_ref[...]).astype(bf16), v)
    s_scr[...] = S

    @pl.when(c == pl.num_programs(2) - 1)
    def _():
        sn_ref[...] = S


def retention(proj, pos0, state0):
    B, T, _ = proj.shape
    C = RET_CHUNK if T % RET_CHUNK == 0 else T
    n = T // C
    H = RET_HEADS
    half = RET_DK // 2
    inv = jnp.exp(-jnp.log(ROPE_BASE) * jnp.arange(half, dtype=f32) / half)
    ang = (pos0 + jnp.arange(T)).astype(f32)[:, None] * inv[None, :]
    cos = jnp.concatenate([jnp.cos(ang), jnp.cos(ang)], axis=-1)
    sin = jnp.concatenate([-jnp.sin(ang), jnp.sin(ang)], axis=-1)
    log_g = jnp.log(1.0 - jnp.exp2(-5.0 - jnp.arange(H, dtype=f32)))
    i = jnp.arange(C, dtype=f32)
    diff = i[:, None] - i[None, :]
    dm = jnp.where(diff >= 0, jnp.exp(jnp.maximum(diff, 0.0)[None] * log_g[:, None, None]), 0.0)
    qd = jnp.exp((i + 1.0)[None, :] * log_g[:, None])[..., None]
    kd = jnp.exp((C - 1.0 - i)[None, :] * log_g[:, None])[..., None]
    sd = jnp.exp(C * log_g)[:, None, None]
    tab = lambda last: pl.BlockSpec((None, C, last), lambda b, h, c: (h, 0, 0))
    st = pl.BlockSpec((None, None, RET_DK, RET_DV), lambda b, h, c: (b, h, 0, 0))
    return pl.pallas_call(
        _retention_body,
        grid=(B, H, n),
        in_specs=[pl.BlockSpec((None, C, RET_DK), lambda b, h, c: (b, c, OFF_RQ // RET_DK + h)),
                  pl.BlockSpec((None, C, RET_DK), lambda b, h, c: (b, c, OFF_RK // RET_DK + h)),
                  pl.BlockSpec((None, C, RET_DV), lambda b, h, c: (b, c, OFF_RV // RET_DV + h)),
                  pl.BlockSpec((C, RET_DK), lambda b, h, c: (c, 0)),
                  pl.BlockSpec((C, RET_DK), lambda b, h, c: (c, 0)),
                  tab(C), tab(1), tab(1),
                  pl.BlockSpec((None, 1, 1), lambda b, h, c: (h, 0, 0)), st],
        out_specs=[pl.BlockSpec((None, C, RET_DV), lambda b, h, c: (b, c, h)), st],
        out_shape=[jax.ShapeDtypeStruct((B, T, B_V), f32), jax.ShapeDtypeStruct((B, H, RET_DK, RET_DV), f32)],
        scratch_shapes=[pltpu.VMEM((RET_DK, RET_DV), f32)],
        compiler_params=_cparams(("parallel", "parallel", "arbitrary")),
        name="retention",
    )(proj, proj, proj, cos, sin, dm, qd, kd, sd, state0)


def _mix_out_body(rg_ref, ga_ref, gb_ref, attn_ref, ro_ref, x_ref, g1_ref, woa_ref, gnw_ref, wob_ref, wo_ref, o_ref):
    ya = _dot(attn_ref[...].astype(bf16), woa_ref[...])
    parts = []
    for h in range(RET_HEADS):
        r = ro_ref[:, h * RET_DV:(h + 1) * RET_DV]
        d = r - jnp.mean(r, axis=-1, keepdims=True)
        var = jnp.mean(d * d, axis=-1, keepdims=True)
        parts.append(d * lax.rsqrt(var + GN_EPS) * gnw_ref[:, h * RET_DV:(h + 1) * RET_DV])
    ron = jnp.concatenate(parts, axis=-1)
    rg = rg_ref[...]
    yb = _dot((rg * jax.nn.sigmoid(rg) * ron).astype(bf16), wob_ref[...])
    m = jax.nn.sigmoid(ga_ref[...]) * ya + jax.nn.sigmoid(gb_ref[...]) * yb
    o_ref[...] = x_ref[...] + g1_ref[...] * _dot(m.astype(bf16), wo_ref[...])


def _mod_spec(mod, tm, D):
    if mod.shape[1] != 1:
        return pl.BlockSpec((None, tm, D), lambda g, i: (g, i, 0))
    return pl.BlockSpec((None, 1, D), lambda g, i: (g, 0, 0))


def mix_out(proj, attn, ro, x, g1, w_oa, gn_w, w_ob, w_o, *, tm):
    G, T, D = x.shape
    full = lambda a: pl.BlockSpec(a.shape, lambda g, i: (0,) * a.ndim)
    wide = lambda blk: pl.BlockSpec((None, tm, B_V), lambda g, i: (g, i, blk))
    gn_w = gn_w.reshape(1, B_V)
    return pl.pallas_call(
        _mix_out_body,
        grid=(G, T // tm),
        in_specs=[wide(OFF_RG // B_V), wide(OFF_GA // B_V), wide(OFF_GB // B_V),
                  pl.BlockSpec((None, tm, A_Q), lambda g, i: (g, i, 0)),
                  wide(0),
                  pl.BlockSpec((None, tm, D), lambda g, i: (g, i, 0)),
                  _mod_spec(g1, tm, D), full(w_oa), full(gn_w), full(w_ob), full(w_o)],
        out_specs=pl.BlockSpec((None, tm, D), lambda g, i: (g, i, 0)),
        out_shape=jax.ShapeDtypeStruct((G, T, D), f32),
        compiler_params=_cparams(("parallel", "parallel")),
        name="mix_out",
    )(proj, proj, proj, attn, ro, x, g1, w_oa, gn_w, w_ob, w_o)


def _ffn_tail(a, am1, am2, b_ref, cw_ref, cb_ref, wout_ref, x_ref, g2_ref, nf_ref, o_ref, final_norm):
    u = cb_ref[...] + am2 * cw_ref[0:1, :] + am1 * cw_ref[1:2, :] + a * cw_ref[2:3, :]
    y = _dot((jax.nn.gelu(u) * b_ref[...]).astype(bf16), wout_ref[...])
    xo = x_ref[...] + g2_ref[...] * y
    if final_norm:
        xo = xo * lax.rsqrt(jnp.mean(xo * xo, axis=-1, keepdims=True) + RMS_EPS) * nf_ref[...]
    o_ref[...] = xo


def _ffn_out_seq_body(a_ref, b_ref, halo_ref, prev_ref, cw_ref, cb_ref, wout_ref, x_ref, g2_ref, nf_ref, o_ref, scr,
                      *, final_norm):
    tm = a_ref.shape[0]
    a = a_ref[...]
    scr[8:8 + tm, :] = a
    scr[6:8, :] = jnp.where(pl.program_id(1) == 0, prev_ref[...], halo_ref[6:8, :])
    _ffn_tail(a, scr[pl.ds(7, tm), :], scr[pl.ds(6, tm), :], b_ref, cw_ref, cb_ref, wout_ref, x_ref, g2_ref, nf_ref,
              o_ref, final_norm)


def _ffn_out_rows_body(a_ref, am1_ref, am2_ref, b_ref, cw_ref, cb_ref, wout_ref, x_ref, g2_ref, nf_ref, o_ref,
                       *, final_norm):
    _ffn_tail(a_ref[...], am1_ref[...], am2_ref[...], b_ref, cw_ref, cb_ref, wout_ref, x_ref, g2_ref, nf_ref, o_ref,
              final_norm)


def ffn_out_seq(ab, prev, x, g2, conv_w, conv_b, w_out, normf_w, *, tm, final_norm):
    B, T, D = x.shape
    F = w_out.shape[0]
    full = lambda a: pl.BlockSpec(a.shape, lambda g, i: (0,) * a.ndim)
    conv_b = conv_b.reshape(1, F)
    normf_w = normf_w.reshape(1, D)
    hb = tm // 8
    return pl.pallas_call(
        functools.partial(_ffn_out_seq_body, final_norm=final_norm),
        grid=(B, T // tm),
        in_specs=[pl.BlockSpec((None, tm, F), lambda g, i: (g, i, 0)),
                  pl.BlockSpec((None, tm, F), lambda g, i: (g, i, 1)),
                  pl.BlockSpec((None, 8, F), lambda g, i: (g, jnp.maximum(i * hb - 1, 0), 0)),
                  pl.BlockSpec((None, CONV_W - 1, F), lambda g, i: (g, 0, 0)),
                  full(conv_w), full(conv_b), full(w_out),
                  pl.BlockSpec((None, tm, D), lambda g, i: (g, i, 0)),
                  _mod_spec(g2, tm, D), full(normf_w)],
        out_specs=pl.BlockSpec((None, tm, D), lambda g, i: (g, i, 0)),
        out_shape=jax.ShapeDtypeStruct((B, T, D), f32),
        scratch_shapes=[pltpu.VMEM((tm + 8, F), f32)],
        compiler_params=_cparams(("parallel", "arbitrary")),
        name="ffn_out_seq",
    )(ab, ab, ab, prev, conv_w, conv_b, w_out, x, g2, normf_w)


def ffn_out_rows(ab, am1, am2, x, g2, conv_w, conv_b, w_out, normf_w, *, tm, final_norm):
    G, T, D = x.shape
    F = w_out.shape[0]
    full = lambda a: pl.BlockSpec(a.shape, lambda g, i: (0,) * a.ndim)
    conv_b = conv_b.reshape(1, F)
    normf_w = normf_w.reshape(1, D)
    rowsF = lambda blk: pl.BlockSpec((None, tm, F), lambda g, i: (g, i, blk))
    return pl.pallas_call(
        functools.partial(_ffn_out_rows_body, final_norm=final_norm),
        grid=(G, T // tm),
        in_specs=[rowsF(0), rowsF(0), rowsF(0), rowsF(1),
                  full(conv_w), full(conv_b), full(w_out),
                  pl.BlockSpec((None, tm, D), lambda g, i: (g, i, 0)),
                  _mod_spec(g2, tm, D), full(normf_w)],
        out_specs=pl.BlockSpec((None, tm, D), lambda g, i: (g, i, 0)),
        out_shape=jax.ShapeDtypeStruct((G, T, D), f32),
        compiler_params=_cparams(("parallel", "parallel")),
        name="ffn_out_rows",
    )(ab, am1, am2, ab, conv_w, conv_b, w_out, x, g2, normf_w)


def _prep_w_in(w):
    D = w.shape[0]
    o = 0
    parts = {}
    for name, n in (("aq", A_Q), ("akv", A_KV), ("ag", A_G), ("rq", B_QK), ("rk", B_QK), ("rv", B_V), ("rg", B_V),
                    ("ga", D), ("gb", D)):
        parts[name] = w[:, o:o + n]
        o += n
    pad = jnp.zeros((D, N_TOK - OFF_AG - A_G), w.dtype)
    tok = jnp.concatenate([parts["rg"], parts["ga"], parts["gb"], parts["rv"], parts["aq"], parts["rq"], parts["rk"],
                           parts["akv"][:, :KV_ROWS], parts["ag"], pad], axis=1)
    return tok.astype(bf16), parts["akv"].T.astype(bf16)


def _prep_cmp(cmp_pos, cmp_w1, cmp_w2):
    G = NSA_KV_HEADS
    pos = jnp.concatenate([cmp_pos] * G, axis=-1)
    w1 = cmp_w1.reshape(2, CMP_BLOCK, HD, CMP_HIDDEN)
    eye = jnp.eye(G, dtype=w1.dtype)
    w1bd = jnp.einsum("eldh,gk->elgdkh", w1, eye).reshape(2, CMP_BLOCK, G * HD, G * CMP_HIDDEN)
    eye_e = jnp.eye(2, dtype=w1.dtype)
    w2bd = jnp.einsum("ehd,gk,ef->eghkfd", cmp_w2, eye, eye_e).reshape(2, G * CMP_HIDDEN, G * 2 * HD)
    return pos, w1bd.astype(bf16), w2bd.astype(bf16)


def _kv_rows_to_out(rowsT, lead):
    t = rowsT.shape[-1]
    r = rowsT.reshape(*lead, 2, NSA_KV_HEADS, HD, t)
    n = len(lead)
    return r.transpose(*range(n), n + 3, n, n + 1, n + 2)


def _keep_last_lanes(a, n):
    t = a.shape[-1]
    if t >= n:
        return a[..., t - n:]
    return jnp.pad(a, ((0, 0),) * (a.ndim - 1) + ((n - t, 0),))


def kernel(x_prompt, x_sample, c_prompt, c_sample, cache_cmp_kv, cache_sel_kv, state_win_kv, state_ret, state_conv,
           page_table, norm1_w, ada_w, ada_b, w_in, cmp_pos, cmp_w1, cmp_w2, w_oa, ret_gn_w, w_ob, w_o, norm2_w,
           ffn_w_in, ffn_conv_w, ffn_conv_b, ffn_w_out, normf_w):
    B, T, D = x_prompt.shape
    Bd, Td, _ = x_sample.shape
    depth = w_in.shape[0]
    n_phys, page = cache_cmp_kv.shape[1], cache_cmp_kv.shape[2]
    past = page_table.shape[1] * page
    wlen = state_win_kv.shape[2]
    F = ffn_w_out.shape[1]
    Rs = Bd * Td
    tm_p = min(512, T)

    mod = ada_mod(jnp.concatenate([c_prompt, c_sample], axis=0), ada_w, ada_b)
    cmpT = cache_cmp_kv.transpose(0, 1, 3, 4, 5, 2).reshape(depth, n_phys, KV_ROWS, page)
    selT = cache_sel_kv.transpose(0, 1, 3, 4, 5, 2).reshape(depth, n_phys, KV_ROWS, page)
    winT = state_win_kv.transpose(0, 1, 3, 4, 5, 2).reshape(depth, Bd, KV_ROWS, wlen)

    xp = x_prompt
    xs = x_sample.reshape(1, Rs, D)
    outs = [[] for _ in range(10)]
    for l in range(depth):
        w_tok, w_kvT = _prep_w_in(w_in[l])
        cw = _prep_cmp(cmp_pos[l], cmp_w1[l], cmp_w2[l])
        w_oa_b, w_ob_b, w_o_b = w_oa[l].astype(bf16), w_ob[l].astype(bf16), w_o[l].astype(bf16)
        ffn_in_b, ffn_out_b = ffn_w_in[l].astype(bf16), ffn_w_out[l].astype(bf16)
        last = l == depth - 1
        mp = [mod[l, :B, k * D:(k + 1) * D].reshape(B, 1, D) for k in range(6)]
        ms = [jnp.repeat(mod[l, B:, k * D:(k + 1) * D], Td, axis=0).reshape(1, Rs, D) for k in range(6)]

        proj, kvT = norm_proj(xp, mp[1], mp[0], norm1_w[l], w_tok, w_kvT, tm=tm_p, tn=1024)
        comp = compress_prompt(proj, cw)
        attn = nsa_prompt(proj, kvT, comp)
        ro, ret_p = retention(proj, 0, jnp.zeros((B, RET_HEADS, RET_DK, RET_DV), f32))
        x1 = mix_out(proj, attn, ro, xp, mp[2], w_oa_b, ret_gn_w[l], w_ob_b, w_o_b, tm=tm_p)
        ab = norm_proj(x1, mp[4], mp[3], norm2_w[l], ffn_in_b, None, tm=tm_p, tn=F)
        xp = ffn_out_seq(ab, jnp.zeros((B, CONV_W - 1, F), f32), x1, mp[5], ffn_conv_w[l], ffn_conv_b[l], ffn_out_b,
                         normf_w, tm=min(256, T), final_norm=last)
        outs[0].append(_kv_rows_to_out(kvT[:, 0:KV_ROWS], (B,)))
        outs[2].append(_kv_rows_to_out(kvT[:, KV_ROWS:2 * KV_ROWS], (B,)))
        outs[4].append(_kv_rows_to_out(_keep_last_lanes(kvT[:, 2 * KV_ROWS:], wlen), (B,)))
        outs[6].append(ret_p)
        outs[8].append(ab[:, T - (CONV_W - 1):, :F])

        proj_s, kvT_s = norm_proj(xs, ms[1], ms[0], norm1_w[l], w_tok, w_kvT, tm=Rs, tn=1024)
        comp_s = compress_paged(cmpT, l, page_table, cw)
        q_s = proj_s[0, :, OFF_AQ:OFF_AQ + A_Q].reshape(Bd, Td, NSA_KV_HEADS, HPG, HD)
        q_s = q_s.transpose(0, 2, 3, 1, 4).reshape(Bd, NSA_HEADS * Td, HD)
        ag_s = proj_s[0, :, OFF_AG:OFF_AG + A_G].reshape(Bd, Td, NSA_KV_HEADS, HPG, 3)
        ag_s = ag_s.transpose(0, 2, 3, 1, 4).reshape(Bd, NSA_HEADS * Td, 3)
        newT = kvT_s[0].reshape(A_KV, Bd, Td).transpose(1, 0, 2)
        newT_pad = jnp.pad(newT, ((0, 0), (0, 0), (0, LANE - Td)))
        attn_s = nsa_sample(q_s, ag_s, comp_s, selT, l, page_table, winT[l], newT_pad, tn=Td, past=past)
        attn_s = attn_s.reshape(Bd, NSA_KV_HEADS, HPG, Td, HD).transpose(0, 3, 1, 2, 4).reshape(1, Rs, A_Q)
        ro_s, ret_s = retention(proj_s.reshape(Bd, Td, N_TOK), past, state_ret[l])
        x1s = mix_out(proj_s, attn_s, ro_s.reshape(1, Rs, B_V), xs, ms[2], w_oa_b, ret_gn_w[l], w_ob_b, w_o_b, tm=Rs)
        ab_s = norm_proj(x1s, ms[4], ms[3], norm2_w[l], ffn_in_b, None, tm=Rs, tn=F)
        a_ext = jnp.concatenate([state_conv[l], ab_s[0, :, :F].reshape(Bd, Td, F)], axis=1)
        am1 = a_ext[:, 1:1 + Td].reshape(1, Rs, F)
        am2 = a_ext[:, 0:Td].reshape(1, Rs, F)
        xs = ffn_out_rows(ab_s, am1, am2, x1s, ms[5], ffn_conv_w[l], ffn_conv_b[l], ffn_out_b, normf_w, tm=Rs,
                          final_norm=last)
        outs[1].append(_kv_rows_to_out(newT[:, 0:KV_ROWS], (Bd,)))
        outs[3].append(_kv_rows_to_out(newT[:, KV_ROWS:2 * KV_ROWS], (Bd,)))
        win_all = jnp.concatenate([winT[l], newT[:, 2 * KV_ROWS:]], axis=-1)
        outs[5].append(_kv_rows_to_out(_keep_last_lanes(win_all, wlen), (Bd,)))
        outs[7].append(ret_s)
        outs[9].append(a_ext[:, Td:])

    st = lambda k: jnp.stack(outs[k])
    return (xp, xs.reshape(Bd, Td, D), st(0), st(1), st(2), st(3), st(4), st(5), st(6), st(7), st(8), st(9))
```

```python
import functools

import jax
import jax.numpy as jnp
from jax import lax
from jax.experimental import pallas as pl
from jax.experimental.pallas import tpu as pltpu

f32 = jnp.float32
bf16 = jnp.bfloat16

NSA_HEADS = 8
NSA_KV_HEADS = 2
HPG = NSA_HEADS // NSA_KV_HEADS
HD = 64
CMP_STRIDE = 16
CMP_BLOCK = 32
CMP_HIDDEN = 64
SEL_BLOCK = 64
SEL_SHIFT = 6
N_SEL = 16
WINDOW = 512
RET_HEADS = 4
RET_DK = 128
RET_DV = 256
RET_CHUNK = 128
ROPE_BASE = 10000.0
CONV_W = 3
RMS_EPS = 1e-6
GN_EPS = 1e-5
NEG = -1e30
FORCE_BONUS = 1e4

A_Q = NSA_HEADS * HD
KV_ROWS = 2 * NSA_KV_HEADS * HD
A_KV = 3 * KV_ROWS
A_G = 3 * NSA_HEADS
B_QK = RET_HEADS * RET_DK
B_V = RET_HEADS * RET_DV

OFF_RG, OFF_GA, OFF_GB, OFF_RV = 0, 1024, 2048, 3072
OFF_AQ, OFF_RQ, OFF_RK, OFF_KVC, OFF_AG = 4096, 4608, 5120, 5632, 5888
N_TOK = 6144

LANE = 128
VMEM_LIMIT = 56 * 1024 * 1024


def _cparams(sem):
    return pltpu.CompilerParams(dimension_semantics=sem, vmem_limit_bytes=VMEM_LIMIT)


def _dot(a, b):
    return jnp.dot(a, b, preferred_element_type=f32)


def _dot_nt(a, b):
    return lax.dot_general(a, b, (((1,), (1,)), ((), ())), preferred_element_type=f32)


def _dot_tn(a, b):
    return lax.dot_general(a, b, (((0,), (0,)), ((), ())), preferred_element_type=f32)


def _split_dot(a, b_bf16):
    hi = a.astype(bf16)
    lo = (a - hi.astype(f32)).astype(bf16)
    return _dot(hi, b_bf16) + _dot(lo, b_bf16)


def _ada_body(c_ref, w_ref, b_ref, o_ref):
    c = c_ref[...]
    s = c * jax.nn.sigmoid(c)
    o_ref[...] = _dot(s.astype(bf16), w_ref[...].astype(bf16)) + b_ref[...]


def ada_mod(c_all, ada_w, ada_b):
    depth, d, n = ada_w.shape
    r = c_all.shape[0]
    tn = 1024
    return pl.pallas_call(
        _ada_body,
        grid=(depth, n // tn),
        in_specs=[pl.BlockSpec((r, d), lambda l, j: (0, 0)),
                  pl.BlockSpec((None, d, tn), lambda l, j: (l, 0, j)),
                  pl.BlockSpec((None, 1, tn), lambda l, j: (l, 0, j))],
        out_specs=pl.BlockSpec((None, r, tn), lambda l, j: (l, 0, j)),
        out_shape=jax.ShapeDtypeStruct((depth, r, n), f32),
        compiler_params=_cparams(("parallel", "parallel")),
        name="ada_mod",
    )(c_all, ada_w, ada_b.reshape(depth, 1, n))


def _norm_proj_body(x_ref, sc_ref, sh_ref, nw_ref, w_ref, *rest, has_t):
    if has_t:
        wt_ref, o_ref, ot_ref, h_scr = rest
    else:
        o_ref, h_scr = rest

    @pl.when(pl.program_id(2) == 0)
    def _():
        x = x_ref[...]
        y = x * lax.rsqrt(jnp.mean(x * x, axis=-1, keepdims=True) + RMS_EPS) * nw_ref[...]
        hb = (y * (1.0 + sc_ref[...]) + sh_ref[...]).astype(bf16)
        h_scr[...] = hb
        if has_t:
            ot_ref[...] = _dot_nt(wt_ref[...], hb)

    o_ref[...] = _dot(h_scr[...], w_ref[...])


def norm_proj(x, sc, sh, nw, w, wt, *, tm, tn):
    G, T, D = x.shape
    N = w.shape[1]
    per_row = sc.shape[1] != 1
    mr = tm if per_row else 1
    mod_spec = pl.BlockSpec((None, mr, D), (lambda g, i, j: (g, i, 0)) if per_row else (lambda g, i, j: (g, 0, 0)))
    in_specs = [pl.BlockSpec((None, tm, D), lambda g, i, j: (g, i, 0)), mod_spec, mod_spec,
                pl.BlockSpec((1, D), lambda g, i, j: (0, 0)),
                pl.BlockSpec((D, tn), lambda g, i, j: (0, j))]
    out_specs = [pl.BlockSpec((None, tm, tn), lambda g, i, j: (g, i, j))]
    out_shape = [jax.ShapeDtypeStruct((G, T, N), f32)]
    args = [x, sc, sh, nw.reshape(1, D), w]
    if wt is not None:
        NT = wt.shape[0]
        in_specs.append(pl.BlockSpec((NT, D), lambda g, i, j: (0, 0)))
        out_specs.append(pl.BlockSpec((None, NT, tm), lambda g, i, j: (g, 0, i)))
        out_shape.append(jax.ShapeDtypeStruct((G, NT, T), f32))
        args.append(wt)
    res = pl.pallas_call(
        functools.partial(_norm_proj_body, has_t=wt is not None),
        grid=(G, T // tm, N // tn),
        in_specs=in_specs, out_specs=out_specs, out_shape=out_shape,
        scratch_shapes=[pltpu.VMEM((tm, D), bf16)],
        compiler_params=_cparams(("parallel", "parallel", "arbitrary")),
        name="norm_proj_t" if wt is not None else "norm_proj",
    )(*args)
    return res if wt is not None else res[0]


def _compress_finish(xs_ref, pos_ref, w1_ref, w2_ref, o_ref):
    n = o_ref.shape[0]
    xs_ref[:, pl.ds(n * CMP_STRIDE, CMP_BLOCK), :] = jnp.zeros((2, CMP_BLOCK, LANE), f32)
    out = jnp.zeros((n, 2 * LANE), f32)
    for e in range(2):
        acc = jnp.zeros((n, LANE), f32)
        for l in range(CMP_BLOCK):
            a = xs_ref[e, pl.ds(l, n, stride=CMP_STRIDE), :] + pos_ref[e, l:l + 1, :]
            acc = acc + _dot(a.astype(bf16), w1_ref[e, l])
        out = out + _dot(jax.nn.gelu(acc).astype(bf16), w2_ref[e])
    o_ref[...] = out


def _compress_prompt_body(x_ref, pos_ref, w1_ref, w2_ref, o_ref, xs_ref):
    T = x_ref.shape[0]
    xs_ref[0, 0:T, :] = x_ref[:, 0:LANE]
    xs_ref[1, 0:T, :] = x_ref[:, LANE:2 * LANE]
    _compress_finish(xs_ref, pos_ref, w1_ref, w2_ref, o_ref)


def _cmp_weight_specs(nargs):
    z3 = (lambda *a: (0, 0, 0))
    z4 = (lambda *a: (0, 0, 0, 0))
    return [pl.BlockSpec((2, CMP_BLOCK, LANE), z3),
            pl.BlockSpec((2, CMP_BLOCK, LANE, LANE), z4),
            pl.BlockSpec((2, LANE, 2 * LANE), z3)]


def compress_prompt(proj, cw):
    B, T, _ = proj.shape
    n = T // CMP_STRIDE
    return pl.pallas_call(
        _compress_prompt_body,
        grid=(B,),
        in_specs=[pl.BlockSpec((None, T, KV_ROWS), lambda b: (b, 0, OFF_KVC // KV_ROWS))] + _cmp_weight_specs(1),
        out_specs=pl.BlockSpec((None, n, 2 * LANE), lambda b: (b, 0, 0)),
        out_shape=jax.ShapeDtypeStruct((B, n, 2 * LANE), f32),
        scratch_shapes=[pltpu.VMEM((2, T + CMP_BLOCK, LANE), f32)],
        compiler_params=_cparams(("parallel",)),
        name="compress_prompt",
    )(proj, *cw)


def _page_specs(layer, n_pages, page):
    return [pl.BlockSpec((None, None, KV_ROWS, page), lambda b, pt, k=k: (layer, pt[b, k], 0, 0))
            for k in range(n_pages)]


def _compress_paged_body(pt_ref, pos_ref, w1_ref, w2_ref, *rest, n_pages):
    pages, o_ref, xs_ref = rest[:n_pages], rest[n_pages], rest[n_pages + 1]
    for k in range(n_pages):
        xs_ref[0, k * LANE:(k + 1) * LANE, :] = pages[k][0:LANE, :].T
        xs_ref[1, k * LANE:(k + 1) * LANE, :] = pages[k][LANE:2 * LANE, :].T
    _compress_finish(xs_ref, pos_ref, w1_ref, w2_ref, o_ref)


def compress_paged(cacheT, layer, page_table, cw):
    Bd, n_pages = page_table.shape
    page = cacheT.shape[-1]
    assert page == LANE
    L = n_pages * page
    n = L // CMP_STRIDE
    return pl.pallas_call(
        functools.partial(_compress_paged_body, n_pages=n_pages),
        grid_spec=pltpu.PrefetchScalarGridSpec(
            num_scalar_prefetch=1,
            grid=(Bd,),
            in_specs=_cmp_weight_specs(2) + _page_specs(layer, n_pages, page),
            out_specs=pl.BlockSpec((None, n, 2 * LANE), lambda b, pt: (b, 0, 0)),
            scratch_shapes=[pltpu.VMEM((2, L + CMP_BLOCK, LANE), f32)],
        ),
        out_shape=jax.ShapeDtypeStruct((Bd, n, 2 * LANE), f32),
        compiler_params=_cparams(("parallel",)),
        name="compress_paged",
    )(page_table, *cw, *([cacheT] * n_pages))


def _select_blocks(imp, s_ok, forced, n_sel, jj):
    k = min(N_SEL, n_sel)
    score = jnp.where(s_ok, imp + jnp.where(forced, FORCE_BONUS, 0.0), NEG)
    rank = jnp.zeros(score.shape, f32)
    for j2 in range(n_sel):
        col = score[:, j2:j2 + 1]
        ge = jnp.where(col >= score, 1.0, 0.0)
        gt = jnp.where(col > score, 1.0, 0.0)
        rank = rank + jnp.where(jj > j2, ge, gt)
    return jnp.where((rank < k) & s_ok, 1.0, 0.0)


def _softmax_rows(sm, ok):
    m = jnp.max(sm, axis=-1, keepdims=True)
    e = jnp.where(ok, jnp.exp(sm - m), 0.0)
    den = jnp.sum(e, axis=-1, keepdims=True)
    return e / jnp.maximum(den, 1e-30)


def _nsa_prompt_body(q_ref, ag_ref, comp_ref, ksT_ref, vsT_ref, kwT_ref, vwT_ref, ov_ref, ex_ref, o_ref,
                     *, tq, n_cmp, n_sel, wk, ck):
    i = pl.program_id(1)
    t0 = i * tq
    scale = HD ** -0.5
    NC = comp_ref.shape[0]
    SW = ov_ref.shape[1]
    qpos = t0 + lax.broadcasted_iota(jnp.int32, (tq, 1), 0)
    col_n = lax.broadcasted_iota(jnp.int32, (tq, NC), 1)
    c_ok = ((col_n * CMP_STRIDE + (CMP_BLOCK - 1)) <= qpos) & (col_n < n_cmp)
    jj = lax.broadcasted_iota(jnp.int32, (tq, SW), 1)
    cur = jnp.right_shift(qpos, SEL_SHIFT)
    s_ok = (jj * SEL_BLOCK <= qpos) & (jj < n_sel)
    forced = (jj == 0) | (jj == cur) | (jj == cur - 1)
    wstart = pl.multiple_of(jnp.maximum(t0 + tq - wk, 0), LANE)
    kposw = wstart + lax.broadcasted_iota(jnp.int32, (tq, wk), 1)
    rel = qpos - kposw
    w_bias4 = jnp.concatenate([jnp.where((rel >= 0) & (rel < WINDOW), 0.0, NEG)] * HPG, axis=0)
    c_ok4 = jnp.concatenate([c_ok] * HPG, axis=0)
    sg = jax.nn.sigmoid(ag_ref[...])
    n_chunks = (t0 + tq + ck - 1) // ck
    kcol = lax.broadcasted_iota(jnp.int32, (tq, ck), 1)
    all_valid_selected = t0 + tq <= min(N_SEL, n_sel) * SEL_BLOCK

    for g in range(NSA_KV_HEADS):
        kc = comp_ref[:, g * LANE:g * LANE + HD].astype(bf16)
        vc = comp_ref[:, g * LANE + HD:(g + 1) * LANE].astype(bf16)
        q4 = jnp.concatenate([(q_ref[:, (g * HPG + h) * HD:(g * HPG + h + 1) * HD] * scale).astype(bf16)
                              for h in range(HPG)], axis=0)
        p = _softmax_rows(jnp.where(c_ok4, _dot_nt(q4, kc), NEG), c_ok4)
        o_c = _dot(p.astype(bf16), vc)
        psum = p[0:tq]
        for h in range(1, HPG):
            psum = psum + p[h * tq:(h + 1) * tq]
        imp = _split_dot(psum, ov_ref[...])
        sel = lax.cond(all_valid_selected,
                       lambda: jnp.where(s_ok, 1.0, 0.0),
                       lambda: _select_blocks(imp, s_ok, forced, n_sel, jj))
        selb = sel.astype(bf16)

        def chunk_step(c, carry):
            k0 = pl.multiple_of(c * ck, ck)
            kT = ksT_ref[g * HD:(g + 1) * HD, pl.ds(k0, ck)].astype(bf16)
            vT = vsT_ref[g * HD:(g + 1) * HD, pl.ds(k0, ck)].astype(bf16)
            ok = (_dot(selb, ex_ref[:, pl.ds(k0, ck)]) > 0.5) & (k0 + kcol <= qpos)
            bias = jnp.where(ok, 0.0, NEG)
            m, l, acc = carry
            s = _dot(q4, kT) + jnp.concatenate([bias] * HPG, axis=0)
            m_new = jnp.maximum(m, jnp.max(s, axis=-1, keepdims=True))
            e = jnp.exp(s - m_new)
            alpha = jnp.exp(m - m_new)
            l = alpha * l + jnp.sum(e, axis=-1, keepdims=True)
            acc = alpha * acc + _dot_nt(e.astype(bf16), vT)
            return m_new, l, acc

        init = (jnp.full((HPG * tq, 1), NEG, f32), jnp.zeros((HPG * tq, 1), f32), jnp.zeros((HPG * tq, HD), f32))
        _, l, acc = lax.fori_loop(0, n_chunks, chunk_step, init)
        o_s = acc / l

        kw = kwT_ref[g * HD:(g + 1) * HD, pl.ds(wstart, wk)].astype(bf16)
        vw = vwT_ref[g * HD:(g + 1) * HD, pl.ds(wstart, wk)].astype(bf16)
        s = _dot(q4, kw) + w_bias4
        e = jnp.exp(s - jnp.max(s, axis=-1, keepdims=True))
        o_w = _dot_nt(e.astype(bf16), vw) / jnp.sum(e, axis=-1, keepdims=True)

        gate = lambda br: jnp.concatenate(
            [sg[:, 3 * (g * HPG + h) + br:3 * (g * HPG + h) + br + 1] for h in range(HPG)], axis=0)
        o = gate(0) * o_c + gate(1) * o_s + gate(2) * o_w
        for h in range(HPG):
            o_ref[:, (g * HPG + h) * HD:(g * HPG + h + 1) * HD] = o[h * tq:(h + 1) * tq]


def _overlap_matrix(nc_rows, n_cmp, sw_cols, n_sel):
    cs = jnp.arange(nc_rows) * CMP_STRIDE
    ss = jnp.arange(sw_cols) * SEL_BLOCK
    ok = (cs[:, None] < ss[None, :] + SEL_BLOCK) & (cs[:, None] + CMP_BLOCK > ss[None, :])
    ok = ok & (jnp.arange(nc_rows)[:, None] < n_cmp) & (jnp.arange(sw_cols)[None, :] < n_sel)
    return ok.astype(bf16)


def nsa_prompt(proj, kvT, comp):
    B, T, _ = proj.shape
    tq = min(128, T)
    n_cmp = T // CMP_STRIDE - (CMP_BLOCK // CMP_STRIDE) + 1
    n_sel = -(-T // SEL_BLOCK)
    NC = comp.shape[1]
    SW = LANE
    assert n_sel <= SW and T % tq == 0
    wk = min(WINDOW + tq, T)
    ck = min(512, T)
    assert T % ck == 0
    ov = _overlap_matrix(NC, n_cmp, SW, n_sel)
    ex = (jnp.arange(SW)[:, None] == (jnp.arange(T)[None, :] // SEL_BLOCK)).astype(bf16)
    kv_spec = lambda blk: pl.BlockSpec((None, LANE, T), lambda b, i: (b, blk, 0))
    return pl.pallas_call(
        functools.partial(_nsa_prompt_body, tq=tq, n_cmp=n_cmp, n_sel=n_sel, wk=wk, ck=ck),
        grid=(B, T // tq),
        in_specs=[pl.BlockSpec((None, tq, A_Q), lambda b, i: (b, i, OFF_AQ // A_Q)),
                  pl.BlockSpec((None, tq, LANE), lambda b, i: (b, i, OFF_AG // LANE)),
                  pl.BlockSpec((None, NC, 2 * LANE), lambda b, i: (b, 0, 0)),
                  kv_spec(2), kv_spec(3), kv_spec(4), kv_spec(5),
                  pl.BlockSpec((NC, SW), lambda b, i: (0, 0)),
                  pl.BlockSpec((SW, T), lambda b, i: (0, 0))],
        out_specs=pl.BlockSpec((None, tq, A_Q), lambda b, i: (b, i, 0)),
        out_shape=jax.ShapeDtypeStruct((B, T, A_Q), f32),
        compiler_params=_cparams(("parallel", "arbitrary")),
        name="nsa_prompt",
    )(proj, proj, comp, kvT, kvT, kvT, kvT, ov, ex)


def _nsa_sample_body(pt_ref, q_ref, ag_ref, comp_ref, kwT_ref, newT_ref, ov_ref, rsum_ref, rexp_ref, ex_ref, *rest,
                     tn, past, n_cmp, n_sel, n_pages):
    pages, o_ref = rest[:n_pages], rest[n_pages]
    scale = HD ** -0.5
    R = HPG * tn
    NC = comp_ref.shape[0]
    SW = ov_ref.shape[1]
    tok = lax.rem(lax.broadcasted_iota(jnp.int32, (R, 1), 0), tn)
    qpos = past + tok
    col_n = lax.broadcasted_iota(jnp.int32, (R, NC), 1)
    c_ok = ((col_n * CMP_STRIDE + (CMP_BLOCK - 1)) <= qpos) & (col_n < n_cmp)
    jj = lax.broadcasted_iota(jnp.int32, (tn, SW), 1)
    qpos_t = past + lax.broadcasted_iota(jnp.int32, (tn, 1), 0)
    cur = jnp.right_shift(qpos_t, SEL_SHIFT)
    s_ok = (jj * SEL_BLOCK <= qpos_t) & (jj < n_sel)
    forced = (jj == 0) | (jj == cur) | (jj == cur - 1)
    wlen = kwT_ref.shape[1]
    kposw = (past - wlen) + lax.broadcasted_iota(jnp.int32, (R, wlen), 1)
    relw = qpos - kposw
    w_ok = (relw >= 0) & (relw < WINDOW)
    nl = newT_ref.shape[1]
    u = lax.broadcasted_iota(jnp.int32, (R, nl), 1)
    reln = tok - u
    n_ok = (reln >= 0) & (reln < WINDOW) & (u < tn)
    nb = past // SEL_BLOCK
    sg = jax.nn.sigmoid(ag_ref[...])
    for g in range(NSA_KV_HEADS):
        q = (q_ref[g * R:(g + 1) * R, :] * scale).astype(bf16)
        kc = comp_ref[:, g * LANE:g * LANE + HD].astype(bf16)
        vc = comp_ref[:, g * LANE + HD:(g + 1) * LANE].astype(bf16)
        pc = _softmax_rows(jnp.where(c_ok, _dot_nt(q, kc), NEG), c_ok)
        o_c = _dot(pc.astype(bf16), vc)
        imp = _split_dot(_rsum_exact(rsum_ref, pc), ov_ref[...])
        sel = _select_blocks(imp, s_ok, forced, n_sel, jj)
        selr = _dot(rexp_ref[...], sel.astype(bf16))
        kwp = kwT_ref[g * HD:(g + 1) * HD, :].astype(bf16)
        vwp = kwT_ref[2 * HD + g * HD:2 * HD + (g + 1) * HD, :].astype(bf16)
        base = 2 * KV_ROWS
        kwn = newT_ref[base + g * HD:base + (g + 1) * HD, :].astype(bf16)
        vwn = newT_ref[base + 2 * HD + g * HD:base + 2 * HD + (g + 1) * HD, :].astype(bf16)
        s1 = jnp.where(w_ok, _dot(q, kwp), NEG)
        s2 = jnp.where(n_ok, _dot(q, kwn), NEG)
        m = jnp.maximum(jnp.max(s1, axis=-1, keepdims=True), jnp.max(s2, axis=-1, keepdims=True))
        e1 = jnp.exp(s1 - m)
        e2 = jnp.exp(s2 - m)
        den = jnp.sum(e1, axis=-1, keepdims=True) + jnp.sum(e2, axis=-1, keepdims=True)
        o_w = (_dot_nt(e1.astype(bf16), vwp) + _dot_nt(e2.astype(bf16), vwn)) / den
        base = KV_ROWS
        ksn = newT_ref[base + g * HD:base + (g + 1) * HD, :].astype(bf16)
        vsn = newT_ref[base + 2 * HD + g * HD:base + 2 * HD + (g + 1) * HD, :].astype(bf16)
        ksp = jnp.concatenate([pg[g * HD:(g + 1) * HD, :] for pg in pages], axis=1).astype(bf16)
        vsp = jnp.concatenate([pg[2 * HD + g * HD:2 * HD + (g + 1) * HD, :] for pg in pages], axis=1).astype(bf16)
        okp = _dot(selr.astype(bf16), ex_ref[...]) > 0.5
        okn = (u <= tok) & (u < tn) & (selr[:, nb:nb + 1] > 0.5)
        s1 = jnp.where(okp, _dot(q, ksp), NEG)
        s2 = jnp.where(okn, _dot(q, ksn), NEG)
        m = jnp.maximum(jnp.max(s1, axis=-1, keepdims=True), jnp.max(s2, axis=-1, keepdims=True))
        e1 = jnp.exp(s1 - m)
        e2 = jnp.exp(s2 - m)
        den = jnp.sum(e1, axis=-1, keepdims=True) + jnp.sum(e2, axis=-1, keepdims=True)
        o_s = (_dot_nt(e1.astype(bf16), vsp) + _dot_nt(e2.astype(bf16), vsn)) / den
        gg = sg[g * R:(g + 1) * R, :]
        o_ref[g * R:(g + 1) * R, :] = gg[:, 0:1] * o_c + gg[:, 1:2] * o_s + gg[:, 2:3] * o_w


def _rsum_exact(rsum_ref, pc):
    r = rsum_ref[...]
    hi = pc.astype(bf16)
    lo = (pc - hi.astype(f32)).astype(bf16)
    return _dot(r, hi) + _dot(r, lo)


def nsa_sample(q_s, ag_s, comp, selT, layer, page_table, winT, newT, *, tn, past):
    Bd, n_pages = page_table.shape
    page = selT.shape[-1]
    R = HPG * tn
    L = past + tn
    n_cmp = L // CMP_STRIDE - (CMP_BLOCK // CMP_STRIDE) + 1
    n_sel = -(-L // SEL_BLOCK)
    NC = comp.shape[1]
    SW = -(-n_sel // LANE) * LANE
    assert past % page == 0 and page % SEL_BLOCK == 0 and tn <= SEL_BLOCK and n_cmp <= NC
    wlen = winT.shape[-1]
    ov = _overlap_matrix(NC, n_cmp, SW, n_sel)
    rsum = (jnp.arange(tn)[:, None] == (jnp.arange(R)[None, :] % tn)).astype(bf16)
    ex = (jnp.arange(SW)[:, None] == (jnp.arange(past)[None, :] // SEL_BLOCK)).astype(bf16)
    const = lambda *shape: pl.BlockSpec(shape, lambda b, pt: (0,) * len(shape))
    return pl.pallas_call(
        functools.partial(_nsa_sample_body, tn=tn, past=past, n_cmp=n_cmp, n_sel=n_sel, n_pages=n_pages),
        grid_spec=pltpu.PrefetchScalarGridSpec(
            num_scalar_prefetch=1,
            grid=(Bd,),
            in_specs=[pl.BlockSpec((None, 2 * R, HD), lambda b, pt: (b, 0, 0)),
                      pl.BlockSpec((None, 2 * R, 3), lambda b, pt: (b, 0, 0)),
                      pl.BlockSpec((None, NC, 2 * LANE), lambda b, pt: (b, 0, 0)),
                      pl.BlockSpec((None, KV_ROWS, wlen), lambda b, pt: (b, 0, 0)),
                      pl.BlockSpec((None, A_KV, LANE), lambda b, pt: (b, 0, 0)),
                      const(NC, SW), const(tn, R), const(R, tn), const(SW, past)]
            + _page_specs(layer, n_pages, page),
            out_specs=pl.BlockSpec((None, 2 * R, HD), lambda b, pt: (b, 0, 0)),
        ),
        out_shape=jax.ShapeDtypeStruct((Bd, 2 * R, HD), f32),
        compiler_params=_cparams(("parallel",)),
        name="nsa_sample",
    )(page_table, q_s, ag_s, comp, winT, newT, ov, rsum, rsum.T, ex, *([selT] * n_pages))


def _retention_body(q_ref, k_ref, v_ref, cos_ref, sin_ref, dm_ref, qd_ref, kd_ref, sd_ref, s0_ref, o_ref, sn_ref,
                    s_scr):
    c = pl.program_id(1)

    @pl.when(c == 0)
    def _():
        s_scr[...] = s0_ref[...]

    cos = cos_ref[...]
    sin = sin_ref[...]

    def rot(x):
        return x * cos + pltpu.roll(x, RET_DK // 2, axis=1) * sin

    for h in range(RET_HEADS):
        q = rot(q_ref[:, h * RET_DK:(h + 1) * RET_DK])
        k = rot(k_ref[:, h * RET_DK:(h + 1) * RET_DK]) * (RET_DK ** -0.5)
        v = v_ref[:, h * RET_DV:(h + 1) * RET_DV].astype(bf16)
        S = s_scr[h]
        inner = _dot_nt(q.astype(bf16), k.astype(bf16)) * dm_ref[h]
        o_ref[:, h * RET_DV:(h + 1) * RET_DV] = (_dot(inner.astype(bf16), v)
                                                 + _dot((q * qd_ref[h]).astype(bf16), S.astype(bf16)))
        s_scr[h] = S * sd_ref[h] + _dot_tn((k * kd_ref[h]).astype(bf16), v)

    @pl.when(c == pl.num_programs(1) - 1)
    def _():
        sn_ref[...] = s_scr[...]


def retention(proj, pos0, state0):
    B, T, _ = proj.shape
    C = RET_CHUNK if T % RET_CHUNK == 0 else T
    n = T // C
    H = RET_HEADS
    half = RET_DK // 2
    inv = jnp.exp(-jnp.log(ROPE_BASE) * jnp.arange(half, dtype=f32) / half)
    ang = (pos0 + jnp.arange(T)).astype(f32)[:, None] * inv[None, :]
    cos = jnp.concatenate([jnp.cos(ang), jnp.cos(ang)], axis=-1)
    sin = jnp.concatenate([-jnp.sin(ang), jnp.sin(ang)], axis=-1)
    log_g = jnp.log(1.0 - jnp.exp2(-5.0 - jnp.arange(H, dtype=f32)))
    i = jnp.arange(C, dtype=f32)
    diff = i[:, None] - i[None, :]
    dm = jnp.where(diff >= 0, jnp.exp(jnp.maximum(diff, 0.0)[None] * log_g[:, None, None]), 0.0)
    qd = jnp.exp((i + 1.0)[None, :] * log_g[:, None])[..., None]
    kd = jnp.exp((C - 1.0 - i)[None, :] * log_g[:, None])[..., None]
    sd = jnp.exp(C * log_g)[:, None, None]
    tab = lambda a: pl.BlockSpec(a.shape, lambda b, c: (0, 0, 0))
    st = pl.BlockSpec((None, H, RET_DK, RET_DV), lambda b, c: (b, 0, 0, 0))
    return pl.pallas_call(
        _retention_body,
        grid=(B, n),
        in_specs=[pl.BlockSpec((None, C, B_QK), lambda b, c: (b, c, OFF_RQ // B_QK)),
                  pl.BlockSpec((None, C, B_QK), lambda b, c: (b, c, OFF_RK // B_QK)),
                  pl.BlockSpec((None, C, B_V), lambda b, c: (b, c, OFF_RV // B_V)),
                  pl.BlockSpec((C, RET_DK), lambda b, c: (c, 0)),
                  pl.BlockSpec((C, RET_DK), lambda b, c: (c, 0)),
                  tab(dm), tab(qd), tab(kd), tab(sd), st],
        out_specs=[pl.BlockSpec((None, C, B_V), lambda b, c: (b, c, 0)), st],
        out_shape=[jax.ShapeDtypeStruct((B, T, B_V), f32), jax.ShapeDtypeStruct((B, H, RET_DK, RET_DV), f32)],
        scratch_shapes=[pltpu.VMEM((H, RET_DK, RET_DV), f32)],
        compiler_params=_cparams(("parallel", "arbitrary")),
        name="retention",
    )(proj, proj, proj, cos, sin, dm, qd, kd, sd, state0)


def _mix_out_body(rg_ref, ga_ref, gb_ref, attn_ref, ro_ref, x_ref, g1_ref, woa_ref, gnw_ref, wob_ref, wo_ref, o_ref):
    ya = _dot(attn_ref[...].astype(bf16), woa_ref[...])
    parts = []
    for h in range(RET_HEADS):
        r = ro_ref[:, h * RET_DV:(h + 1) * RET_DV]
        d = r - jnp.mean(r, axis=-1, keepdims=True)
        var = jnp.mean(d * d, axis=-1, keepdims=True)
        parts.append(d * lax.rsqrt(var + GN_EPS) * gnw_ref[:, h * RET_DV:(h + 1) * RET_DV])
    ron = jnp.concatenate(parts, axis=-1)
    rg = rg_ref[...]
    yb = _dot((rg * jax.nn.sigmoid(rg) * ron).astype(bf16), wob_ref[...])
    m = jax.nn.sigmoid(ga_ref[...]) * ya + jax.nn.sigmoid(gb_ref[...]) * yb
    o_ref[...] = x_ref[...] + g1_ref[...] * _dot(m.astype(bf16), wo_ref[...])


def _mod_spec(mod, tm, D):
    if mod.shape[1] != 1:
        return pl.BlockSpec((None, tm, D), lambda g, i: (g, i, 0))
    return pl.BlockSpec((None, 1, D), lambda g, i: (g, 0, 0))


def mix_out(proj, attn, ro, x, g1, w_oa, gn_w, w_ob, w_o, *, tm):
    G, T, D = x.shape
    full = lambda a: pl.BlockSpec(a.shape, lambda g, i: (0,) * a.ndim)
    wide = lambda blk: pl.BlockSpec((None, tm, B_V), lambda g, i: (g, i, blk))
    gn_w = gn_w.reshape(1, B_V)
    return pl.pallas_call(
        _mix_out_body,
        grid=(G, T // tm),
        in_specs=[wide(OFF_RG // B_V), wide(OFF_GA // B_V), wide(OFF_GB // B_V),
                  pl.BlockSpec((None, tm, A_Q), lambda g, i: (g, i, 0)),
                  wide(0),
                  pl.BlockSpec((None, tm, D), lambda g, i: (g, i, 0)),
                  _mod_spec(g1, tm, D), full(w_oa), full(gn_w), full(w_ob), full(w_o)],
        out_specs=pl.BlockSpec((None, tm, D), lambda g, i: (g, i, 0)),
        out_shape=jax.ShapeDtypeStruct((G, T, D), f32),
        compiler_params=_cparams(("parallel", "parallel")),
        name="mix_out",
    )(proj, proj, proj, attn, ro, x, g1, w_oa, gn_w, w_ob, w_o)


def _ffn_tail(a, am1, am2, b_ref, cw_ref, cb_ref, wout_ref, x_ref, g2_ref, nf_ref, o_ref, final_norm):
    u = cb_ref[...] + am2 * cw_ref[0:1, :] + am1 * cw_ref[1:2, :] + a * cw_ref[2:3, :]
    y = _dot((jax.nn.gelu(u) * b_ref[...]).astype(bf16), wout_ref[...])
    xo = x_ref[...] + g2_ref[...] * y
    if final_norm:
        xo = xo * lax.rsqrt(jnp.mean(xo * xo, axis=-1, keepdims=True) + RMS_EPS) * nf_ref[...]
    o_ref[...] = xo


def _ffn_out_seq_body(a_ref, b_ref, halo_ref, prev_ref, cw_ref, cb_ref, wout_ref, x_ref, g2_ref, nf_ref, o_ref, scr,
                      *, final_norm):
    tm = a_ref.shape[0]
    a = a_ref[...]
    scr[8:8 + tm, :] = a
    scr[6:8, :] = jnp.where(pl.program_id(1) == 0, prev_ref[...], halo_ref[6:8, :])
    _ffn_tail(a, scr[pl.ds(7, tm), :], scr[pl.ds(6, tm), :], b_ref, cw_ref, cb_ref, wout_ref, x_ref, g2_ref, nf_ref,
              o_ref, final_norm)


def _ffn_out_rows_body(a_ref, am1_ref, am2_ref, b_ref, cw_ref, cb_ref, wout_ref, x_ref, g2_ref, nf_ref, o_ref,
                       *, final_norm):
    _ffn_tail(a_ref[...], am1_ref[...], am2_ref[...], b_ref, cw_ref, cb_ref, wout_ref, x_ref, g2_ref, nf_ref, o_ref,
              final_norm)


def ffn_out_seq(ab, prev, x, g2, conv_w, conv_b, w_out, normf_w, *, tm, final_norm):
    B, T, D = x.shape
    F = w_out.shape[0]
    full = lambda a: pl.BlockSpec(a.shape, lambda g, i: (0,) * a.ndim)
    conv_b = conv_b.reshape(1, F)
    normf_w = normf_w.reshape(1, D)
    hb = tm // 8
    return pl.pallas_call(
        functools.partial(_ffn_out_seq_body, final_norm=final_norm),
        grid=(B, T // tm),
        in_specs=[pl.BlockSpec((None, tm, F), lambda g, i: (g, i, 0)),
                  pl.BlockSpec((None, tm, F), lambda g, i: (g, i, 1)),
                  pl.BlockSpec((None, 8, F), lambda g, i: (g, jnp.maximum(i * hb - 1, 0), 0)),
                  pl.BlockSpec((None, CONV_W - 1, F), lambda g, i: (g, 0, 0)),
                  full(conv_w), full(conv_b), full(w_out),
                  pl.BlockSpec((None, tm, D), lambda g, i: (g, i, 0)),
                  _mod_spec(g2, tm, D), full(normf_w)],
        out_specs=pl.BlockSpec((None, tm, D), lambda g, i: (g, i, 0)),
        out_shape=jax.ShapeDtypeStruct((B, T, D), f32),
        scratch_shapes=[pltpu.VMEM((tm + 8, F), f32)],
        compiler_params=_cparams(("parallel", "arbitrary")),
        name="ffn_out_seq",
    )(ab, ab, ab, prev, conv_w, conv_b, w_out, x, g2, normf_w)


def ffn_out_rows(ab, am1, am2, x, g2, conv_w, conv_b, w_out, normf_w, *, tm, final_norm):
    G, T, D = x.shape
    F = w_out.shape[0]
    full = lambda a: pl.BlockSpec(a.shape, lambda g, i: (0,) * a.ndim)
    conv_b = conv_b.reshape(1, F)
    normf_w = normf_w.reshape(1, D)
    rowsF = lambda blk: pl.BlockSpec((None, tm, F), lambda g, i: (g, i, blk))
    return pl.pallas_call(
        functools.partial(_ffn_out_rows_body, final_norm=final_norm),
        grid=(G, T // tm),
        in_specs=[rowsF(0), rowsF(0), rowsF(0), rowsF(1),
                  full(conv_w), full(conv_b), full(w_out),
                  pl.BlockSpec((None, tm, D), lambda g, i: (g, i, 0)),
                  _mod_spec(g2, tm, D), full(normf_w)],
        out_specs=pl.BlockSpec((None, tm, D), lambda g, i: (g, i, 0)),
        out_shape=jax.ShapeDtypeStruct((G, T, D), f32),
        compiler_params=_cparams(("parallel", "parallel")),
        name="ffn_out_rows",
    )(ab, am1, am2, ab, conv_w, conv_b, w_out, x, g2, normf_w)


def _prep_w_in(w):
    D = w.shape[0]
    o = 0
    parts = {}
    for name, n in (("aq", A_Q), ("akv", A_KV), ("ag", A_G), ("rq", B_QK), ("rk", B_QK), ("rv", B_V), ("rg", B_V),
                    ("ga", D), ("gb", D)):
        parts[name] = w[:, o:o + n]
        o += n
    pad = jnp.zeros((D, N_TOK - OFF_AG - A_G), w.dtype)
    tok = jnp.concatenate([parts["rg"], parts["ga"], parts["gb"], parts["rv"], parts["aq"], parts["rq"], parts["rk"],
                           parts["akv"][:, :KV_ROWS], parts["ag"], pad], axis=1)
    return tok.astype(bf16), parts["akv"].T.astype(bf16)


def _prep_cmp(cmp_pos, cmp_w1, cmp_w2):
    G = NSA_KV_HEADS
    pos = jnp.concatenate([cmp_pos] * G, axis=-1)
    w1 = cmp_w1.reshape(2, CMP_BLOCK, HD, CMP_HIDDEN)
    eye = jnp.eye(G, dtype=w1.dtype)
    w1bd = jnp.einsum("eldh,gk->elgdkh", w1, eye).reshape(2, CMP_BLOCK, G * HD, G * CMP_HIDDEN)
    eye_e = jnp.eye(2, dtype=w1.dtype)
    w2bd = jnp.einsum("ehd,gk,ef->eghkfd", cmp_w2, eye, eye_e).reshape(2, G * CMP_HIDDEN, G * 2 * HD)
    return pos, w1bd.astype(bf16), w2bd.astype(bf16)


def _kv_rows_to_out(rowsT, lead):
    t = rowsT.shape[-1]
    r = rowsT.reshape(*lead, 2, NSA_KV_HEADS, HD, t)
    n = len(lead)
    return r.transpose(*range(n), n + 3, n, n + 1, n + 2)


def _keep_last_lanes(a, n):
    t = a.shape[-1]
    if t >= n:
        return a[..., t - n:]
    return jnp.pad(a, ((0, 0),) * (a.ndim - 1) + ((n - t, 0),))


def kernel(x_prompt, x_sample, c_prompt, c_sample, cache_cmp_kv, cache_sel_kv, state_win_kv, state_ret, state_conv,
           page_table, norm1_w, ada_w, ada_b, w_in, cmp_pos, cmp_w1, cmp_w2, w_oa, ret_gn_w, w_ob, w_o, norm2_w,
           ffn_w_in, ffn_conv_w, ffn_conv_b, ffn_w_out, normf_w):
    B, T, D = x_prompt.shape
    Bd, Td, _ = x_sample.shape
    depth = w_in.shape[0]
    n_phys, page = cache_cmp_kv.shape[1], cache_cmp_kv.shape[2]
    past = page_table.shape[1] * page
    wlen = state_win_kv.shape[2]
    F = ffn_w_out.shape[1]
    Rs = Bd * Td
    tm_p = min(512, T)

    mod = ada_mod(jnp.concatenate([c_prompt, c_sample], axis=0), ada_w, ada_b)
    cmpT = cache_cmp_kv.transpose(0, 1, 3, 4, 5, 2).reshape(depth, n_phys, KV_ROWS, page)
    selT = cache_sel_kv.transpose(0, 1, 3, 4, 5, 2).reshape(depth, n_phys, KV_ROWS, page)
    winT = state_win_kv.transpose(0, 1, 3, 4, 5, 2).reshape(depth, Bd, KV_ROWS, wlen)

    xp = x_prompt
    xs = x_sample.reshape(1, Rs, D)
    outs = [[] for _ in range(10)]
    for l in range(depth):
        w_tok, w_kvT = _prep_w_in(w_in[l])
        cw = _prep_cmp(cmp_pos[l], cmp_w1[l], cmp_w2[l])
        w_oa_b, w_ob_b, w_o_b = w_oa[l].astype(bf16), w_ob[l].astype(bf16), w_o[l].astype(bf16)
        ffn_in_b, ffn_out_b = ffn_w_in[l].astype(bf16), ffn_w_out[l].astype(bf16)
        last = l == depth - 1
        mp = [mod[l, :B, k * D:(k + 1) * D].reshape(B, 1, D) for k in range(6)]
        ms = [jnp.repeat(mod[l, B:, k * D:(k + 1) * D], Td, axis=0).reshape(1, Rs, D) for k in range(6)]

        proj, kvT = norm_proj(xp, mp[1], mp[0], norm1_w[l], w_tok, w_kvT, tm=tm_p, tn=1024)
        comp = compress_prompt(proj, cw)
        attn = nsa_prompt(proj, kvT, comp)
        ro, ret_p = retention(proj, 0, jnp.zeros((B, RET_HEADS, RET_DK, RET_DV), f32))
        x1 = mix_out(proj, attn, ro, xp, mp[2], w_oa_b, ret_gn_w[l], w_ob_b, w_o_b, tm=tm_p)
        ab = norm_proj(x1, mp[4], mp[3], norm2_w[l], ffn_in_b, None, tm=tm_p, tn=F)
        xp = ffn_out_seq(ab, jnp.zeros((B, CONV_W - 1, F), f32), x1, mp[5], ffn_conv_w[l], ffn_conv_b[l], ffn_out_b,
                         normf_w, tm=min(256, T), final_norm=last)
        outs[0].append(_kv_rows_to_out(kvT[:, 0:KV_ROWS], (B,)))
        outs[2].append(_kv_rows_to_out(kvT[:, KV_ROWS:2 * KV_ROWS], (B,)))
        outs[4].append(_kv_rows_to_out(_keep_last_lanes(kvT[:, 2 * KV_ROWS:], wlen), (B,)))
        outs[6].append(ret_p)
        outs[8].append(ab[:, T - (CONV_W - 1):, :F])

        proj_s, kvT_s = norm_proj(xs, ms[1], ms[0], norm1_w[l], w_tok, w_kvT, tm=Rs, tn=1024)
        comp_s = compress_paged(cmpT, l, page_table, cw)
        q_s = proj_s[0, :, OFF_AQ:OFF_AQ + A_Q].reshape(Bd, Td, NSA_KV_HEADS, HPG, HD)
        q_s = q_s.transpose(0, 2, 3, 1, 4).reshape(Bd, NSA_HEADS * Td, HD)
        ag_s = proj_s[0, :, OFF_AG:OFF_AG + A_G].reshape(Bd, Td, NSA_KV_HEADS, HPG, 3)
        ag_s = ag_s.transpose(0, 2, 3, 1, 4).reshape(Bd, NSA_HEADS * Td, 3)
        newT = kvT_s[0].reshape(A_KV, Bd, Td).transpose(1, 0, 2)
        newT_pad = jnp.pad(newT, ((0, 0), (0, 0), (0, LANE - Td)))
        attn_s = nsa_sample(q_s, ag_s, comp_s, selT, l, page_table, winT[l], newT_pad, tn=Td, past=past)
        attn_s = attn_s.reshape(Bd, NSA_KV_HEADS, HPG, Td, HD).transpose(0, 3, 1, 2, 4).reshape(1, Rs, A_Q)
        ro_s, ret_s = retention(proj_s.reshape(Bd, Td, N_TOK), past, state_ret[l])
        x1s = mix_out(proj_s, attn_s, ro_s.reshape(1, Rs, B_V), xs, ms[2], w_oa_b, ret_gn_w[l], w_ob_b, w_o_b, tm=Rs)
        ab_s = norm_proj(x1s, ms[4], ms[3], norm2_w[l], ffn_in_b, None, tm=Rs, tn=F)
        a_ext = jnp.concatenate([state_conv[l], ab_s[0, :, :F].reshape(Bd, Td, F)], axis=1)
        am1 = a_ext[:, 1:1 + Td].reshape(1, Rs, F)
        am2 = a_ext[:, 0:Td].reshape(1, Rs, F)
        xs = ffn_out_rows(ab_s, am1, am2, x1s, ms[5], ffn_conv_w[l], ffn_conv_b[l], ffn_out_b, normf_w, tm=Rs,
                          final_norm=last)
        outs[1].append(_kv_rows_to_out(newT[:, 0:KV_ROWS], (Bd,)))
        outs[3].append(_kv_rows_to_out(newT[:, KV_ROWS:2 * KV_ROWS], (Bd,)))
        win_all = jnp.concatenate([winT[l], newT[:, 2 * KV_ROWS:]], axis=-1)
        outs[5].append(_kv_rows_to_out(_keep_last_lanes(win_all, wlen), (Bd,)))
        outs[7].append(ret_s)
        outs[9].append(a_ext[:, Td:])

    st = lambda k: jnp.stack(outs[k])
    return (xp, xs.reshape(Bd, Td, D), st(0), st(1), st(2), st(3), st(4), st(5), st(6), st(7), st(8), st(9))
```

```python
import functools

import jax
import jax.numpy as jnp
from jax import lax
from jax.experimental import pallas as pl
from jax.experimental.pallas import tpu as pltpu

f32 = jnp.float32
bf16 = jnp.bfloat16

NSA_HEADS = 8
NSA_KV_HEADS = 2
HPG = NSA_HEADS // NSA_KV_HEADS
HD = 64
CMP_STRIDE = 16
CMP_BLOCK = 32
CMP_HIDDEN = 64
SEL_BLOCK = 64
SEL_SHIFT = 6
N_SEL = 16
WINDOW = 512
RET_HEADS = 4
RET_DK = 128
RET_DV = 256
RET_CHUNK = 128
ROPE_BASE = 10000.0
CONV_W = 3
RMS_EPS = 1e-6
GN_EPS = 1e-5
NEG = -1e30
FORCE_BONUS = 1e4

A_Q = NSA_HEADS * HD
KV_ROWS = 2 * NSA_KV_HEADS * HD
A_KV = 3 * KV_ROWS
A_G = 3 * NSA_HEADS
B_QK = RET_HEADS * RET_DK
B_V = RET_HEADS * RET_DV

OFF_RG, OFF_GA, OFF_GB, OFF_RV = 0, 1024, 2048, 3072
OFF_AQ, OFF_RQ, OFF_RK, OFF_KVC, OFF_AG = 4096, 4608, 5120, 5632, 5888
N_TOK = 6144

LANE = 128
VMEM_LIMIT = 56 * 1024 * 1024


def _cparams(sem):
    return pltpu.CompilerParams(dimension_semantics=sem, vmem_limit_bytes=VMEM_LIMIT)


def _dot(a, b):
    return jnp.dot(a, b, preferred_element_type=f32)


def _dot_nt(a, b):
    return lax.dot_general(a, b, (((1,), (1,)), ((), ())), preferred_element_type=f32)


def _dot_tn(a, b):
    return lax.dot_general(a, b, (((0,), (0,)), ((), ())), preferred_element_type=f32)


def _split_dot(a, b_bf16):
    hi = a.astype(bf16)
    lo = (a - hi.astype(f32)).astype(bf16)
    return _dot(hi, b_bf16) + _dot(lo, b_bf16)


def _ada_body(c_ref, w_ref, b_ref, o_ref):
    c = c_ref[...]
    s = c * jax.nn.sigmoid(c)
    o_ref[...] = _dot(s.astype(bf16), w_ref[...].astype(bf16)) + b_ref[...]


def ada_mod(c_all, ada_w, ada_b):
    depth, d, n = ada_w.shape
    r = c_all.shape[0]
    tn = 1024
    return pl.pallas_call(
        _ada_body,
        grid=(depth, n // tn),
        in_specs=[pl.BlockSpec((r, d), lambda l, j: (0, 0)),
                  pl.BlockSpec((None, d, tn), lambda l, j: (l, 0, j)),
                  pl.BlockSpec((None, 1, tn), lambda l, j: (l, 0, j))],
        out_specs=pl.BlockSpec((None, r, tn), lambda l, j: (l, 0, j)),
        out_shape=jax.ShapeDtypeStruct((depth, r, n), f32),
        compiler_params=_cparams(("parallel", "parallel")),
        name="ada_mod",
    )(c_all, ada_w, ada_b.reshape(depth, 1, n))


def _norm_proj_body(x_ref, sc_ref, sh_ref, nw_ref, w_ref, *rest, has_t):
    if has_t:
        wt_ref, o_ref, ot_ref, h_scr = rest
    else:
        o_ref, h_scr = rest

    j = pl.program_id(2)
    tn = o_ref.shape[1]

    @pl.when(j == 0)
    def _():
        h_scr[...] = _norm_mod(x_ref[...], nw_ref, sc_ref, sh_ref)
        if has_t:
            ot_ref[...] = _dot_nt(wt_ref[...], h_scr[...])

    o_ref[...] = _dot(h_scr[...], w_ref[:, pl.ds(pl.multiple_of(j * tn, tn), tn)])


def _norm_mod(x, nw_ref, sc_ref, sh_ref):
    y = x * lax.rsqrt(jnp.mean(x * x, axis=-1, keepdims=True) + RMS_EPS) * nw_ref[...]
    return (y * (1.0 + sc_ref[...]) + sh_ref[...]).astype(bf16)


def _resident(a):
    return pl.BlockSpec(a.shape, lambda *_: (0,) * a.ndim, pipeline_mode=pl.Buffered(1))


def norm_proj(x, sc, sh, nw, w, wt, *, tm, tn):
    G, T, D = x.shape
    N = w.shape[1]
    per_row = sc.shape[1] != 1
    mr = tm if per_row else 1
    mod_spec = pl.BlockSpec((None, mr, D), (lambda g, i, j: (g, i, 0)) if per_row else (lambda g, i, j: (g, 0, 0)))
    in_specs = [pl.BlockSpec((None, tm, D), lambda g, i, j: (g, i, 0)), mod_spec, mod_spec,
                pl.BlockSpec((1, D), lambda g, i, j: (0, 0)),
                _resident(w)]
    out_specs = [pl.BlockSpec((None, tm, tn), lambda g, i, j: (g, i, j))]
    out_shape = [jax.ShapeDtypeStruct((G, T, N), f32)]
    args = [x, sc, sh, nw.reshape(1, D), w]
    if wt is not None:
        NT = wt.shape[0]
        in_specs.append(_resident(wt))
        out_specs.append(pl.BlockSpec((None, NT, tm), lambda g, i, j: (g, 0, i)))
        out_shape.append(jax.ShapeDtypeStruct((G, NT, T), f32))
        args.append(wt)
    res = pl.pallas_call(
        functools.partial(_norm_proj_body, has_t=wt is not None),
        grid=(G, T // tm, N // tn),
        in_specs=in_specs, out_specs=out_specs, out_shape=out_shape,
        scratch_shapes=[pltpu.VMEM((tm, D), bf16)],
        compiler_params=_cparams(("parallel", "parallel", "arbitrary")),
        name="norm_proj_t" if wt is not None else "norm_proj",
    )(*args)
    return res if wt is not None else res[0]


CPP = LANE // CMP_STRIDE
assert CMP_BLOCK == 2 * CMP_STRIDE


def _compress_pages(get_page, n_pages, perm_ref, pos_ref, w1_ref, w2_ref, o_ref, xs_ref, acc_ref):
    n = o_ref.shape[0]
    perm = perm_ref[...]
    for k in range(n_pages):
        for e in range(2):
            y = _dot_nt(perm, get_page(k, e).astype(bf16))
            for s in range(CMP_STRIDE):
                xs_ref[e, s, k * CPP:(k + 1) * CPP, :] = y[s * CPP:(s + 1) * CPP]
    for e in range(2):
        for s in range(CMP_STRIDE):
            xs_ref[e, s, n:n + 8, :] = jnp.zeros((8, LANE), f32)
            xs_ref[e, s, n + 1:n + 2, :] = pos_ref[e, s:s + 1, :]
            xs_ref[e, s, n + 2:n + 3, :] = pos_ref[e, CMP_STRIDE + s:CMP_STRIDE + s + 1, :]
    out = jnp.zeros((n, 2 * LANE), f32)
    for e in range(2):
        acc = jnp.zeros((n + 8, 2 * LANE), f32)
        for s in range(CMP_STRIDE):
            acc = acc + _dot(xs_ref[e, s].astype(bf16), w1_ref[e, s])
        acc_ref[...] = acc
        bias = acc_ref[n + 1:n + 2, 0:LANE] + acc_ref[n + 2:n + 3, LANE:2 * LANE]
        hid = acc_ref[0:n, 0:LANE] + acc_ref[pl.ds(1, n), LANE:2 * LANE] + bias
        out = out + _dot(jax.nn.gelu(hid).astype(bf16), w2_ref[e])
    o_ref[...] = out


def _cmp_weight_specs():
    z = lambda nd: (lambda *a: (0,) * nd)
    return [pl.BlockSpec((LANE, LANE), z(2)),
            pl.BlockSpec((2, CMP_BLOCK, LANE), z(3)),
            pl.BlockSpec((2, CMP_STRIDE, LANE, 2 * LANE), z(4)),
            pl.BlockSpec((2, LANE, 2 * LANE), z(3))]


def _cmp_scratch(n):
    return [pltpu.VMEM((2, CMP_STRIDE, n + 8, LANE), f32), pltpu.VMEM((n + 8, 2 * LANE), f32)]


def _compress_prompt_body(kvT_ref, perm_ref, pos_ref, w1_ref, w2_ref, o_ref, xs_ref, acc_ref):
    n_pages = kvT_ref.shape[1] // LANE
    get = lambda k, e: kvT_ref[e * LANE:(e + 1) * LANE, k * LANE:(k + 1) * LANE]
    _compress_pages(get, n_pages, perm_ref, pos_ref, w1_ref, w2_ref, o_ref, xs_ref, acc_ref)


def compress_prompt(kvT, cw):
    B, _, T = kvT.shape
    assert T % LANE == 0
    n = T // CMP_STRIDE
    return pl.pallas_call(
        _compress_prompt_body,
        grid=(B,),
        in_specs=[pl.BlockSpec((None, KV_ROWS, T), lambda b: (b, 0, 0))] + _cmp_weight_specs(),
        out_specs=pl.BlockSpec((None, n, 2 * LANE), lambda b: (b, 0, 0)),
        out_shape=jax.ShapeDtypeStruct((B, n, 2 * LANE), f32),
        scratch_shapes=_cmp_scratch(n),
        compiler_params=_cparams(("parallel",)),
        name="compress_prompt",
    )(kvT, *cw)


def _page_specs(layer, n_pages, page):
    return [pl.BlockSpec((None, None, KV_ROWS, page), lambda b, pt, k=k: (layer, pt[b, k], 0, 0))
            for k in range(n_pages)]


def _compress_paged_body(pt_ref, perm_ref, pos_ref, w1_ref, w2_ref, *rest, n_pages):
    pages, (o_ref, xs_ref, acc_ref) = rest[:n_pages], rest[n_pages:]
    get = lambda k, e: pages[k][e * LANE:(e + 1) * LANE, :]
    _compress_pages(get, n_pages, perm_ref, pos_ref, w1_ref, w2_ref, o_ref, xs_ref, acc_ref)


def compress_paged(cacheT, layer, page_table, cw):
    Bd, n_pages = page_table.shape
    page = cacheT.shape[-1]
    assert page == LANE
    L = n_pages * page
    n = L // CMP_STRIDE
    return pl.pallas_call(
        functools.partial(_compress_paged_body, n_pages=n_pages),
        grid_spec=pltpu.PrefetchScalarGridSpec(
            num_scalar_prefetch=1,
            grid=(Bd,),
            in_specs=_cmp_weight_specs() + _page_specs(layer, n_pages, page),
            out_specs=pl.BlockSpec((None, n, 2 * LANE), lambda b, pt: (b, 0, 0)),
            scratch_shapes=_cmp_scratch(n),
        ),
        out_shape=jax.ShapeDtypeStruct((Bd, n, 2 * LANE), f32),
        compiler_params=_cparams(("parallel",)),
        name="compress_paged",
    )(page_table, *cw, *([cacheT] * n_pages))


def _select_blocks(imp, s_ok, forced, n_sel, jj):
    k = min(N_SEL, n_sel)
    score = jnp.where(s_ok, imp + jnp.where(forced, FORCE_BONUS, 0.0), NEG)
    rank = jnp.zeros(score.shape, f32)
    for j2 in range(n_sel):
        col = score[:, j2:j2 + 1]
        ge = jnp.where(col >= score, 1.0, 0.0)
        gt = jnp.where(col > score, 1.0, 0.0)
        rank = rank + jnp.where(jj > j2, ge, gt)
    return jnp.where((rank < k) & s_ok, 1.0, 0.0)


def _select_blocks_t(imp, t0, n_sel):
    tq, sw = imp.shape
    rows = -(-n_sel // 8) * 8
    k = min(N_SEL, n_sel)
    jj = lax.broadcasted_iota(jnp.int32, (rows, tq), 0)
    qpos = t0 + lax.broadcasted_iota(jnp.int32, (rows, tq), 1)
    cur = jnp.right_shift(qpos, SEL_SHIFT)
    s_ok = (jj * SEL_BLOCK <= qpos) & (jj < n_sel)
    forced = (jj == 0) | (jj == cur) | (jj == cur - 1)
    imp_t = jnp.concatenate([imp[c * LANE:(c + 1) * LANE].T for c in range(tq // LANE)], axis=1)[:rows]
    score = jnp.where(s_ok, imp_t + jnp.where(forced, FORCE_BONUS, 0.0), NEG)
    rank = jnp.zeros(score.shape, f32)
    for j2 in range(n_sel):
        row = score[j2:j2 + 1, :]
        ge = jnp.where(row >= score, 1.0, 0.0)
        gt = jnp.where(row > score, 1.0, 0.0)
        rank = rank + jnp.where(jj > j2, ge, gt)
    sel_t = jnp.where((rank < k) & s_ok, 1.0, 0.0)
    sel_t = jnp.concatenate([sel_t, jnp.zeros((sw - rows, tq), f32)], axis=0)
    return jnp.concatenate([sel_t[:, c * LANE:(c + 1) * LANE].T for c in range(tq // LANE)], axis=0)


def _softmax_rows(sm, ok):
    m = jnp.max(sm, axis=-1, keepdims=True)
    e = jnp.where(ok, jnp.exp(sm - m), 0.0)
    den = jnp.sum(e, axis=-1, keepdims=True)
    return e / jnp.maximum(den, 1e-30)


def _nsa_prompt_body(q_ref, ag_ref, comp_ref, ksT_ref, vsT_ref, kwT_ref, vwT_ref, ov_ref, ex_ref, o_ref,
                     *, tq, n_cmp, n_sel, wk, ck):
    i = pl.program_id(1)
    t0 = i * tq
    scale = HD ** -0.5
    NC = comp_ref.shape[0]
    SW = ov_ref.shape[1]
    qpos = t0 + lax.broadcasted_iota(jnp.int32, (tq, 1), 0)
    col_n = lax.broadcasted_iota(jnp.int32, (tq, NC), 1)
    c_ok = ((col_n * CMP_STRIDE + (CMP_BLOCK - 1)) <= qpos) & (col_n < n_cmp)
    jj = lax.broadcasted_iota(jnp.int32, (tq, SW), 1)
    s_ok = (jj * SEL_BLOCK <= qpos) & (jj < n_sel)
    wstart = pl.multiple_of(jnp.maximum(t0 + tq - wk, 0), LANE)
    kposw = wstart + lax.broadcasted_iota(jnp.int32, (tq, wk), 1)
    rel = qpos - kposw
    w_bias4 = jnp.concatenate([jnp.where((rel >= 0) & (rel < WINDOW), 0.0, NEG)] * HPG, axis=0)
    c_ok4 = jnp.concatenate([c_ok] * HPG, axis=0)
    sg = jax.nn.sigmoid(ag_ref[...])
    n_chunks = (t0 + tq + ck - 1) // ck
    kcol = lax.broadcasted_iota(jnp.int32, (tq, ck), 1)
    all_valid_selected = t0 + tq <= min(N_SEL, n_sel) * SEL_BLOCK

    for g in range(NSA_KV_HEADS):
        kc = comp_ref[:, g * LANE:g * LANE + HD].astype(bf16)
        vc = comp_ref[:, g * LANE + HD:(g + 1) * LANE].astype(bf16)
        q4 = jnp.concatenate([(q_ref[:, (g * HPG + h) * HD:(g * HPG + h + 1) * HD] * scale).astype(bf16)
                              for h in range(HPG)], axis=0)
        p = _softmax_rows(jnp.where(c_ok4, _dot_nt(q4, kc), NEG), c_ok4)
        o_c = _dot(p.astype(bf16), vc)
        psum = p[0:tq]
        for h in range(1, HPG):
            psum = psum + p[h * tq:(h + 1) * tq]
        imp = _split_dot(psum, ov_ref[...])
        sel = lax.cond(all_valid_selected,
                       lambda: jnp.where(s_ok, 1.0, 0.0),
                       lambda: _select_blocks_t(imp, t0, n_sel))
        selb = sel.astype(bf16)

        def chunk_step(c, carry):
            k0 = pl.multiple_of(c * ck, ck)
            kT = ksT_ref[g * HD:(g + 1) * HD, pl.ds(k0, ck)].astype(bf16)
            vT = vsT_ref[g * HD:(g + 1) * HD, pl.ds(k0, ck)].astype(bf16)
            ok = (_dot(selb, ex_ref[:, pl.ds(k0, ck)]) > 0.5) & (k0 + kcol <= qpos)
            bias = jnp.where(ok, 0.0, NEG)
            m, l, acc = carry
            s = _dot(q4, kT) + jnp.concatenate([bias] * HPG, axis=0)
            m_new = jnp.maximum(m, jnp.max(s, axis=-1, keepdims=True))
            e = jnp.exp(s - m_new)
            alpha = jnp.exp(m - m_new)
            l = alpha * l + jnp.sum(e, axis=-1, keepdims=True)
            acc = alpha * acc + _dot_nt(e.astype(bf16), vT)
            return m_new, l, acc

        init = (jnp.full((HPG * tq, 1), NEG, f32), jnp.zeros((HPG * tq, 1), f32), jnp.zeros((HPG * tq, HD), f32))
        _, l, acc = lax.fori_loop(0, n_chunks, chunk_step, init)
        o_s = acc / l

        kw = kwT_ref[g * HD:(g + 1) * HD, pl.ds(wstart, wk)].astype(bf16)
        vw = vwT_ref[g * HD:(g + 1) * HD, pl.ds(wstart, wk)].astype(bf16)
        s = _dot(q4, kw) + w_bias4
        e = jnp.exp(s - jnp.max(s, axis=-1, keepdims=True))
        o_w = _dot_nt(e.astype(bf16), vw) / jnp.sum(e, axis=-1, keepdims=True)

        gate = lambda br: jnp.concatenate(
            [sg[:, 3 * (g * HPG + h) + br:3 * (g * HPG + h) + br + 1] for h in range(HPG)], axis=0)
        o = gate(0) * o_c + gate(1) * o_s + gate(2) * o_w
        for h in range(HPG):
            o_ref[:, (g * HPG + h) * HD:(g * HPG + h + 1) * HD] = o[h * tq:(h + 1) * tq]


def _overlap_matrix(nc_rows, n_cmp, sw_cols, n_sel):
    cs = jnp.arange(nc_rows) * CMP_STRIDE
    ss = jnp.arange(sw_cols) * SEL_BLOCK
    ok = (cs[:, None] < ss[None, :] + SEL_BLOCK) & (cs[:, None] + CMP_BLOCK > ss[None, :])
    ok = ok & (jnp.arange(nc_rows)[:, None] < n_cmp) & (jnp.arange(sw_cols)[None, :] < n_sel)
    return ok.astype(bf16)


def nsa_prompt(proj, kvT, comp):
    B, T, _ = proj.shape
    tq = min(256, T)
    n_cmp = T // CMP_STRIDE - (CMP_BLOCK // CMP_STRIDE) + 1
    n_sel = -(-T // SEL_BLOCK)
    NC = comp.shape[1]
    SW = LANE
    assert n_sel <= SW and T % tq == 0
    wk = min(WINDOW + tq, T)
    ck = min(512, T)
    assert T % ck == 0
    ov = _overlap_matrix(NC, n_cmp, SW, n_sel)
    ex = (jnp.arange(SW)[:, None] == (jnp.arange(T)[None, :] // SEL_BLOCK)).astype(bf16)
    kv_spec = lambda blk: pl.BlockSpec((None, LANE, T), lambda b, i: (b, blk, 0))
    return pl.pallas_call(
        functools.partial(_nsa_prompt_body, tq=tq, n_cmp=n_cmp, n_sel=n_sel, wk=wk, ck=ck),
        grid=(B, T // tq),
        in_specs=[pl.BlockSpec((None, tq, A_Q), lambda b, i: (b, i, OFF_AQ // A_Q)),
                  pl.BlockSpec((None, tq, LANE), lambda b, i: (b, i, OFF_AG // LANE)),
                  pl.BlockSpec((None, NC, 2 * LANE), lambda b, i: (b, 0, 0)),
                  kv_spec(2), kv_spec(3), kv_spec(4), kv_spec(5),
                  pl.BlockSpec((NC, SW), lambda b, i: (0, 0)),
                  pl.BlockSpec((SW, T), lambda b, i: (0, 0))],
        out_specs=pl.BlockSpec((None, tq, A_Q), lambda b, i: (b, i, 0)),
        out_shape=jax.ShapeDtypeStruct((B, T, A_Q), f32),
        compiler_params=_cparams(("parallel", "arbitrary")),
        name="nsa_prompt",
    )(proj, proj, comp, kvT, kvT, kvT, kvT, ov, ex)


def _nsa_sample_body(pt_ref, q_ref, ag_ref, comp_ref, kwT_ref, newT_ref, ov_ref, rsum_ref, rexp_ref, ex_ref, *rest,
                     tn, past, n_cmp, n_sel, n_pages):
    pages, o_ref = rest[:n_pages], rest[n_pages]
    scale = HD ** -0.5
    R = HPG * tn
    NC = comp_ref.shape[0]
    SW = ov_ref.shape[1]
    tok = lax.rem(lax.broadcasted_iota(jnp.int32, (R, 1), 0), tn)
    qpos = past + tok
    col_n = lax.broadcasted_iota(jnp.int32, (R, NC), 1)
    c_ok = ((col_n * CMP_STRIDE + (CMP_BLOCK - 1)) <= qpos) & (col_n < n_cmp)
    jj = lax.broadcasted_iota(jnp.int32, (tn, SW), 1)
    qpos_t = past + lax.broadcasted_iota(jnp.int32, (tn, 1), 0)
    cur = jnp.right_shift(qpos_t, SEL_SHIFT)
    s_ok = (jj * SEL_BLOCK <= qpos_t) & (jj < n_sel)
    forced = (jj == 0) | (jj == cur) | (jj == cur - 1)
    wlen = kwT_ref.shape[1]
    kposw = (past - wlen) + lax.broadcasted_iota(jnp.int32, (R, wlen), 1)
    relw = qpos - kposw
    w_ok = (relw >= 0) & (relw < WINDOW)
    nl = newT_ref.shape[1]
    u = lax.broadcasted_iota(jnp.int32, (R, nl), 1)
    reln = tok - u
    n_ok = (reln >= 0) & (reln < WINDOW) & (u < tn)
    nb = past // SEL_BLOCK
    sg = jax.nn.sigmoid(ag_ref[...])
    for g in range(NSA_KV_HEADS):
        q = (q_ref[g * R:(g + 1) * R, :] * scale).astype(bf16)
        kc = comp_ref[:, g * LANE:g * LANE + HD].astype(bf16)
        vc = comp_ref[:, g * LANE + HD:(g + 1) * LANE].astype(bf16)
        pc = _softmax_rows(jnp.where(c_ok, _dot_nt(q, kc), NEG), c_ok)
        o_c = _dot(pc.astype(bf16), vc)
        imp = _split_dot(_rsum_exact(rsum_ref, pc), ov_ref[...])
        sel = _select_blocks(imp, s_ok, forced, n_sel, jj)
        selr = _dot(rexp_ref[...], sel.astype(bf16))
        kwp = kwT_ref[g * HD:(g + 1) * HD, :].astype(bf16)
        vwp = kwT_ref[2 * HD + g * HD:2 * HD + (g + 1) * HD, :].astype(bf16)
        base = 2 * KV_ROWS
        kwn = newT_ref[base + g * HD:base + (g + 1) * HD, :].astype(bf16)
        vwn = newT_ref[base + 2 * HD + g * HD:base + 2 * HD + (g + 1) * HD, :].astype(bf16)
        s1 = jnp.where(w_ok, _dot(q, kwp), NEG)
        s2 = jnp.where(n_ok, _dot(q, kwn), NEG)
        m = jnp.maximum(jnp.max(s1, axis=-1, keepdims=True), jnp.max(s2, axis=-1, keepdims=True))
        e1 = jnp.exp(s1 - m)
        e2 = jnp.exp(s2 - m)
        den = jnp.sum(e1, axis=-1, keepdims=True) + jnp.sum(e2, axis=-1, keepdims=True)
        o_w = (_dot_nt(e1.astype(bf16), vwp) + _dot_nt(e2.astype(bf16), vwn)) / den
        base = KV_ROWS
        ksn = newT_ref[base + g * HD:base + (g + 1) * HD, :].astype(bf16)
        vsn = newT_ref[base + 2 * HD + g * HD:base + 2 * HD + (g + 1) * HD, :].astype(bf16)
        ksp = jnp.concatenate([pg[g * HD:(g + 1) * HD, :] for pg in pages], axis=1).astype(bf16)
        vsp = jnp.concatenate([pg[2 * HD + g * HD:2 * HD + (g + 1) * HD, :] for pg in pages], axis=1).astype(bf16)
        okp = _dot(selr.astype(bf16), ex_ref[...]) > 0.5
        okn = (u <= tok) & (u < tn) & (selr[:, nb:nb + 1] > 0.5)
        s1 = jnp.where(okp, _dot(q, ksp), NEG)
        s2 = jnp.where(okn, _dot(q, ksn), NEG)
        m = jnp.maximum(jnp.max(s1, axis=-1, keepdims=True), jnp.max(s2, axis=-1, keepdims=True))
        e1 = jnp.exp(s1 - m)
        e2 = jnp.exp(s2 - m)
        den = jnp.sum(e1, axis=-1, keepdims=True) + jnp.sum(e2, axis=-1, keepdims=True)
        o_s = (_dot_nt(e1.astype(bf16), vsp) + _dot_nt(e2.astype(bf16), vsn)) / den
        gg = sg[g * R:(g + 1) * R, :]
        o_ref[g * R:(g + 1) * R, :] = gg[:, 0:1] * o_c + gg[:, 1:2] * o_s + gg[:, 2:3] * o_w


def _rsum_exact(rsum_ref, pc):
    r = rsum_ref[...]
    hi = pc.astype(bf16)
    lo = (pc - hi.astype(f32)).astype(bf16)
    return _dot(r, hi) + _dot(r, lo)


def nsa_sample(q_s, ag_s, comp, selT, layer, page_table, winT, newT, *, tn, past):
    Bd, n_pages = page_table.shape
    page = selT.shape[-1]
    R = HPG * tn
    L = past + tn
    n_cmp = L // CMP_STRIDE - (CMP_BLOCK // CMP_STRIDE) + 1
    n_sel = -(-L // SEL_BLOCK)
    NC = comp.shape[1]
    SW = -(-n_sel // LANE) * LANE
    assert past % page == 0 and page % SEL_BLOCK == 0 and tn <= SEL_BLOCK and n_cmp <= NC
    wlen = winT.shape[-1]
    ov = _overlap_matrix(NC, n_cmp, SW, n_sel)
    rsum = (jnp.arange(tn)[:, None] == (jnp.arange(R)[None, :] % tn)).astype(bf16)
    ex = (jnp.arange(SW)[:, None] == (jnp.arange(past)[None, :] // SEL_BLOCK)).astype(bf16)
    const = lambda *shape: pl.BlockSpec(shape, lambda b, pt: (0,) * len(shape))
    return pl.pallas_call(
        functools.partial(_nsa_sample_body, tn=tn, past=past, n_cmp=n_cmp, n_sel=n_sel, n_pages=n_pages),
        grid_spec=pltpu.PrefetchScalarGridSpec(
            num_scalar_prefetch=1,
            grid=(Bd,),
            in_specs=[pl.BlockSpec((None, 2 * R, HD), lambda b, pt: (b, 0, 0)),
                      pl.BlockSpec((None, 2 * R, 3), lambda b, pt: (b, 0, 0)),
                      pl.BlockSpec((None, NC, 2 * LANE), lambda b, pt: (b, 0, 0)),
                      pl.BlockSpec((None, KV_ROWS, wlen), lambda b, pt: (b, 0, 0)),
                      pl.BlockSpec((None, A_KV, LANE), lambda b, pt: (b, 0, 0)),
                      const(NC, SW), const(tn, R), const(R, tn), const(SW, past)]
            + _page_specs(layer, n_pages, page),
            out_specs=pl.BlockSpec((None, 2 * R, HD), lambda b, pt: (b, 0, 0)),
        ),
        out_shape=jax.ShapeDtypeStruct((Bd, 2 * R, HD), f32),
        compiler_params=_cparams(("parallel",)),
        name="nsa_sample",
    )(page_table, q_s, ag_s, comp, winT, newT, ov, rsum, rsum.T, ex, *([selT] * n_pages))


def _retention_body(q_ref, k_ref, v_ref, cos_ref, sin_ref, dm_ref, qd_ref, kd_ref, sd_ref, s0_ref, o_ref, sn_ref,
                    s_scr):
    c = pl.program_id(1)

    @pl.when(c == 0)
    def _():
        s_scr[...] = s0_ref[...]

    C = dm_ref.shape[1]
    for cc in range(q_ref.shape[0] // C):
        r = slice(cc * C, (cc + 1) * C)
        cos = cos_ref[r, :]
        sin = sin_ref[r, :]

        def rot(x):
            return x * cos + pltpu.roll(x, RET_DK // 2, axis=1) * sin

        for h in range(RET_HEADS):
            q = rot(q_ref[r, h * RET_DK:(h + 1) * RET_DK])
            k = rot(k_ref[r, h * RET_DK:(h + 1) * RET_DK]) * (RET_DK ** -0.5)
            v = v_ref[r, h * RET_DV:(h + 1) * RET_DV].astype(bf16)
            S = s_scr[h]
            inner = _dot_nt(q.astype(bf16), k.astype(bf16)) * dm_ref[h]
            o_ref[r, h * RET_DV:(h + 1) * RET_DV] = (_dot(inner.astype(bf16), v)
                                                     + _dot((q * qd_ref[h]).astype(bf16), S.astype(bf16)))
            s_scr[h] = S * sd_ref[h] + _dot_tn((k * kd_ref[h]).astype(bf16), v)

    @pl.when(c == pl.num_programs(1) - 1)
    def _():
        sn_ref[...] = s_scr[...]


def retention(proj, pos0, state0):
    B, T, _ = proj.shape
    C = RET_CHUNK if T % RET_CHUNK == 0 else T
    n = T // C
    H = RET_HEADS
    half = RET_DK // 2
    inv = jnp.exp(-jnp.log(ROPE_BASE) * jnp.arange(half, dtype=f32) / half)
    ang = (pos0 + jnp.arange(T)).astype(f32)[:, None] * inv[None, :]
    cos = jnp.concatenate([jnp.cos(ang), jnp.cos(ang)], axis=-1)
    sin = jnp.concatenate([-jnp.sin(ang), jnp.sin(ang)], axis=-1)
    log_g = jnp.log(1.0 - jnp.exp2(-5.0 - jnp.arange(H, dtype=f32)))
    i = jnp.arange(C, dtype=f32)
    diff = i[:, None] - i[None, :]
    dm = jnp.where(diff >= 0, jnp.exp(jnp.maximum(diff, 0.0)[None] * log_g[:, None, None]), 0.0)
    qd = jnp.exp((i + 1.0)[None, :] * log_g[:, None])[..., None]
    kd = jnp.exp((C - 1.0 - i)[None, :] * log_g[:, None])[..., None]
    sd = jnp.exp(C * log_g)[:, None, None]
    tab = lambda a: pl.BlockSpec(a.shape, lambda b, c: (0, 0, 0))
    st = pl.BlockSpec((None, H, RET_DK, RET_DV), lambda b, c: (b, 0, 0, 0))
    cps = 4 if n % 4 == 0 else 1
    R = cps * C
    return pl.pallas_call(
        _retention_body,
        grid=(B, n // cps),
        in_specs=[pl.BlockSpec((None, R, B_QK), lambda b, c: (b, c, OFF_RQ // B_QK)),
                  pl.BlockSpec((None, R, B_QK), lambda b, c: (b, c, OFF_RK // B_QK)),
                  pl.BlockSpec((None, R, B_V), lambda b, c: (b, c, OFF_RV // B_V)),
                  pl.BlockSpec((R, RET_DK), lambda b, c: (c, 0)),
                  pl.BlockSpec((R, RET_DK), lambda b, c: (c, 0)),
                  tab(dm), tab(qd), tab(kd), tab(sd), st],
        out_specs=[pl.BlockSpec((None, R, B_V), lambda b, c: (b, c, 0)), st],
        out_shape=[jax.ShapeDtypeStruct((B, T, B_V), f32), jax.ShapeDtypeStruct((B, H, RET_DK, RET_DV), f32)],
        scratch_shapes=[pltpu.VMEM((H, RET_DK, RET_DV), f32)],
        compiler_params=_cparams(("parallel", "arbitrary")),
        name="retention",
    )(proj, proj, proj, cos, sin, dm, qd, kd, sd, state0)


def _mix_out_body(rg_ref, ga_ref, gb_ref, attn_ref, ro_ref, x_ref, g1_ref, woa_ref, gnw_ref, wob_ref, wo_ref, o_ref):
    ya = _dot(attn_ref[...].astype(bf16), woa_ref[...])
    parts = []
    for h in range(RET_HEADS):
        r = ro_ref[:, h * RET_DV:(h + 1) * RET_DV]
        d = r - jnp.mean(r, axis=-1, keepdims=True)
        var = jnp.mean(d * d, axis=-1, keepdims=True)
        parts.append(d * lax.rsqrt(var + GN_EPS) * gnw_ref[:, h * RET_DV:(h + 1) * RET_DV])
    ron = jnp.concatenate(parts, axis=-1)
    rg = rg_ref[...]
    yb = _dot((rg * jax.nn.sigmoid(rg) * ron).astype(bf16), wob_ref[...])
    m = jax.nn.sigmoid(ga_ref[...]) * ya + jax.nn.sigmoid(gb_ref[...]) * yb
    o_ref[...] = x_ref[...] + g1_ref[...] * _dot(m.astype(bf16), wo_ref[...])


def _mod_spec(mod, tm, D):
    if mod.shape[1] != 1:
        return pl.BlockSpec((None, tm, D), lambda g, i: (g, i, 0))
    return pl.BlockSpec((None, 1, D), lambda g, i: (g, 0, 0))


def mix_out(proj, attn, ro, x, g1, w_oa, gn_w, w_ob, w_o, *, tm):
    G, T, D = x.shape
    full = lambda a: pl.BlockSpec(a.shape, lambda g, i: (0,) * a.ndim)
    wide = lambda blk: pl.BlockSpec((None, tm, B_V), lambda g, i: (g, i, blk))
    gn_w = gn_w.reshape(1, B_V)
    return pl.pallas_call(
        _mix_out_body,
        grid=(G, T // tm),
        in_specs=[wide(OFF_RG // B_V), wide(OFF_GA // B_V), wide(OFF_GB // B_V),
                  pl.BlockSpec((None, tm, A_Q), lambda g, i: (g, i, 0)),
                  wide(0),
                  pl.BlockSpec((None, tm, D), lambda g, i: (g, i, 0)),
                  _mod_spec(g1, tm, D), full(w_oa), full(gn_w), full(w_ob), full(w_o)],
        out_specs=pl.BlockSpec((None, tm, D), lambda g, i: (g, i, 0)),
        out_shape=jax.ShapeDtypeStruct((G, T, D), f32),
        compiler_params=_cparams(("parallel", "parallel")),
        name="mix_out",
    )(proj, proj, proj, attn, ro, x, g1, w_oa, gn_w, w_ob, w_o)


def _ffn_tail(a, am1, am2, b_ref, cw_ref, cb_ref, wout_ref, x_ref, g2_ref, nf_ref, o_ref, final_norm):
    u = cb_ref[...] + am2 * cw_ref[0:1, :] + am1 * cw_ref[1:2, :] + a * cw_ref[2:3, :]
    y = _dot((jax.nn.gelu(u) * b_ref[...]).astype(bf16), wout_ref[...])
    xo = x_ref[...] + g2_ref[...] * y
    if final_norm:
        xo = xo * lax.rsqrt(jnp.mean(xo * xo, axis=-1, keepdims=True) + RMS_EPS) * nf_ref[...]
    o_ref[...] = xo


def _ffn_seq_body(x_ref, sc_ref, sh_ref, nw_ref, win_ref, prev_ref, cw_ref, cb_ref, wout_ref, g2_ref, nf_ref,
                  o_ref, conv_ref, scr, tail_scr, *, final_norm):
    tm = x_ref.shape[0]
    F = wout_ref.shape[0]
    h = _norm_mod(x_ref[...], nw_ref, sc_ref, sh_ref)
    a = _dot(h, win_ref[:, 0:F])
    scr[8:8 + tm, :] = a
    scr[6:8, :] = jnp.where(pl.program_id(1) == 0, prev_ref[...], tail_scr[...])
    last2 = scr[pl.ds(tm + 6, 2), :]
    tail_scr[...] = last2
    conv_ref[...] = last2
    b = _dot(h, win_ref[:, F:2 * F])
    u = cb_ref[...] + scr[pl.ds(6, tm), :] * cw_ref[0:1, :] + scr[pl.ds(7, tm), :] * cw_ref[1:2, :] + a * cw_ref[2:3, :]
    y = _dot((jax.nn.gelu(u) * b).astype(bf16), wout_ref[...])
    xo = x_ref[...] + g2_ref[...] * y
    if final_norm:
        xo = xo * lax.rsqrt(jnp.mean(xo * xo, axis=-1, keepdims=True) + RMS_EPS) * nf_ref[...]
    o_ref[...] = xo


def ffn_seq(x, sc, sh, nw, w_in, prev, g2, conv_w, conv_b, w_out, normf_w, *, tm, final_norm):
    B, T, D = x.shape
    F = w_out.shape[0]
    assert T % tm == 0 and tm >= CONV_W - 1
    full = lambda a: pl.BlockSpec(a.shape, lambda g, i: (0,) * a.ndim)
    conv_b = conv_b.reshape(1, F)
    normf_w = normf_w.reshape(1, D)
    nw = nw.reshape(1, D)
    rows = pl.BlockSpec((None, tm, D), lambda g, i: (g, i, 0))
    state = pl.BlockSpec((None, CONV_W - 1, F), lambda g, i: (g, 0, 0))
    return pl.pallas_call(
        functools.partial(_ffn_seq_body, final_norm=final_norm),
        grid=(B, T // tm),
        in_specs=[rows, _mod_spec(sc, tm, D), _mod_spec(sh, tm, D), full(nw), _resident(w_in), state,
                  full(conv_w), full(conv_b), _resident(w_out), _mod_spec(g2, tm, D), full(normf_w)],
        out_specs=[rows, state],
        out_shape=[jax.ShapeDtypeStruct((B, T, D), f32), jax.ShapeDtypeStruct((B, CONV_W - 1, F), f32)],
        scratch_shapes=[pltpu.VMEM((tm + 8, F), f32), pltpu.VMEM((CONV_W - 1, F), f32)],
        compiler_params=_cparams(("parallel", "arbitrary")),
        name="ffn_seq",
    )(x, sc, sh, nw, w_in, prev, conv_w, conv_b, w_out, g2, normf_w)


def _ffn_out_rows_body(a_ref, am1_ref, am2_ref, b_ref, cw_ref, cb_ref, wout_ref, x_ref, g2_ref, nf_ref, o_ref,
                       *, final_norm):
    _ffn_tail(a_ref[...], am1_ref[...], am2_ref[...], b_ref, cw_ref, cb_ref, wout_ref, x_ref, g2_ref, nf_ref, o_ref,
              final_norm)


def ffn_out_rows(ab, am1, am2, x, g2, conv_w, conv_b, w_out, normf_w, *, tm, final_norm):
    G, T, D = x.shape
    F = w_out.shape[0]
    full = lambda a: pl.BlockSpec(a.shape, lambda g, i: (0,) * a.ndim)
    conv_b = conv_b.reshape(1, F)
    normf_w = normf_w.reshape(1, D)
    rowsF = lambda blk: pl.BlockSpec((None, tm, F), lambda g, i: (g, i, blk))
    return pl.pallas_call(
        functools.partial(_ffn_out_rows_body, final_norm=final_norm),
        grid=(G, T // tm),
        in_specs=[rowsF(0), rowsF(0), rowsF(0), rowsF(1),
                  full(conv_w), full(conv_b), full(w_out),
                  pl.BlockSpec((None, tm, D), lambda g, i: (g, i, 0)),
                  _mod_spec(g2, tm, D), full(normf_w)],
        out_specs=pl.BlockSpec((None, tm, D), lambda g, i: (g, i, 0)),
        out_shape=jax.ShapeDtypeStruct((G, T, D), f32),
        compiler_params=_cparams(("parallel", "parallel")),
        name="ffn_out_rows",
    )(ab, am1, am2, ab, conv_w, conv_b, w_out, x, g2, normf_w)


def _prep_w_in(w):
    D = w.shape[0]
    o = 0
    parts = {}
    for name, n in (("aq", A_Q), ("akv", A_KV), ("ag", A_G), ("rq", B_QK), ("rk", B_QK), ("rv", B_V), ("rg", B_V),
                    ("ga", D), ("gb", D)):
        parts[name] = w[:, o:o + n]
        o += n
    pad = jnp.zeros((D, N_TOK - OFF_AG - A_G), w.dtype)
    tok = jnp.concatenate([parts["rg"], parts["ga"], parts["gb"], parts["rv"], parts["aq"], parts["rq"], parts["rk"],
                           parts["akv"][:, :KV_ROWS], parts["ag"], pad], axis=1)
    return tok.astype(bf16), parts["akv"].T.astype(bf16)


def _prep_cmp(cmp_pos, cmp_w1, cmp_w2):
    G = NSA_KV_HEADS
    r = jnp.arange(LANE)
    perm = ((r % CPP)[:, None] * CMP_STRIDE + (r // CPP)[:, None] == r[None, :]).astype(bf16)
    pos = jnp.concatenate([cmp_pos] * G, axis=-1)
    w1 = cmp_w1.reshape(2, CMP_BLOCK, HD, CMP_HIDDEN)
    eye = jnp.eye(G, dtype=w1.dtype)
    w1bd = jnp.einsum("eldh,gk->elgdkh", w1, eye).reshape(2, CMP_BLOCK, G * HD, G * CMP_HIDDEN)
    w1cat = jnp.concatenate([w1bd[:, :CMP_STRIDE], w1bd[:, CMP_STRIDE:]], axis=-1)
    eye_e = jnp.eye(2, dtype=w1.dtype)
    w2bd = jnp.einsum("ehd,gk,ef->eghkfd", cmp_w2, eye, eye_e).reshape(2, G * CMP_HIDDEN, G * 2 * HD)
    return perm, pos, w1cat.astype(bf16), w2bd.astype(bf16)


def _kv_rows_to_out(rowsT, lead):
    t = rowsT.shape[-1]
    r = rowsT.reshape(*lead, 2, NSA_KV_HEADS, HD, t)
    n = len(lead)
    return r.transpose(*range(n), n + 3, n, n + 1, n + 2)


def _keep_last_lanes(a, n):
    t = a.shape[-1]
    if t >= n:
        return a[..., t - n:]
    return jnp.pad(a, ((0, 0),) * (a.ndim - 1) + ((n - t, 0),))


def kernel(x_prompt, x_sample, c_prompt, c_sample, cache_cmp_kv, cache_sel_kv, state_win_kv, state_ret, state_conv,
           page_table, norm1_w, ada_w, ada_b, w_in, cmp_pos, cmp_w1, cmp_w2, w_oa, ret_gn_w, w_ob, w_o, norm2_w,
           ffn_w_in, ffn_conv_w, ffn_conv_b, ffn_w_out, normf_w):
    B, T, D = x_prompt.shape
    Bd, Td, _ = x_sample.shape
    depth = w_in.shape[0]
    n_phys, page = cache_cmp_kv.shape[1], cache_cmp_kv.shape[2]
    past = page_table.shape[1] * page
    wlen = state_win_kv.shape[2]
    F = ffn_w_out.shape[1]
    Rs = Bd * Td
    tm_p = min(512, T)

    mod = ada_mod(jnp.concatenate([c_prompt, c_sample], axis=0), ada_w, ada_b)
    cmpT = cache_cmp_kv.transpose(0, 1, 3, 4, 5, 2).reshape(depth, n_phys, KV_ROWS, page)
    selT = cache_sel_kv.transpose(0, 1, 3, 4, 5, 2).reshape(depth, n_phys, KV_ROWS, page)
    winT = state_win_kv.transpose(0, 1, 3, 4, 5, 2).reshape(depth, Bd, KV_ROWS, wlen)

    xp = x_prompt
    xs = x_sample.reshape(1, Rs, D)
    outs = [[] for _ in range(10)]
    for l in range(depth):
        w_tok, w_kvT = _prep_w_in(w_in[l])
        cw = _prep_cmp(cmp_pos[l], cmp_w1[l], cmp_w2[l])
        w_oa_b, w_ob_b, w_o_b = w_oa[l].astype(bf16), w_ob[l].astype(bf16), w_o[l].astype(bf16)
        ffn_in_b, ffn_out_b = ffn_w_in[l].astype(bf16), ffn_w_out[l].astype(bf16)
        last = l == depth - 1
        mp = [mod[l, :B, k * D:(k + 1) * D].reshape(B, 1, D) for k in range(6)]
        ms = [jnp.repeat(mod[l, B:, k * D:(k + 1) * D], Td, axis=0).reshape(1, Rs, D) for k in range(6)]

        proj, kvT = norm_proj(xp, mp[1], mp[0], norm1_w[l], w_tok, w_kvT, tm=tm_p, tn=2048)
        comp = compress_prompt(kvT, cw)
        attn = nsa_prompt(proj, kvT, comp)
        ro, ret_p = retention(proj, 0, jnp.zeros((B, RET_HEADS, RET_DK, RET_DV), f32))
        x1 = mix_out(proj, attn, ro, xp, mp[2], w_oa_b, ret_gn_w[l], w_ob_b, w_o_b, tm=tm_p)
        xp, conv_p = ffn_seq(x1, mp[4], mp[3], norm2_w[l], ffn_in_b, jnp.zeros((B, CONV_W - 1, F), f32), mp[5],
                             ffn_conv_w[l], ffn_conv_b[l], ffn_out_b, normf_w, tm=min(256, T), final_norm=last)
        outs[0].append(_kv_rows_to_out(kvT[:, 0:KV_ROWS], (B,)))
        outs[2].append(_kv_rows_to_out(kvT[:, KV_ROWS:2 * KV_ROWS], (B,)))
        outs[4].append(_kv_rows_to_out(_keep_last_lanes(kvT[:, 2 * KV_ROWS:], wlen), (B,)))
        outs[6].append(ret_p)
        outs[8].append(conv_p)

        proj_s, kvT_s = norm_proj(xs, ms[1], ms[0], norm1_w[l], w_tok, w_kvT, tm=Rs, tn=2048)
        comp_s = compress_paged(cmpT, l, page_table, cw)
        q_s = proj_s[0, :, OFF_AQ:OFF_AQ + A_Q].reshape(Bd, Td, NSA_KV_HEADS, HPG, HD)
        q_s = q_s.transpose(0, 2, 3, 1, 4).reshape(Bd, NSA_HEADS * Td, HD)
        ag_s = proj_s[0, :, OFF_AG:OFF_AG + A_G].reshape(Bd, Td, NSA_KV_HEADS, HPG, 3)
        ag_s = ag_s.transpose(0, 2, 3, 1, 4).reshape(Bd, NSA_HEADS * Td, 3)
        newT = kvT_s[0].reshape(A_KV, Bd, Td).transpose(1, 0, 2)
        newT_pad = jnp.pad(newT, ((0, 0), (0, 0), (0, LANE - Td)))
        attn_s = nsa_sample(q_s, ag_s, comp_s, selT, l, page_table, winT[l], newT_pad, tn=Td, past=past)
        attn_s = attn_s.reshape(Bd, NSA_KV_HEADS, HPG, Td, HD).transpose(0, 3, 1, 2, 4).reshape(1, Rs, A_Q)
        ro_s, ret_s = retention(proj_s.reshape(Bd, Td, N_TOK), past, state_ret[l])
        x1s = mix_out(proj_s, attn_s, ro_s.reshape(1, Rs, B_V), xs, ms[2], w_oa_b, ret_gn_w[l], w_ob_b, w_o_b, tm=Rs)
        ab_s = norm_proj(x1s, ms[4], ms[3], norm2_w[l], ffn_in_b, None, tm=Rs, tn=F)
        a_ext = jnp.concatenate([state_conv[l], ab_s[0, :, :F].reshape(Bd, Td, F)], axis=1)
        am1 = a_ext[:, 1:1 + Td].reshape(1, Rs, F)
        am2 = a_ext[:, 0:Td].reshape(1, Rs, F)
        xs = ffn_out_rows(ab_s, am1, am2, x1s, ms[5], ffn_conv_w[l], ffn_conv_b[l], ffn_out_b, normf_w, tm=Rs,
                          final_norm=last)
        outs[1].append(_kv_rows_to_out(newT[:, 0:KV_ROWS], (Bd,)))
        outs[3].append(_kv_rows_to_out(newT[:, KV_ROWS:2 * KV_ROWS], (Bd,)))
        win_all = jnp.concatenate([winT[l], newT[:, 2 * KV_ROWS:]], axis=-1)
        outs[5].append(_kv_rows_to_out(_keep_last_lanes(win_all, wlen), (Bd,)))
        outs[7].append(ret_s)
        outs[9].append(a_ext[:, Td:])

    st = lambda k: jnp.stack(outs[k])
    return (xp, xs.reshape(Bd, Td, D), st(0), st(1), st(2), st(3), st(4), st(5), st(6), st(7), st(8), st(9))
```

```python
import functools

import jax
import jax.numpy as jnp
from jax import lax
from jax.experimental import pallas as pl
from jax.experimental.pallas import tpu as pltpu

f32 = jnp.float32
bf16 = jnp.bfloat16

NSA_HEADS = 8
NSA_KV_HEADS = 2
HPG = NSA_HEADS // NSA_KV_HEADS
HD = 64
CMP_STRIDE = 16
CMP_BLOCK = 32
CMP_HIDDEN = 64
SEL_BLOCK = 64
SEL_SHIFT = 6
N_SEL = 16
WINDOW = 512
RET_HEADS = 4
RET_DK = 128
RET_DV = 256
RET_CHUNK = 128
ROPE_BASE = 10000.0
CONV_W = 3
RMS_EPS = 1e-6
GN_EPS = 1e-5
NEG = -1e30
FORCE_BONUS = 1e4

A_Q = NSA_HEADS * HD
KV_ROWS = 2 * NSA_KV_HEADS * HD
A_KV = 3 * KV_ROWS
A_G = 3 * NSA_HEADS
B_QK = RET_HEADS * RET_DK
B_V = RET_HEADS * RET_DV

OFF_RG, OFF_GA, OFF_GB, OFF_RV = 0, 1024, 2048, 3072
OFF_AQ, OFF_RQ, OFF_RK, OFF_KVC, OFF_AG = 4096, 4608, 5120, 5632, 5888
N_TOK = 6144

LANE = 128
VMEM_LIMIT = 56 * 1024 * 1024


def _cparams(sem):
    return pltpu.CompilerParams(dimension_semantics=sem, vmem_limit_bytes=VMEM_LIMIT)


def _dot(a, b):
    return jnp.dot(a, b, preferred_element_type=f32)


def _dot_nt(a, b):
    return lax.dot_general(a, b, (((1,), (1,)), ((), ())), preferred_element_type=f32)


def _dot_tn(a, b):
    return lax.dot_general(a, b, (((0,), (0,)), ((), ())), preferred_element_type=f32)


def _split_dot(a, b_bf16):
    hi = a.astype(bf16)
    lo = (a - hi.astype(f32)).astype(bf16)
    return _dot(hi, b_bf16) + _dot(lo, b_bf16)


def _ada_body(c_ref, w_ref, b_ref, o_ref):
    c = c_ref[...]
    s = c * jax.nn.sigmoid(c)
    o_ref[...] = _dot(s.astype(bf16), w_ref[...].astype(bf16)) + b_ref[...]


def ada_mod(c_all, ada_w, ada_b):
    depth, d, n = ada_w.shape
    r = c_all.shape[0]
    tn = 1024
    return pl.pallas_call(
        _ada_body,
        grid=(depth, n // tn),
        in_specs=[pl.BlockSpec((r, d), lambda l, j: (0, 0)),
                  pl.BlockSpec((None, d, tn), lambda l, j: (l, 0, j)),
                  pl.BlockSpec((None, 1, tn), lambda l, j: (l, 0, j))],
        out_specs=pl.BlockSpec((None, r, tn), lambda l, j: (l, 0, j)),
        out_shape=jax.ShapeDtypeStruct((depth, r, n), f32),
        compiler_params=_cparams(("parallel", "parallel")),
        name="ada_mod",
    )(c_all, ada_w, ada_b.reshape(depth, 1, n))


def _norm_proj_body(x_ref, sc_ref, sh_ref, nw_ref, w_ref, *rest, has_t):
    if has_t:
        wt_ref, o_ref, ot_ref, h_scr = rest
    else:
        o_ref, h_scr = rest

    j = pl.program_id(2)
    tn = o_ref.shape[1]

    @pl.when(j == 0)
    def _():
        h_scr[...] = _norm_mod(x_ref[...], nw_ref, sc_ref, sh_ref)
        if has_t:
            ot_ref[...] = _dot_nt(wt_ref[...], h_scr[...])

    o_ref[...] = _dot(h_scr[...], w_ref[:, pl.ds(pl.multiple_of(j * tn, tn), tn)])


def _norm_mod(x, nw_ref, sc_ref, sh_ref):
    y = x * lax.rsqrt(jnp.mean(x * x, axis=-1, keepdims=True) + RMS_EPS) * nw_ref[...]
    return (y * (1.0 + sc_ref[...]) + sh_ref[...]).astype(bf16)


def _resident(a):
    return pl.BlockSpec(a.shape, lambda *_: (0,) * a.ndim, pipeline_mode=pl.Buffered(1))


def norm_proj(x, sc, sh, nw, w, wt, *, tm, tn):
    G, T, D = x.shape
    N = w.shape[1]
    per_row = sc.shape[1] != 1
    mr = tm if per_row else 1
    mod_spec = pl.BlockSpec((None, mr, D), (lambda g, i, j: (g, i, 0)) if per_row else (lambda g, i, j: (g, 0, 0)))
    in_specs = [pl.BlockSpec((None, tm, D), lambda g, i, j: (g, i, 0)), mod_spec, mod_spec,
                pl.BlockSpec((1, D), lambda g, i, j: (0, 0)),
                _resident(w)]
    out_specs = [pl.BlockSpec((None, tm, tn), lambda g, i, j: (g, i, j))]
    out_shape = [jax.ShapeDtypeStruct((G, T, N), f32)]
    args = [x, sc, sh, nw.reshape(1, D), w]
    if wt is not None:
        NT = wt.shape[0]
        in_specs.append(_resident(wt))
        out_specs.append(pl.BlockSpec((None, NT, tm), lambda g, i, j: (g, 0, i)))
        out_shape.append(jax.ShapeDtypeStruct((G, NT, T), f32))
        args.append(wt)
    res = pl.pallas_call(
        functools.partial(_norm_proj_body, has_t=wt is not None),
        grid=(G, T // tm, N // tn),
        in_specs=in_specs, out_specs=out_specs, out_shape=out_shape,
        scratch_shapes=[pltpu.VMEM((tm, D), bf16)],
        compiler_params=_cparams(("parallel", "parallel", "arbitrary")),
        name="norm_proj_t" if wt is not None else "norm_proj",
    )(*args)
    return res if wt is not None else res[0]


CPP = LANE // CMP_STRIDE
assert CMP_BLOCK == 2 * CMP_STRIDE


def _compress_pages(get_page, n_pages, perm_ref, pos_ref, w1_ref, w2_ref, o_ref, xs_ref, acc_ref):
    n = o_ref.shape[0]
    perm = perm_ref[...]
    for k in range(n_pages):
        y = _dot_nt(perm, get_page(k).astype(bf16))
        for e in range(2):
            for s in range(CMP_STRIDE):
                xs_ref[e, s, k * CPP:(k + 1) * CPP, :] = y[s * CPP:(s + 1) * CPP, e * LANE:(e + 1) * LANE]
    for e in range(2):
        for s in range(CMP_STRIDE):
            xs_ref[e, s, n:n + 8, :] = jnp.zeros((8, LANE), f32)
            xs_ref[e, s, n + 1:n + 2, :] = pos_ref[e, s:s + 1, :]
            xs_ref[e, s, n + 2:n + 3, :] = pos_ref[e, CMP_STRIDE + s:CMP_STRIDE + s + 1, :]
    out = jnp.zeros((n, 2 * LANE), f32)
    for e in range(2):
        acc = jnp.zeros((n + 8, 2 * LANE), f32)
        for s in range(CMP_STRIDE):
            acc = acc + _dot(xs_ref[e, s].astype(bf16), w1_ref[e, s])
        acc_ref[...] = acc
        bias = acc_ref[n + 1:n + 2, 0:LANE] + acc_ref[n + 2:n + 3, LANE:2 * LANE]
        hid = acc_ref[0:n, 0:LANE] + acc_ref[pl.ds(1, n), LANE:2 * LANE] + bias
        out = out + _dot(jax.nn.gelu(hid).astype(bf16), w2_ref[e])
    o_ref[...] = out


def _cmp_weight_specs():
    z = lambda nd: (lambda *a: (0,) * nd)
    return [pl.BlockSpec((LANE, LANE), z(2)),
            pl.BlockSpec((2, CMP_BLOCK, LANE), z(3)),
            pl.BlockSpec((2, CMP_STRIDE, LANE, 2 * LANE), z(4)),
            pl.BlockSpec((2, LANE, 2 * LANE), z(3))]


def _cmp_scratch(n):
    return [pltpu.VMEM((2, CMP_STRIDE, n + 8, LANE), f32), pltpu.VMEM((n + 8, 2 * LANE), f32)]


def _compress_prompt_body(kvT_ref, perm_ref, pos_ref, w1_ref, w2_ref, o_ref, xs_ref, acc_ref):
    n_pages = kvT_ref.shape[1] // LANE
    get = lambda k: kvT_ref[:, k * LANE:(k + 1) * LANE]
    _compress_pages(get, n_pages, perm_ref, pos_ref, w1_ref, w2_ref, o_ref, xs_ref, acc_ref)


def compress_prompt(kvT, cw):
    B, _, T = kvT.shape
    assert T % LANE == 0
    n = T // CMP_STRIDE
    return pl.pallas_call(
        _compress_prompt_body,
        grid=(B,),
        in_specs=[pl.BlockSpec((None, KV_ROWS, T), lambda b: (b, 0, 0))] + _cmp_weight_specs(),
        out_specs=pl.BlockSpec((None, n, 2 * LANE), lambda b: (b, 0, 0)),
        out_shape=jax.ShapeDtypeStruct((B, n, 2 * LANE), f32),
        scratch_shapes=_cmp_scratch(n),
        compiler_params=_cparams(("parallel",)),
        name="compress_prompt",
    )(kvT, *cw)


def _page_specs(layer, n_pages, page):
    return [pl.BlockSpec((None, None, KV_ROWS, page), lambda b, pt, k=k: (layer, pt[b, k], 0, 0))
            for k in range(n_pages)]


def _compress_paged_body(pt_ref, perm_ref, pos_ref, w1_ref, w2_ref, *rest, n_pages):
    pages, (o_ref, xs_ref, acc_ref) = rest[:n_pages], rest[n_pages:]
    get = lambda k: pages[k][...]
    _compress_pages(get, n_pages, perm_ref, pos_ref, w1_ref, w2_ref, o_ref, xs_ref, acc_ref)


def compress_paged(cacheT, layer, page_table, cw):
    Bd, n_pages = page_table.shape
    page = cacheT.shape[-1]
    assert page == LANE
    L = n_pages * page
    n = L // CMP_STRIDE
    return pl.pallas_call(
        functools.partial(_compress_paged_body, n_pages=n_pages),
        grid_spec=pltpu.PrefetchScalarGridSpec(
            num_scalar_prefetch=1,
            grid=(Bd,),
            in_specs=_cmp_weight_specs() + _page_specs(layer, n_pages, page),
            out_specs=pl.BlockSpec((None, n, 2 * LANE), lambda b, pt: (b, 0, 0)),
            scratch_shapes=_cmp_scratch(n),
        ),
        out_shape=jax.ShapeDtypeStruct((Bd, n, 2 * LANE), f32),
        compiler_params=_cparams(("parallel",)),
        name="compress_paged",
    )(page_table, *cw, *([cacheT] * n_pages))


def _select_blocks(imp, s_ok, forced, n_sel, jj):
    k = min(N_SEL, n_sel)
    score = jnp.where(s_ok, imp + jnp.where(forced, FORCE_BONUS, 0.0), NEG)
    rank = jnp.zeros(score.shape, f32)
    for j2 in range(n_sel):
        col = score[:, j2:j2 + 1]
        ge = jnp.where(col >= score, 1.0, 0.0)
        gt = jnp.where(col > score, 1.0, 0.0)
        rank = rank + jnp.where(jj > j2, ge, gt)
    return jnp.where((rank < k) & s_ok, 1.0, 0.0)


def _select_blocks_t(imp, t0, n_sel):
    tq, sw = imp.shape
    rows = -(-n_sel // 8) * 8
    k = min(N_SEL, n_sel)
    jj = lax.broadcasted_iota(jnp.int32, (rows, tq), 0)
    qpos = t0 + lax.broadcasted_iota(jnp.int32, (rows, tq), 1)
    cur = jnp.right_shift(qpos, SEL_SHIFT)
    s_ok = (jj * SEL_BLOCK <= qpos) & (jj < n_sel)
    forced = (jj == 0) | (jj == cur) | (jj == cur - 1)
    imp_t = jnp.concatenate([imp[c * LANE:(c + 1) * LANE].T for c in range(tq // LANE)], axis=1)[:rows]
    score = jnp.where(s_ok, imp_t + jnp.where(forced, FORCE_BONUS, 0.0), NEG)
    rank = jnp.zeros(score.shape, f32)
    for j2 in range(n_sel):
        row = score[j2:j2 + 1, :]
        ge = jnp.where(row >= score, 1.0, 0.0)
        gt = jnp.where(row > score, 1.0, 0.0)
        rank = rank + jnp.where(jj > j2, ge, gt)
    sel_t = jnp.where((rank < k) & s_ok, 1.0, 0.0)
    sel_t = jnp.concatenate([sel_t, jnp.zeros((sw - rows, tq), f32)], axis=0)
    return jnp.concatenate([sel_t[:, c * LANE:(c + 1) * LANE].T for c in range(tq // LANE)], axis=0)


def _softmax_rows(sm, ok):
    m = jnp.max(sm, axis=-1, keepdims=True)
    e = jnp.where(ok, jnp.exp(sm - m), 0.0)
    den = jnp.sum(e, axis=-1, keepdims=True)
    return e / jnp.maximum(den, 1e-30)


def _pv_and_rowsum(e, vT):
    v_aug = jnp.concatenate([vT, jnp.ones(vT.shape, vT.dtype)], axis=0)
    return _dot_nt(e.astype(bf16), v_aug)


def _normalise(acc):
    return (acc / pltpu.roll(acc, HD, axis=1))[:, :HD]


def _nsa_prompt_body(q_ref, ag_ref, comp_ref, ksT_ref, vsT_ref, kwT_ref, vwT_ref, ov_ref, ex_ref, o_ref,
                     *, tq, n_cmp, n_sel, wk, ck):
    i = pl.program_id(1)
    t0 = i * tq
    scale = HD ** -0.5
    NC = comp_ref.shape[0]
    SW = ov_ref.shape[1]
    qpos = t0 + lax.broadcasted_iota(jnp.int32, (tq, 1), 0)
    col_n = lax.broadcasted_iota(jnp.int32, (tq, NC), 1)
    c_ok = ((col_n * CMP_STRIDE + (CMP_BLOCK - 1)) <= qpos) & (col_n < n_cmp)
    jj = lax.broadcasted_iota(jnp.int32, (tq, SW), 1)
    s_ok = (jj * SEL_BLOCK <= qpos) & (jj < n_sel)
    wstart = pl.multiple_of(jnp.maximum(t0 + tq - wk, 0), LANE)
    kposw = wstart + lax.broadcasted_iota(jnp.int32, (tq, wk), 1)
    rel = qpos - kposw
    w_bias4 = jnp.concatenate([jnp.where((rel >= 0) & (rel < WINDOW), 0.0, NEG)] * HPG, axis=0)
    c_ok4 = jnp.concatenate([c_ok] * HPG, axis=0)
    sg = jax.nn.sigmoid(ag_ref[...])
    n_chunks = (t0 + tq + ck - 1) // ck
    kcol = lax.broadcasted_iota(jnp.int32, (tq, ck), 1)
    all_valid_selected = t0 + tq <= min(N_SEL, n_sel) * SEL_BLOCK

    for g in range(NSA_KV_HEADS):
        kc = comp_ref[:, g * LANE:g * LANE + HD].astype(bf16)
        vc = comp_ref[:, g * LANE + HD:(g + 1) * LANE].astype(bf16)
        q4 = jnp.concatenate([(q_ref[:, (g * HPG + h) * HD:(g * HPG + h + 1) * HD] * scale).astype(bf16)
                              for h in range(HPG)], axis=0)
        p = _softmax_rows(jnp.where(c_ok4, _dot_nt(q4, kc), NEG), c_ok4)
        o_c = _dot(p.astype(bf16), vc)
        psum = p[0:tq]
        for h in range(1, HPG):
            psum = psum + p[h * tq:(h + 1) * tq]
        imp = _split_dot(psum, ov_ref[...])
        sel = lax.cond(all_valid_selected,
                       lambda: jnp.where(s_ok, 1.0, 0.0),
                       lambda: _select_blocks_t(imp, t0, n_sel))
        selb = sel.astype(bf16)

        def chunk_step(c, carry):
            k0 = pl.multiple_of(c * ck, ck)
            kT = ksT_ref[g * HD:(g + 1) * HD, pl.ds(k0, ck)].astype(bf16)
            vT = vsT_ref[g * HD:(g + 1) * HD, pl.ds(k0, ck)].astype(bf16)
            ok = (_dot(selb, ex_ref[:, pl.ds(k0, ck)]) > 0.5) & (k0 + kcol <= qpos)
            bias = jnp.where(ok, 0.0, NEG)
            m, acc = carry
            s = _dot(q4, kT) + jnp.concatenate([bias] * HPG, axis=0)
            m_new = jnp.maximum(m, jnp.max(s, axis=-1, keepdims=True))
            e = jnp.exp(s - m_new)
            acc = jnp.exp(m - m_new) * acc + _pv_and_rowsum(e, vT)
            return m_new, acc

        init = (jnp.full((HPG * tq, 1), NEG, f32), jnp.zeros((HPG * tq, 2 * HD), f32))
        _, acc = lax.fori_loop(0, n_chunks, chunk_step, init)
        o_s = _normalise(acc)

        kw = kwT_ref[g * HD:(g + 1) * HD, pl.ds(wstart, wk)].astype(bf16)
        vw = vwT_ref[g * HD:(g + 1) * HD, pl.ds(wstart, wk)].astype(bf16)
        s = _dot(q4, kw) + w_bias4
        e = jnp.exp(s - jnp.max(s, axis=-1, keepdims=True))
        o_w = _normalise(_pv_and_rowsum(e, vw))

        gate = lambda br: jnp.concatenate(
            [sg[:, 3 * (g * HPG + h) + br:3 * (g * HPG + h) + br + 1] for h in range(HPG)], axis=0)
        o = gate(0) * o_c + gate(1) * o_s + gate(2) * o_w
        for h in range(HPG):
            o_ref[:, (g * HPG + h) * HD:(g * HPG + h + 1) * HD] = o[h * tq:(h + 1) * tq]


def _overlap_matrix(nc_rows, n_cmp, sw_cols, n_sel):
    cs = jnp.arange(nc_rows) * CMP_STRIDE
    ss = jnp.arange(sw_cols) * SEL_BLOCK
    ok = (cs[:, None] < ss[None, :] + SEL_BLOCK) & (cs[:, None] + CMP_BLOCK > ss[None, :])
    ok = ok & (jnp.arange(nc_rows)[:, None] < n_cmp) & (jnp.arange(sw_cols)[None, :] < n_sel)
    return ok.astype(bf16)


def nsa_prompt(proj, kvT, comp):
    B, T, _ = proj.shape
    tq = min(256, T)
    n_cmp = T // CMP_STRIDE - (CMP_BLOCK // CMP_STRIDE) + 1
    n_sel = -(-T // SEL_BLOCK)
    NC = comp.shape[1]
    SW = LANE
    assert n_sel <= SW and T % tq == 0
    wk = min(WINDOW + tq, T)
    ck = min(512, T)
    assert T % ck == 0
    ov = _overlap_matrix(NC, n_cmp, SW, n_sel)
    ex = (jnp.arange(SW)[:, None] == (jnp.arange(T)[None, :] // SEL_BLOCK)).astype(bf16)
    kv_spec = lambda blk: pl.BlockSpec((None, LANE, T), lambda b, i: (b, blk, 0))
    return pl.pallas_call(
        functools.partial(_nsa_prompt_body, tq=tq, n_cmp=n_cmp, n_sel=n_sel, wk=wk, ck=ck),
        grid=(B, T // tq),
        in_specs=[pl.BlockSpec((None, tq, A_Q), lambda b, i: (b, i, OFF_AQ // A_Q)),
                  pl.BlockSpec((None, tq, LANE), lambda b, i: (b, i, OFF_AG // LANE)),
                  pl.BlockSpec((None, NC, 2 * LANE), lambda b, i: (b, 0, 0)),
                  kv_spec(2), kv_spec(3), kv_spec(4), kv_spec(5),
                  pl.BlockSpec((NC, SW), lambda b, i: (0, 0)),
                  pl.BlockSpec((SW, T), lambda b, i: (0, 0))],
        out_specs=pl.BlockSpec((None, tq, A_Q), lambda b, i: (b, i, 0)),
        out_shape=jax.ShapeDtypeStruct((B, T, A_Q), f32),
        compiler_params=_cparams(("parallel", "arbitrary")),
        name="nsa_prompt",
    )(proj, proj, comp, kvT, kvT, kvT, kvT, ov, ex)


def _nsa_sample_body(pt_ref, q_ref, ag_ref, comp_ref, kwT_ref, newT_ref, ov_ref, rsum_ref, rexp_ref, ex_ref, *rest,
                     tn, past, n_cmp, n_sel, n_pages):
    pages, o_ref = rest[:n_pages], rest[n_pages]
    scale = HD ** -0.5
    R = HPG * tn
    NC = comp_ref.shape[0]
    SW = ov_ref.shape[1]
    tok = lax.rem(lax.broadcasted_iota(jnp.int32, (R, 1), 0), tn)
    qpos = past + tok
    col_n = lax.broadcasted_iota(jnp.int32, (R, NC), 1)
    c_ok = ((col_n * CMP_STRIDE + (CMP_BLOCK - 1)) <= qpos) & (col_n < n_cmp)
    jj = lax.broadcasted_iota(jnp.int32, (tn, SW), 1)
    qpos_t = past + lax.broadcasted_iota(jnp.int32, (tn, 1), 0)
    cur = jnp.right_shift(qpos_t, SEL_SHIFT)
    s_ok = (jj * SEL_BLOCK <= qpos_t) & (jj < n_sel)
    forced = (jj == 0) | (jj == cur) | (jj == cur - 1)
    wlen = kwT_ref.shape[1]
    kposw = (past - wlen) + lax.broadcasted_iota(jnp.int32, (R, wlen), 1)
    relw = qpos - kposw
    w_ok = (relw >= 0) & (relw < WINDOW)
    nl = newT_ref.shape[1]
    u = lax.broadcasted_iota(jnp.int32, (R, nl), 1)
    reln = tok - u
    n_ok = (reln >= 0) & (reln < WINDOW) & (u < tn)
    nb = past // SEL_BLOCK
    sg = jax.nn.sigmoid(ag_ref[...])
    for g in range(NSA_KV_HEADS):
        q = (q_ref[g * R:(g + 1) * R, :] * scale).astype(bf16)
        kc = comp_ref[:, g * LANE:g * LANE + HD].astype(bf16)
        vc = comp_ref[:, g * LANE + HD:(g + 1) * LANE].astype(bf16)
        pc = _softmax_rows(jnp.where(c_ok, _dot_nt(q, kc), NEG), c_ok)
        o_c = _dot(pc.astype(bf16), vc)
        imp = _split_dot(_rsum_exact(rsum_ref, pc), ov_ref[...])
        sel = _select_blocks(imp, s_ok, forced, n_sel, jj)
        selr = _dot(rexp_ref[...], sel.astype(bf16))
        kwp = kwT_ref[g * HD:(g + 1) * HD, :].astype(bf16)
        vwp = kwT_ref[2 * HD + g * HD:2 * HD + (g + 1) * HD, :].astype(bf16)
        base = 2 * KV_ROWS
        kwn = newT_ref[base + g * HD:base + (g + 1) * HD, :].astype(bf16)
        vwn = newT_ref[base + 2 * HD + g * HD:base + 2 * HD + (g + 1) * HD, :].astype(bf16)
        s1 = jnp.where(w_ok, _dot(q, kwp), NEG)
        s2 = jnp.where(n_ok, _dot(q, kwn), NEG)
        m = jnp.maximum(jnp.max(s1, axis=-1, keepdims=True), jnp.max(s2, axis=-1, keepdims=True))
        e1 = jnp.exp(s1 - m)
        e2 = jnp.exp(s2 - m)
        den = jnp.sum(e1, axis=-1, keepdims=True) + jnp.sum(e2, axis=-1, keepdims=True)
        o_w = (_dot_nt(e1.astype(bf16), vwp) + _dot_nt(e2.astype(bf16), vwn)) / den
        base = KV_ROWS
        ksn = newT_ref[base + g * HD:base + (g + 1) * HD, :].astype(bf16)
        vsn = newT_ref[base + 2 * HD + g * HD:base + 2 * HD + (g + 1) * HD, :].astype(bf16)
        ksp = jnp.concatenate([pg[g * HD:(g + 1) * HD, :] for pg in pages], axis=1).astype(bf16)
        vsp = jnp.concatenate([pg[2 * HD + g * HD:2 * HD + (g + 1) * HD, :] for pg in pages], axis=1).astype(bf16)
        okp = _dot(selr.astype(bf16), ex_ref[...]) > 0.5
        okn = (u <= tok) & (u < tn) & (selr[:, nb:nb + 1] > 0.5)
        s1 = jnp.where(okp, _dot(q, ksp), NEG)
        s2 = jnp.where(okn, _dot(q, ksn), NEG)
        m = jnp.maximum(jnp.max(s1, axis=-1, keepdims=True), jnp.max(s2, axis=-1, keepdims=True))
        e1 = jnp.exp(s1 - m)
        e2 = jnp.exp(s2 - m)
        den = jnp.sum(e1, axis=-1, keepdims=True) + jnp.sum(e2, axis=-1, keepdims=True)
        o_s = (_dot_nt(e1.astype(bf16), vsp) + _dot_nt(e2.astype(bf16), vsn)) / den
        gg = sg[g * R:(g + 1) * R, :]
        o_ref[g * R:(g + 1) * R, :] = gg[:, 0:1] * o_c + gg[:, 1:2] * o_s + gg[:, 2:3] * o_w


def _rsum_exact(rsum_ref, pc):
    r = rsum_ref[...]
    hi = pc.astype(bf16)
    lo = (pc - hi.astype(f32)).astype(bf16)
    return _dot(r, hi) + _dot(r, lo)


def nsa_sample(q_s, ag_s, comp, selT, layer, page_table, winT, newT, *, tn, past):
    Bd, n_pages = page_table.shape
    page = selT.shape[-1]
    R = HPG * tn
    L = past + tn
    n_cmp = L // CMP_STRIDE - (CMP_BLOCK // CMP_STRIDE) + 1
    n_sel = -(-L // SEL_BLOCK)
    NC = comp.shape[1]
    SW = -(-n_sel // LANE) * LANE
    assert past % page == 0 and page % SEL_BLOCK == 0 and tn <= SEL_BLOCK and n_cmp <= NC
    wlen = winT.shape[-1]
    ov = _overlap_matrix(NC, n_cmp, SW, n_sel)
    rsum = (jnp.arange(tn)[:, None] == (jnp.arange(R)[None, :] % tn)).astype(bf16)
    ex = (jnp.arange(SW)[:, None] == (jnp.arange(past)[None, :] // SEL_BLOCK)).astype(bf16)
    const = lambda *shape: pl.BlockSpec(shape, lambda b, pt: (0,) * len(shape))
    return pl.pallas_call(
        functools.partial(_nsa_sample_body, tn=tn, past=past, n_cmp=n_cmp, n_sel=n_sel, n_pages=n_pages),
        grid_spec=pltpu.PrefetchScalarGridSpec(
            num_scalar_prefetch=1,
            grid=(Bd,),
            in_specs=[pl.BlockSpec((None, 2 * R, HD), lambda b, pt: (b, 0, 0)),
                      pl.BlockSpec((None, 2 * R, 3), lambda b, pt: (b, 0, 0)),
                      pl.BlockSpec((None, NC, 2 * LANE), lambda b, pt: (b, 0, 0)),
                      pl.BlockSpec((None, None, KV_ROWS, wlen), lambda b, pt: (layer, b, 0, 0)),
                      pl.BlockSpec((None, A_KV, LANE), lambda b, pt: (b, 0, 0)),
                      const(NC, SW), const(tn, R), const(R, tn), const(SW, past)]
            + _page_specs(layer, n_pages, page),
            out_specs=pl.BlockSpec((None, 2 * R, HD), lambda b, pt: (b, 0, 0)),
        ),
        out_shape=jax.ShapeDtypeStruct((Bd, 2 * R, HD), f32),
        compiler_params=_cparams(("parallel",)),
        name="nsa_sample",
    )(page_table, q_s, ag_s, comp, winT, newT, ov, rsum, rsum.T, ex, *([selT] * n_pages))


def _retention_body(q_ref, k_ref, v_ref, cos_ref, sin_ref, dm_ref, qd_ref, kd_ref, sd_ref, s0_ref, o_ref, sn_ref,
                    s_scr):
    c = pl.program_id(1)

    @pl.when(c == 0)
    def _():
        s_scr[...] = s0_ref[...]

    C = dm_ref.shape[1]
    for cc in range(q_ref.shape[0] // C):
        r = slice(cc * C, (cc + 1) * C)
        cos = cos_ref[r, :]
        sin = sin_ref[r, :]

        def rot(x):
            return x * cos + pltpu.roll(x, RET_DK // 2, axis=1) * sin

        for h in range(RET_HEADS):
            q = rot(q_ref[r, h * RET_DK:(h + 1) * RET_DK])
            k = rot(k_ref[r, h * RET_DK:(h + 1) * RET_DK]) * (RET_DK ** -0.5)
            v = v_ref[r, h * RET_DV:(h + 1) * RET_DV].astype(bf16)
            S = s_scr[h]
            inner = _dot_nt(q.astype(bf16), k.astype(bf16)) * dm_ref[h]
            o_ref[r, h * RET_DV:(h + 1) * RET_DV] = (_dot(inner.astype(bf16), v)
                                                     + _dot((q * qd_ref[h]).astype(bf16), S.astype(bf16)))
            s_scr[h] = S * sd_ref[h] + _dot_tn((k * kd_ref[h]).astype(bf16), v)

    @pl.when(c == pl.num_programs(1) - 1)
    def _():
        sn_ref[...] = s_scr[...]


def retention(proj, pos0, states, layer):
    B, T, _ = proj.shape
    C = RET_CHUNK if T % RET_CHUNK == 0 else T
    n = T // C
    H = RET_HEADS
    half = RET_DK // 2
    inv = jnp.exp(-jnp.log(ROPE_BASE) * jnp.arange(half, dtype=f32) / half)
    ang = (pos0 + jnp.arange(T)).astype(f32)[:, None] * inv[None, :]
    cos = jnp.concatenate([jnp.cos(ang), jnp.cos(ang)], axis=-1)
    sin = jnp.concatenate([-jnp.sin(ang), jnp.sin(ang)], axis=-1)
    log_g = jnp.log(1.0 - jnp.exp2(-5.0 - jnp.arange(H, dtype=f32)))
    i = jnp.arange(C, dtype=f32)
    diff = i[:, None] - i[None, :]
    dm = jnp.where(diff >= 0, jnp.exp(jnp.maximum(diff, 0.0)[None] * log_g[:, None, None]), 0.0)
    qd = jnp.exp((i + 1.0)[None, :] * log_g[:, None])[..., None]
    kd = jnp.exp((C - 1.0 - i)[None, :] * log_g[:, None])[..., None]
    sd = jnp.exp(C * log_g)[:, None, None]
    tab = lambda a: pl.BlockSpec(a.shape, lambda b, c: (0, 0, 0))
    st = pl.BlockSpec((None, H, RET_DK, RET_DV), lambda b, c: (b, 0, 0, 0))
    cps = 4 if n % 4 == 0 else 1
    R = cps * C
    return pl.pallas_call(
        _retention_body,
        grid=(B, n // cps),
        in_specs=[pl.BlockSpec((None, R, B_QK), lambda b, c: (b, c, OFF_RQ // B_QK)),
                  pl.BlockSpec((None, R, B_QK), lambda b, c: (b, c, OFF_RK // B_QK)),
                  pl.BlockSpec((None, R, B_V), lambda b, c: (b, c, OFF_RV // B_V)),
                  pl.BlockSpec((R, RET_DK), lambda b, c: (c, 0)),
                  pl.BlockSpec((R, RET_DK), lambda b, c: (c, 0)),
                  tab(dm), tab(qd), tab(kd), tab(sd),
                  pl.BlockSpec((None, None, H, RET_DK, RET_DV), lambda b, c: (layer, b, 0, 0, 0))],
        out_specs=[pl.BlockSpec((None, R, B_V), lambda b, c: (b, c, 0)), st],
        out_shape=[jax.ShapeDtypeStruct((B, T, B_V), f32), jax.ShapeDtypeStruct((B, H, RET_DK, RET_DV), f32)],
        scratch_shapes=[pltpu.VMEM((H, RET_DK, RET_DV), f32)],
        compiler_params=_cparams(("parallel", "arbitrary")),
        name="retention",
    )(proj, proj, proj, cos, sin, dm, qd, kd, sd, states)


def _mix_out_body(rg_ref, ga_ref, gb_ref, attn_ref, ro_ref, x_ref, g1_ref, woa_ref, gnw_ref, wob_ref, wo_ref, o_ref):
    ya = _dot(attn_ref[...].astype(bf16), woa_ref[...])
    parts = []
    for h in range(RET_HEADS):
        r = ro_ref[:, h * RET_DV:(h + 1) * RET_DV]
        d = r - jnp.mean(r, axis=-1, keepdims=True)
        var = jnp.mean(d * d, axis=-1, keepdims=True)
        parts.append(d * lax.rsqrt(var + GN_EPS) * gnw_ref[:, h * RET_DV:(h + 1) * RET_DV])
    ron = jnp.concatenate(parts, axis=-1)
    rg = rg_ref[...]
    yb = _dot((rg * jax.nn.sigmoid(rg) * ron).astype(bf16), wob_ref[...])
    m = jax.nn.sigmoid(ga_ref[...]) * ya + jax.nn.sigmoid(gb_ref[...]) * yb
    o_ref[...] = x_ref[...] + g1_ref[...] * _dot(m.astype(bf16), wo_ref[...])


def _mod_spec(mod, tm, D):
    if mod.shape[1] != 1:
        return pl.BlockSpec((None, tm, D), lambda g, i: (g, i, 0))
    return pl.BlockSpec((None, 1, D), lambda g, i: (g, 0, 0))


def mix_out(proj, attn, ro, x, g1, w_oa, gn_w, w_ob, w_o, *, tm):
    G, T, D = x.shape
    full = lambda a: pl.BlockSpec(a.shape, lambda g, i: (0,) * a.ndim)
    wide = lambda blk: pl.BlockSpec((None, tm, B_V), lambda g, i: (g, i, blk))
    gn_w = gn_w.reshape(1, B_V)
    return pl.pallas_call(
        _mix_out_body,
        grid=(G, T // tm),
        in_specs=[wide(OFF_RG // B_V), wide(OFF_GA // B_V), wide(OFF_GB // B_V),
                  pl.BlockSpec((None, tm, A_Q), lambda g, i: (g, i, 0)),
                  wide(0),
                  pl.BlockSpec((None, tm, D), lambda g, i: (g, i, 0)),
                  _mod_spec(g1, tm, D), full(w_oa), full(gn_w), full(w_ob), full(w_o)],
        out_specs=pl.BlockSpec((None, tm, D), lambda g, i: (g, i, 0)),
        out_shape=jax.ShapeDtypeStruct((G, T, D), f32),
        compiler_params=_cparams(("parallel", "parallel")),
        name="mix_out",
    )(proj, proj, proj, attn, ro, x, g1, w_oa, gn_w, w_ob, w_o)


def _ffn_tail(a, am1, am2, b_ref, cw_ref, cb_ref, wout_ref, x_ref, g2_ref, nf_ref, o_ref, final_norm):
    u = cb_ref[...] + am2 * cw_ref[0:1, :] + am1 * cw_ref[1:2, :] + a * cw_ref[2:3, :]
    y = _dot((jax.nn.gelu(u) * b_ref[...]).astype(bf16), wout_ref[...])
    xo = x_ref[...] + g2_ref[...] * y
    if final_norm:
        xo = xo * lax.rsqrt(jnp.mean(xo * xo, axis=-1, keepdims=True) + RMS_EPS) * nf_ref[...]
    o_ref[...] = xo


def _ffn_seq_body(x_ref, sc_ref, sh_ref, nw_ref, win_ref, prev_ref, cw_ref, cb_ref, wout_ref, g2_ref, nf_ref,
                  o_ref, conv_ref, scr, tail_scr, *, final_norm):
    tm = x_ref.shape[0]
    F = wout_ref.shape[0]
    fc = scr.shape[1]
    h = _norm_mod(x_ref[...], nw_ref, sc_ref, sh_ref)
    first = pl.program_id(1) == 0
    y = jnp.zeros(o_ref.shape, f32)
    for c0 in range(0, F, fc):
        cols = slice(c0, c0 + fc)
        a = _dot(h, win_ref[:, cols])
        scr[8:8 + tm, :] = a
        scr[6:8, :] = jnp.where(first, prev_ref[:, cols], tail_scr[:, cols])
        last2 = scr[pl.ds(tm + 6, 2), :]
        tail_scr[:, cols] = last2
        conv_ref[:, cols] = last2
        b = _dot(h, win_ref[:, F + c0:F + c0 + fc])
        u = (cb_ref[:, cols] + scr[pl.ds(6, tm), :] * cw_ref[0:1, cols] + scr[pl.ds(7, tm), :] * cw_ref[1:2, cols]
             + a * cw_ref[2:3, cols])
        y = y + _dot((jax.nn.gelu(u) * b).astype(bf16), wout_ref[cols, :])
    xo = x_ref[...] + g2_ref[...] * y
    if final_norm:
        xo = xo * lax.rsqrt(jnp.mean(xo * xo, axis=-1, keepdims=True) + RMS_EPS) * nf_ref[...]
    o_ref[...] = xo


def ffn_seq(x, sc, sh, nw, w_in, prev, g2, conv_w, conv_b, w_out, normf_w, *, tm, final_norm):
    B, T, D = x.shape
    F = w_out.shape[0]
    assert T % tm == 0 and tm >= CONV_W - 1
    fc = F // 2 if F % (2 * LANE) == 0 else F
    full = lambda a: pl.BlockSpec(a.shape, lambda g, i: (0,) * a.ndim)
    conv_b = conv_b.reshape(1, F)
    normf_w = normf_w.reshape(1, D)
    nw = nw.reshape(1, D)
    rows = pl.BlockSpec((None, tm, D), lambda g, i: (g, i, 0))
    state = pl.BlockSpec((None, CONV_W - 1, F), lambda g, i: (g, 0, 0))
    return pl.pallas_call(
        functools.partial(_ffn_seq_body, final_norm=final_norm),
        grid=(B, T // tm),
        in_specs=[rows, _mod_spec(sc, tm, D), _mod_spec(sh, tm, D), full(nw), _resident(w_in), state,
                  full(conv_w), full(conv_b), _resident(w_out), _mod_spec(g2, tm, D), full(normf_w)],
        out_specs=[rows, state],
        out_shape=[jax.ShapeDtypeStruct((B, T, D), f32), jax.ShapeDtypeStruct((B, CONV_W - 1, F), f32)],
        scratch_shapes=[pltpu.VMEM((tm + 8, fc), f32), pltpu.VMEM((CONV_W - 1, F), f32)],
        compiler_params=_cparams(("parallel", "arbitrary")),
        name="ffn_seq",
    )(x, sc, sh, nw, w_in, prev, conv_w, conv_b, w_out, g2, normf_w)


def _ffn_out_rows_body(a_ref, am1_ref, am2_ref, b_ref, cw_ref, cb_ref, wout_ref, x_ref, g2_ref, nf_ref, o_ref,
                       *, final_norm):
    _ffn_tail(a_ref[...], am1_ref[...], am2_ref[...], b_ref, cw_ref, cb_ref, wout_ref, x_ref, g2_ref, nf_ref, o_ref,
              final_norm)


def ffn_out_rows(ab, am1, am2, x, g2, conv_w, conv_b, w_out, normf_w, *, tm, final_norm):
    G, T, D = x.shape
    F = w_out.shape[0]
    full = lambda a: pl.BlockSpec(a.shape, lambda g, i: (0,) * a.ndim)
    conv_b = conv_b.reshape(1, F)
    normf_w = normf_w.reshape(1, D)
    rowsF = lambda blk: pl.BlockSpec((None, tm, F), lambda g, i: (g, i, blk))
    return pl.pallas_call(
        functools.partial(_ffn_out_rows_body, final_norm=final_norm),
        grid=(G, T // tm),
        in_specs=[rowsF(0), rowsF(0), rowsF(0), rowsF(1),
                  full(conv_w), full(conv_b), full(w_out),
                  pl.BlockSpec((None, tm, D), lambda g, i: (g, i, 0)),
                  _mod_spec(g2, tm, D), full(normf_w)],
        out_specs=pl.BlockSpec((None, tm, D), lambda g, i: (g, i, 0)),
        out_shape=jax.ShapeDtypeStruct((G, T, D), f32),
        compiler_params=_cparams(("parallel", "parallel")),
        name="ffn_out_rows",
    )(ab, am1, am2, ab, conv_w, conv_b, w_out, x, g2, normf_w)


def _prep_w_in(w):
    D = w.shape[0]
    o = 0
    parts = {}
    for name, n in (("aq", A_Q), ("akv", A_KV), ("ag", A_G), ("rq", B_QK), ("rk", B_QK), ("rv", B_V), ("rg", B_V),
                    ("ga", D), ("gb", D)):
        parts[name] = w[:, o:o + n]
        o += n
    pad = jnp.zeros((D, N_TOK - OFF_AG - A_G), w.dtype)
    tok = jnp.concatenate([parts["rg"], parts["ga"], parts["gb"], parts["rv"], parts["aq"], parts["rq"], parts["rk"],
                           parts["akv"][:, :KV_ROWS], parts["ag"], pad], axis=1)
    return tok.astype(bf16), parts["akv"].T.astype(bf16)


def _prep_cmp(cmp_pos, cmp_w1, cmp_w2):
    G = NSA_KV_HEADS
    r = jnp.arange(LANE)
    perm = ((r % CPP)[:, None] * CMP_STRIDE + (r // CPP)[:, None] == r[None, :]).astype(bf16)
    pos = jnp.concatenate([cmp_pos] * G, axis=-1)
    w1 = cmp_w1.reshape(2, CMP_BLOCK, HD, CMP_HIDDEN)
    eye = jnp.eye(G, dtype=w1.dtype)
    w1bd = jnp.einsum("eldh,gk->elgdkh", w1, eye).reshape(2, CMP_BLOCK, G * HD, G * CMP_HIDDEN)
    w1cat = jnp.concatenate([w1bd[:, :CMP_STRIDE], w1bd[:, CMP_STRIDE:]], axis=-1)
    eye_e = jnp.eye(2, dtype=w1.dtype)
    w2bd = jnp.einsum("ehd,gk,ef->eghkfd", cmp_w2, eye, eye_e).reshape(2, G * CMP_HIDDEN, G * 2 * HD)
    return perm, pos, w1cat.astype(bf16), w2bd.astype(bf16)


def _kv_rows_to_out(rowsT, lead):
    t = rowsT.shape[-1]
    r = rowsT.reshape(*lead, 2, NSA_KV_HEADS, HD, t)
    n = len(lead)
    return r.transpose(*range(n), n + 3, n, n + 1, n + 2)


def _keep_last_lanes(a, n):
    t = a.shape[-1]
    if t >= n:
        return a[..., t - n:]
    return jnp.pad(a, ((0, 0),) * (a.ndim - 1) + ((n - t, 0),))


def kernel(x_prompt, x_sample, c_prompt, c_sample, cache_cmp_kv, cache_sel_kv, state_win_kv, state_ret, state_conv,
           page_table, norm1_w, ada_w, ada_b, w_in, cmp_pos, cmp_w1, cmp_w2, w_oa, ret_gn_w, w_ob, w_o, norm2_w,
           ffn_w_in, ffn_conv_w, ffn_conv_b, ffn_w_out, normf_w):
    B, T, D = x_prompt.shape
    Bd, Td, _ = x_sample.shape
    depth = w_in.shape[0]
    n_phys, page = cache_cmp_kv.shape[1], cache_cmp_kv.shape[2]
    past = page_table.shape[1] * page
    wlen = state_win_kv.shape[2]
    F = ffn_w_out.shape[1]
    Rs = Bd * Td
    tm_p = min(512, T)

    mod = ada_mod(jnp.concatenate([c_prompt, c_sample], axis=0), ada_w, ada_b)
    cmpT = cache_cmp_kv.transpose(0, 1, 3, 4, 5, 2).reshape(depth, n_phys, KV_ROWS, page)
    selT = cache_sel_kv.transpose(0, 1, 3, 4, 5, 2).reshape(depth, n_phys, KV_ROWS, page)
    winT = state_win_kv.transpose(0, 1, 3, 4, 5, 2).reshape(depth, Bd, KV_ROWS, wlen)

    w_in_all, w_oa_all, w_ob_all, w_o_all, ffn_in_all, ffn_out_all = (
        w.astype(bf16) for w in (w_in, w_oa, w_ob, w_o, ffn_w_in, ffn_w_out))
    ret_zero = jnp.zeros((1, B, RET_HEADS, RET_DK, RET_DV), f32)

    xp = x_prompt
    xs = x_sample.reshape(1, Rs, D)
    outs = [[] for _ in range(10)]
    for l in range(depth):
        w_tok, w_kvT = _prep_w_in(w_in_all[l])
        cw = _prep_cmp(cmp_pos[l], cmp_w1[l], cmp_w2[l])
        w_oa_b, w_ob_b, w_o_b = w_oa_all[l], w_ob_all[l], w_o_all[l]
        ffn_in_b, ffn_out_b = ffn_in_all[l], ffn_out_all[l]
        last = l == depth - 1
        mp = [mod[l, :B, k * D:(k + 1) * D].reshape(B, 1, D) for k in range(6)]
        ms = [jnp.repeat(mod[l, B:, k * D:(k + 1) * D], Td, axis=0).reshape(1, Rs, D) for k in range(6)]

        proj, kvT = norm_proj(xp, mp[1], mp[0], norm1_w[l], w_tok, w_kvT, tm=tm_p, tn=2048)
        comp = compress_prompt(kvT, cw)
        attn = nsa_prompt(proj, kvT, comp)
        ro, ret_p = retention(proj, 0, ret_zero, 0)
        x1 = mix_out(proj, attn, ro, xp, mp[2], w_oa_b, ret_gn_w[l], w_ob_b, w_o_b, tm=tm_p)
        xp, conv_p = ffn_seq(x1, mp[4], mp[3], norm2_w[l], ffn_in_b, jnp.zeros((B, CONV_W - 1, F), f32), mp[5],
                             ffn_conv_w[l], ffn_conv_b[l], ffn_out_b, normf_w, tm=min(256, T), final_norm=last)
        outs[0].append(_kv_rows_to_out(kvT[:, 0:KV_ROWS], (B,)))
        outs[2].append(_kv_rows_to_out(kvT[:, KV_ROWS:2 * KV_ROWS], (B,)))
        outs[4].append(_kv_rows_to_out(_keep_last_lanes(kvT[:, 2 * KV_ROWS:], wlen), (B,)))
        outs[6].append(ret_p)
        outs[8].append(conv_p)

        proj_s, kvT_s = norm_proj(xs, ms[1], ms[0], norm1_w[l], w_tok, w_kvT, tm=Rs, tn=2048)
        comp_s = compress_paged(cmpT, l, page_table, cw)
        q_s = proj_s[0, :, OFF_AQ:OFF_AQ + A_Q].reshape(Bd, Td, NSA_KV_HEADS, HPG, HD)
        q_s = q_s.transpose(0, 2, 3, 1, 4).reshape(Bd, NSA_HEADS * Td, HD)
        ag_s = proj_s[0, :, OFF_AG:OFF_AG + A_G].reshape(Bd, Td, NSA_KV_HEADS, HPG, 3)
        ag_s = ag_s.transpose(0, 2, 3, 1, 4).reshape(Bd, NSA_HEADS * Td, 3)
        newT = kvT_s[0].reshape(A_KV, Bd, Td).transpose(1, 0, 2)
        newT_pad = jnp.pad(newT, ((0, 0), (0, 0), (0, LANE - Td)))
        attn_s = nsa_sample(q_s, ag_s, comp_s, selT, l, page_table, winT, newT_pad, tn=Td, past=past)
        attn_s = attn_s.reshape(Bd, NSA_KV_HEADS, HPG, Td, HD).transpose(0, 3, 1, 2, 4).reshape(1, Rs, A_Q)
        ro_s, ret_s = retention(proj_s.reshape(Bd, Td, N_TOK), past, state_ret, l)
        x1s = mix_out(proj_s, attn_s, ro_s.reshape(1, Rs, B_V), xs, ms[2], w_oa_b, ret_gn_w[l], w_ob_b, w_o_b, tm=Rs)
        ab_s = norm_proj(x1s, ms[4], ms[3], norm2_w[l], ffn_in_b, None, tm=Rs, tn=F)
        a_ext = jnp.concatenate([state_conv[l], ab_s[0, :, :F].reshape(Bd, Td, F)], axis=1)
        am1 = a_ext[:, 1:1 + Td].reshape(1, Rs, F)
        am2 = a_ext[:, 0:Td].reshape(1, Rs, F)
        xs = ffn_out_rows(ab_s, am1, am2, x1s, ms[5], ffn_conv_w[l], ffn_conv_b[l], ffn_out_b, normf_w, tm=Rs,
                          final_norm=last)
        outs[1].append(_kv_rows_to_out(newT[:, 0:KV_ROWS], (Bd,)))
        outs[3].append(_kv_rows_to_out(newT[:, KV_ROWS:2 * KV_ROWS], (Bd,)))
        win_all = jnp.concatenate([winT[l], newT[:, 2 * KV_ROWS:]], axis=-1)
        outs[5].append(_kv_rows_to_out(_keep_last_lanes(win_all, wlen), (Bd,)))
        outs[7].append(ret_s)
        outs[9].append(a_ext[:, Td:])

    st = lambda k: jnp.stack(outs[k])
    return (xp, xs.reshape(Bd, Td, D), st(0), st(1), st(2), st(3), st(4), st(5), st(6), st(7), st(8), st(9))
```

```python
import functools

import jax
import jax.numpy as jnp
from jax import lax
from jax.experimental import pallas as pl
from jax.experimental.pallas import tpu as pltpu

f32 = jnp.float32
bf16 = jnp.bfloat16

NSA_HEADS = 8
NSA_KV_HEADS = 2
HPG = NSA_HEADS // NSA_KV_HEADS
HD = 64
CMP_STRIDE = 16
CMP_BLOCK = 32
CMP_HIDDEN = 64
SEL_BLOCK = 64
SEL_SHIFT = 6
N_SEL = 16
WINDOW = 512
RET_HEADS = 4
RET_DK = 128
RET_DV = 256
RET_CHUNK = 128
ROPE_BASE = 10000.0
CONV_W = 3
RMS_EPS = 1e-6
GN_EPS = 1e-5
NEG = -1e30
FORCE_BONUS = 1e4

A_Q = NSA_HEADS * HD
KV_ROWS = 2 * NSA_KV_HEADS * HD
A_KV = 3 * KV_ROWS
A_G = 3 * NSA_HEADS
B_QK = RET_HEADS * RET_DK
B_V = RET_HEADS * RET_DV

OFF_RG, OFF_GA, OFF_GB, OFF_RV = 0, 1024, 2048, 3072
OFF_AQ, OFF_RQ, OFF_RK, OFF_AG = 4096, 4608, 5120, 5632
N_TOK = 5760
PROJ_TN = 1920

LANE = 128
VMEM_LIMIT = 56 * 1024 * 1024


def _cparams(sem):
    return pltpu.CompilerParams(dimension_semantics=sem, vmem_limit_bytes=VMEM_LIMIT)


def _dot(a, b):
    return jnp.dot(a, b, preferred_element_type=f32)


def _dot_nt(a, b):
    return lax.dot_general(a, b, (((1,), (1,)), ((), ())), preferred_element_type=f32)


def _dot_tn(a, b):
    return lax.dot_general(a, b, (((0,), (0,)), ((), ())), preferred_element_type=f32)


def _split_dot(a, b_bf16):
    hi = a.astype(bf16)
    lo = (a - hi.astype(f32)).astype(bf16)
    return _dot(hi, b_bf16) + _dot(lo, b_bf16)


def _ada_body(c_ref, w_ref, b_ref, o_ref):
    c = c_ref[...]
    s = c * jax.nn.sigmoid(c)
    o_ref[...] = _dot(s.astype(bf16), w_ref[...].astype(bf16)) + b_ref[...]


def ada_mod(c_all, ada_w, ada_b):
    depth, d, n = ada_w.shape
    r = c_all.shape[0]
    tn = 1024
    return pl.pallas_call(
        _ada_body,
        grid=(depth, n // tn),
        in_specs=[pl.BlockSpec((r, d), lambda l, j: (0, 0)),
                  pl.BlockSpec((None, d, tn), lambda l, j: (l, 0, j)),
                  pl.BlockSpec((None, 1, tn), lambda l, j: (l, 0, j))],
        out_specs=pl.BlockSpec((None, r, tn), lambda l, j: (l, 0, j)),
        out_shape=jax.ShapeDtypeStruct((depth, r, n), f32),
        compiler_params=_cparams(("parallel", "parallel")),
        name="ada_mod",
    )(c_all, ada_w, ada_b.reshape(depth, 1, n))


def _norm_proj_body(x_ref, sc_ref, sh_ref, nw_ref, w_ref, *rest, has_t):
    if has_t:
        wt_ref, o_ref, ot_ref, h_scr = rest
    else:
        o_ref, h_scr = rest

    j = pl.program_id(2)
    tn = o_ref.shape[1]

    @pl.when(j == 0)
    def _():
        h_scr[...] = _norm_mod(x_ref[...], nw_ref, sc_ref, sh_ref)
        if has_t:
            ot_ref[...] = _dot_nt(wt_ref[...], h_scr[...])

    o_ref[...] = _dot(h_scr[...], w_ref[:, pl.ds(pl.multiple_of(j * tn, tn), tn)])


def _norm_mod(x, nw_ref, sc_ref, sh_ref):
    y = x * lax.rsqrt(jnp.mean(x * x, axis=-1, keepdims=True) + RMS_EPS) * nw_ref[...]
    return (y * (1.0 + sc_ref[...]) + sh_ref[...]).astype(bf16)


def _resident(a):
    return pl.BlockSpec(a.shape, lambda *_: (0,) * a.ndim, pipeline_mode=pl.Buffered(1))


def norm_proj(x, sc, sh, nw, w, wt, *, tm, tn):
    G, T, D = x.shape
    N = w.shape[1]
    per_row = sc.shape[1] != 1
    mr = tm if per_row else 1
    mod_spec = pl.BlockSpec((None, mr, D), (lambda g, i, j: (g, i, 0)) if per_row else (lambda g, i, j: (g, 0, 0)))
    in_specs = [pl.BlockSpec((None, tm, D), lambda g, i, j: (g, i, 0)), mod_spec, mod_spec,
                pl.BlockSpec((1, D), lambda g, i, j: (0, 0)),
                _resident(w)]
    out_specs = [pl.BlockSpec((None, tm, tn), lambda g, i, j: (g, i, j))]
    out_shape = [jax.ShapeDtypeStruct((G, T, N), f32)]
    args = [x, sc, sh, nw.reshape(1, D), w]
    if wt is not None:
        NT = wt.shape[0]
        in_specs.append(_resident(wt))
        out_specs.append(pl.BlockSpec((None, NT, tm), lambda g, i, j: (g, 0, i)))
        out_shape.append(jax.ShapeDtypeStruct((G, NT, T), f32))
        args.append(wt)
    res = pl.pallas_call(
        functools.partial(_norm_proj_body, has_t=wt is not None),
        grid=(G, T // tm, N // tn),
        in_specs=in_specs, out_specs=out_specs, out_shape=out_shape,
        scratch_shapes=[pltpu.VMEM((tm, D), bf16)],
        compiler_params=_cparams(("parallel", "parallel", "arbitrary")),
        name="norm_proj_t" if wt is not None else "norm_proj",
    )(*args)
    return res if wt is not None else res[0]


def _norm_proj_layers_body(x_ref, sc_ref, sh_ref, nw_ref, w_ref, wt_ref, *rest):
    o_ref, cmp_ref, sel_ref, win_ref, h_scr = rest[-5:]
    j = pl.program_id(2)
    tn = o_ref.shape[1]

    @pl.when(j == 0)
    def _():
        h_scr[...] = _norm_mod(x_ref[...], nw_ref, sc_ref, sh_ref)
        for b, ref in enumerate((cmp_ref, sel_ref, win_ref)):
            kv = _dot_nt(wt_ref[b * KV_ROWS:(b + 1) * KV_ROWS, :], h_scr[...])
            if len(ref.shape) == 3:
                ref[0] = kv
                if ref.shape[0] > 1:
                    ref[1:] = jnp.zeros((ref.shape[0] - 1,) + kv.shape, f32)
            else:
                ref[...] = kv

    o_ref[...] = _dot(h_scr[...], w_ref[:, pl.ds(pl.multiple_of(j * tn, tn), tn)])


def norm_proj_layers(x, sc, sh, nw, w, wt, layer, depth, prev, *, tm, tn):
    G, T, D = x.shape
    N = w.shape[1]
    mod_spec = pl.BlockSpec((None, 1, D), lambda g, i, j: (g, 0, 0))
    in_specs = [pl.BlockSpec((None, tm, D), lambda g, i, j: (g, i, 0)), mod_spec, mod_spec,
                pl.BlockSpec((1, D), lambda g, i, j: (0, 0)), _resident(w), _resident(wt)]
    args = [x, sc, sh, nw.reshape(1, D), w, wt]
    aliases = {}
    if prev is not None:
        in_specs += [pl.BlockSpec(memory_space=pl.ANY)] * 2
        aliases = {len(args): 1, len(args) + 1: 2}
        args += list(prev)
    if prev is None:
        assert layer == 0
        layered = pl.BlockSpec((depth, None, KV_ROWS, tm), lambda g, i, j: (0, g, 0, i))
    else:
        layered = pl.BlockSpec((None, None, KV_ROWS, tm), lambda g, i, j: (layer, g, 0, i))
    return pl.pallas_call(
        _norm_proj_layers_body,
        grid=(G, T // tm, N // tn),
        in_specs=in_specs,
        out_specs=[pl.BlockSpec((None, tm, tn), lambda g, i, j: (g, i, j)), layered, layered,
                   pl.BlockSpec((None, KV_ROWS, tm), lambda g, i, j: (g, 0, i))],
        out_shape=[jax.ShapeDtypeStruct((G, T, N), f32), jax.ShapeDtypeStruct((depth, G, KV_ROWS, T), f32),
                   jax.ShapeDtypeStruct((depth, G, KV_ROWS, T), f32), jax.ShapeDtypeStruct((G, KV_ROWS, T), f32)],
        input_output_aliases=aliases,
        scratch_shapes=[pltpu.VMEM((tm, D), bf16)],
        compiler_params=_cparams(("parallel", "parallel", "arbitrary")),
        name="norm_proj_layers",
    )(*args)


CPP = LANE // CMP_STRIDE
assert CMP_BLOCK == 2 * CMP_STRIDE


def _compress_pages(get_page, n_pages, perm_ref, pos_ref, w1_ref, w2_ref, o_ref, xs_ref, acc_ref):
    n = o_ref.shape[0]
    perm = perm_ref[...]
    for k in range(n_pages):
        y = _dot_nt(perm, get_page(k).astype(bf16))
        for e in range(2):
            for s in range(CMP_STRIDE):
                xs_ref[e, s, k * CPP:(k + 1) * CPP, :] = y[s * CPP:(s + 1) * CPP, e * LANE:(e + 1) * LANE]
    for e in range(2):
        for s in range(CMP_STRIDE):
            xs_ref[e, s, n:n + 8, :] = jnp.zeros((8, LANE), f32)
            xs_ref[e, s, n + 1:n + 2, :] = pos_ref[e, s:s + 1, :]
            xs_ref[e, s, n + 2:n + 3, :] = pos_ref[e, CMP_STRIDE + s:CMP_STRIDE + s + 1, :]
    out = jnp.zeros((n, 2 * LANE), f32)
    for e in range(2):
        acc = jnp.zeros((n + 8, 2 * LANE), f32)
        for s in range(CMP_STRIDE):
            acc = acc + _dot(xs_ref[e, s].astype(bf16), w1_ref[e, s])
        acc_ref[...] = acc
        bias = acc_ref[n + 1:n + 2, 0:LANE] + acc_ref[n + 2:n + 3, LANE:2 * LANE]
        hid = acc_ref[0:n, 0:LANE] + acc_ref[pl.ds(1, n), LANE:2 * LANE] + bias
        out = out + _dot(jax.nn.gelu(hid).astype(bf16), w2_ref[e])
    o_ref[...] = out


def _cmp_weight_specs():
    z = lambda nd: (lambda *a: (0,) * nd)
    return [pl.BlockSpec((LANE, LANE), z(2)),
            pl.BlockSpec((2, CMP_BLOCK, LANE), z(3)),
            pl.BlockSpec((2, CMP_STRIDE, LANE, 2 * LANE), z(4)),
            pl.BlockSpec((2, LANE, 2 * LANE), z(3))]


def _cmp_scratch(n):
    return [pltpu.VMEM((2, CMP_STRIDE, n + 8, LANE), f32), pltpu.VMEM((n + 8, 2 * LANE), f32)]


def _compress_prompt_body(kvT_ref, perm_ref, pos_ref, w1_ref, w2_ref, o_ref, xs_ref, acc_ref):
    n_pages = kvT_ref.shape[1] // LANE
    get = lambda k: kvT_ref[:, k * LANE:(k + 1) * LANE]
    _compress_pages(get, n_pages, perm_ref, pos_ref, w1_ref, w2_ref, o_ref, xs_ref, acc_ref)


def compress_prompt(kvT, layer, cw):
    _, B, _, T = kvT.shape
    assert T % LANE == 0
    n = T // CMP_STRIDE
    return pl.pallas_call(
        _compress_prompt_body,
        grid=(B,),
        in_specs=[pl.BlockSpec((None, None, KV_ROWS, T), lambda b: (layer, b, 0, 0))] + _cmp_weight_specs(),
        out_specs=pl.BlockSpec((None, n, 2 * LANE), lambda b: (b, 0, 0)),
        out_shape=jax.ShapeDtypeStruct((B, n, 2 * LANE), f32),
        scratch_shapes=_cmp_scratch(n),
        compiler_params=_cparams(("parallel",)),
        name="compress_prompt",
    )(kvT, *cw)


def _page_specs(layer, n_pages, page):
    return [pl.BlockSpec((None, None, KV_ROWS, page), lambda b, pt, k=k: (layer, pt[b, k], 0, 0))
            for k in range(n_pages)]


def _compress_paged_body(pt_ref, perm_ref, pos_ref, w1_ref, w2_ref, *rest, n_pages):
    pages, (o_ref, xs_ref, acc_ref) = rest[:n_pages], rest[n_pages:]
    get = lambda k: pages[k][...]
    _compress_pages(get, n_pages, perm_ref, pos_ref, w1_ref, w2_ref, o_ref, xs_ref, acc_ref)


def compress_paged(cacheT, layer, page_table, cw):
    Bd, n_pages = page_table.shape
    page = cacheT.shape[-1]
    assert page == LANE
    L = n_pages * page
    n = L // CMP_STRIDE
    return pl.pallas_call(
        functools.partial(_compress_paged_body, n_pages=n_pages),
        grid_spec=pltpu.PrefetchScalarGridSpec(
            num_scalar_prefetch=1,
            grid=(Bd,),
            in_specs=_cmp_weight_specs() + _page_specs(layer, n_pages, page),
            out_specs=pl.BlockSpec((None, n, 2 * LANE), lambda b, pt: (b, 0, 0)),
            scratch_shapes=_cmp_scratch(n),
        ),
        out_shape=jax.ShapeDtypeStruct((Bd, n, 2 * LANE), f32),
        compiler_params=_cparams(("parallel",)),
        name="compress_paged",
    )(page_table, *cw, *([cacheT] * n_pages))


def _select_blocks(imp, s_ok, forced, n_sel, jj):
    k = min(N_SEL, n_sel)
    score = jnp.where(s_ok, imp + jnp.where(forced, FORCE_BONUS, 0.0), NEG)
    rank = jnp.zeros(score.shape, f32)
    for j2 in range(n_sel):
        col = score[:, j2:j2 + 1]
        ge = jnp.where(col >= score, 1.0, 0.0)
        gt = jnp.where(col > score, 1.0, 0.0)
        rank = rank + jnp.where(jj > j2, ge, gt)
    return jnp.where((rank < k) & s_ok, 1.0, 0.0)


def _select_blocks_t(imp, t0, n_sel):
    tq, sw = imp.shape
    rows = -(-n_sel // 8) * 8
    k = min(N_SEL, n_sel)
    jj = lax.broadcasted_iota(jnp.int32, (rows, tq), 0)
    qpos = t0 + lax.broadcasted_iota(jnp.int32, (rows, tq), 1)
    cur = jnp.right_shift(qpos, SEL_SHIFT)
    s_ok = (jj * SEL_BLOCK <= qpos) & (jj < n_sel)
    forced = (jj == 0) | (jj == cur) | (jj == cur - 1)
    imp_t = jnp.concatenate([imp[c * LANE:(c + 1) * LANE].T for c in range(tq // LANE)], axis=1)[:rows]
    score = jnp.where(s_ok, imp_t + jnp.where(forced, FORCE_BONUS, 0.0), NEG)
    rank = jnp.zeros(score.shape, f32)
    for j2 in range(n_sel):
        row = score[j2:j2 + 1, :]
        ge = jnp.where(row >= score, 1.0, 0.0)
        gt = jnp.where(row > score, 1.0, 0.0)
        rank = rank + jnp.where(jj > j2, ge, gt)
    sel_t = jnp.where((rank < k) & s_ok, 1.0, 0.0)
    sel_t = jnp.concatenate([sel_t, jnp.zeros((sw - rows, tq), f32)], axis=0)
    return jnp.concatenate([sel_t[:, c * LANE:(c + 1) * LANE].T for c in range(tq // LANE)], axis=0)


def _softmax_rows(sm, ok):
    m = jnp.max(sm, axis=-1, keepdims=True)
    e = jnp.where(ok, jnp.exp(sm - m), 0.0)
    den = jnp.sum(e, axis=-1, keepdims=True)
    return e / jnp.maximum(den, 1e-30)


def _pv_and_rowsum(e, vT):
    v_aug = jnp.concatenate([vT, jnp.ones(vT.shape, vT.dtype)], axis=0)
    return _dot_nt(e.astype(bf16), v_aug)


def _normalise(acc):
    return (acc / pltpu.roll(acc, HD, axis=1))[:, :HD]


def _nsa_prompt_body(q_ref, ag_ref, comp_ref, ksT_ref, vsT_ref, kwT_ref, vwT_ref, ov_ref, ex_ref, o_ref,
                     *, tq, n_cmp, n_sel, wk, ck):
    i = pl.program_id(1)
    t0 = i * tq
    scale = HD ** -0.5
    NC = comp_ref.shape[0]
    SW = ov_ref.shape[1]
    qpos = t0 + lax.broadcasted_iota(jnp.int32, (tq, 1), 0)
    col_n = lax.broadcasted_iota(jnp.int32, (tq, NC), 1)
    c_ok = ((col_n * CMP_STRIDE + (CMP_BLOCK - 1)) <= qpos) & (col_n < n_cmp)
    jj = lax.broadcasted_iota(jnp.int32, (tq, SW), 1)
    s_ok = (jj * SEL_BLOCK <= qpos) & (jj < n_sel)
    wstart = pl.multiple_of(jnp.maximum(t0 + tq - wk, 0), LANE)
    kposw = wstart + lax.broadcasted_iota(jnp.int32, (tq, wk), 1)
    rel = qpos - kposw
    w_bias4 = jnp.concatenate([jnp.where((rel >= 0) & (rel < WINDOW), 0.0, NEG)] * HPG, axis=0)
    c_ok4 = jnp.concatenate([c_ok] * HPG, axis=0)
    sg = jax.nn.sigmoid(ag_ref[...])
    n_chunks = (t0 + tq + ck - 1) // ck
    kcol = lax.broadcasted_iota(jnp.int32, (tq, ck), 1)
    all_valid_selected = t0 + tq <= min(N_SEL, n_sel) * SEL_BLOCK

    GS = range(NSA_KV_HEADS)
    rows = lambda g: slice(g * HD, (g + 1) * HD)
    kc = [comp_ref[:, g * LANE:g * LANE + HD].astype(bf16) for g in GS]
    vc = [comp_ref[:, g * LANE + HD:(g + 1) * LANE].astype(bf16) for g in GS]
    q4 = [jnp.concatenate([(q_ref[:, (g * HPG + h) * HD:(g * HPG + h + 1) * HD] * scale).astype(bf16)
                           for h in range(HPG)], axis=0) for g in GS]
    kw = [kwT_ref[rows(g), pl.ds(wstart, wk)].astype(bf16) for g in GS]
    vw = [vwT_ref[rows(g), pl.ds(wstart, wk)].astype(bf16) for g in GS]
    s_c = [_dot_nt(q4[g], kc[g]) for g in GS]
    s_w = [_dot(q4[g], kw[g]) + w_bias4 for g in GS]
    p_c = [_softmax_rows(jnp.where(c_ok4, s_c[g], NEG), c_ok4) for g in GS]
    e_w = [jnp.exp(s_w[g] - jnp.max(s_w[g], axis=-1, keepdims=True)) for g in GS]
    o_c = [_dot(p_c[g].astype(bf16), vc[g]) for g in GS]
    o_w = [_normalise(_pv_and_rowsum(e_w[g], vw[g])) for g in GS]
    imp = []
    for g in GS:
        psum = p_c[g][0:tq]
        for h in range(1, HPG):
            psum = psum + p_c[g][h * tq:(h + 1) * tq]
        imp.append(_split_dot(psum, ov_ref[...]))
    sel = lax.cond(all_valid_selected,
                   lambda: tuple(jnp.where(s_ok, 1.0, 0.0) for _ in GS),
                   lambda: tuple(_select_blocks_t(imp[g], t0, n_sel) for g in GS))
    selb = [s.astype(bf16) for s in sel]

    def chunk_step(c, carry):
        k0 = pl.multiple_of(c * ck, ck)
        causal = k0 + kcol <= qpos
        ex = ex_ref[:, pl.ds(k0, ck)]
        kT = [ksT_ref[rows(g), pl.ds(k0, ck)].astype(bf16) for g in GS]
        vT = [vsT_ref[rows(g), pl.ds(k0, ck)].astype(bf16) for g in GS]
        bias = [jnp.where((_dot(selb[g], ex) > 0.5) & causal, 0.0, NEG) for g in GS]
        s = [_dot(q4[g], kT[g]) + jnp.concatenate([bias[g]] * HPG, axis=0) for g in GS]
        m_new = [jnp.maximum(carry[g][0], jnp.max(s[g], axis=-1, keepdims=True)) for g in GS]
        e = [jnp.exp(s[g] - m_new[g]) for g in GS]
        pv = [_pv_and_rowsum(e[g], vT[g]) for g in GS]
        return tuple((m_new[g], jnp.exp(carry[g][0] - m_new[g]) * carry[g][1] + pv[g]) for g in GS)

    init = tuple((jnp.full((HPG * tq, 1), NEG, f32), jnp.zeros((HPG * tq, 2 * HD), f32)) for _ in GS)
    state = lax.fori_loop(0, n_chunks, chunk_step, init)

    for g in GS:
        gate = lambda br: jnp.concatenate(
            [sg[:, 3 * (g * HPG + h) + br:3 * (g * HPG + h) + br + 1] for h in range(HPG)], axis=0)
        o = gate(0) * o_c[g] + gate(1) * _normalise(state[g][1]) + gate(2) * o_w[g]
        for h in range(HPG):
            o_ref[:, (g * HPG + h) * HD:(g * HPG + h + 1) * HD] = o[h * tq:(h + 1) * tq]


def _overlap_matrix(nc_rows, n_cmp, sw_cols, n_sel):
    cs = jnp.arange(nc_rows) * CMP_STRIDE
    ss = jnp.arange(sw_cols) * SEL_BLOCK
    ok = (cs[:, None] < ss[None, :] + SEL_BLOCK) & (cs[:, None] + CMP_BLOCK > ss[None, :])
    ok = ok & (jnp.arange(nc_rows)[:, None] < n_cmp) & (jnp.arange(sw_cols)[None, :] < n_sel)
    return ok.astype(bf16)


def nsa_prompt(proj, selT, layer, winT, comp):
    B, T, _ = proj.shape
    tq = min(256, T)
    n_cmp = T // CMP_STRIDE - (CMP_BLOCK // CMP_STRIDE) + 1
    n_sel = -(-T // SEL_BLOCK)
    NC = comp.shape[1]
    SW = LANE
    assert n_sel <= SW and T % tq == 0
    wk = min(WINDOW + tq, T)
    ck = min(512, T)
    assert T % ck == 0
    ov = _overlap_matrix(NC, n_cmp, SW, n_sel)
    ex = (jnp.arange(SW)[:, None] == (jnp.arange(T)[None, :] // SEL_BLOCK)).astype(bf16)
    sel_spec = lambda blk: pl.BlockSpec((None, None, LANE, T), lambda b, i: (layer, b, blk, 0))
    win_spec = lambda blk: pl.BlockSpec((None, LANE, T), lambda b, i: (b, blk, 0))
    return pl.pallas_call(
        functools.partial(_nsa_prompt_body, tq=tq, n_cmp=n_cmp, n_sel=n_sel, wk=wk, ck=ck),
        grid=(B, T // tq),
        in_specs=[pl.BlockSpec((None, tq, A_Q), lambda b, i: (b, i, OFF_AQ // A_Q)),
                  pl.BlockSpec((None, tq, LANE), lambda b, i: (b, i, OFF_AG // LANE)),
                  pl.BlockSpec((None, NC, 2 * LANE), lambda b, i: (b, 0, 0)),
                  sel_spec(0), sel_spec(1), win_spec(0), win_spec(1),
                  pl.BlockSpec((NC, SW), lambda b, i: (0, 0)),
                  pl.BlockSpec((SW, T), lambda b, i: (0, 0))],
        out_specs=pl.BlockSpec((None, tq, A_Q), lambda b, i: (b, i, 0)),
        out_shape=jax.ShapeDtypeStruct((B, T, A_Q), f32),
        compiler_params=_cparams(("parallel", "arbitrary")),
        name="nsa_prompt",
    )(proj, proj, comp, selT, selT, winT, winT, ov, ex)


def _nsa_sample_body(pt_ref, q_ref, ag_ref, comp_ref, kwT_ref, newT_ref, ov_ref, rsum_ref, rexp_ref, ex_ref, *rest,
                     tn, past, n_cmp, n_sel, n_pages):
    pages, o_ref = rest[:n_pages], rest[n_pages]
    scale = HD ** -0.5
    R = HPG * tn
    NC = comp_ref.shape[0]
    SW = ov_ref.shape[1]
    tok = lax.rem(lax.broadcasted_iota(jnp.int32, (R, 1), 0), tn)
    qpos = past + tok
    col_n = lax.broadcasted_iota(jnp.int32, (R, NC), 1)
    c_ok = ((col_n * CMP_STRIDE + (CMP_BLOCK - 1)) <= qpos) & (col_n < n_cmp)
    jj = lax.broadcasted_iota(jnp.int32, (tn, SW), 1)
    qpos_t = past + lax.broadcasted_iota(jnp.int32, (tn, 1), 0)
    cur = jnp.right_shift(qpos_t, SEL_SHIFT)
    s_ok = (jj * SEL_BLOCK <= qpos_t) & (jj < n_sel)
    forced = (jj == 0) | (jj == cur) | (jj == cur - 1)
    wlen = kwT_ref.shape[1]
    kposw = (past - wlen) + lax.broadcasted_iota(jnp.int32, (R, wlen), 1)
    relw = qpos - kposw
    w_ok = (relw >= 0) & (relw < WINDOW)
    nl = newT_ref.shape[1]
    u = lax.broadcasted_iota(jnp.int32, (R, nl), 1)
    reln = tok - u
    n_ok = (reln >= 0) & (reln < WINDOW) & (u < tn)
    nb = past // SEL_BLOCK
    sg = jax.nn.sigmoid(ag_ref[...])
    for g in range(NSA_KV_HEADS):
        q = (q_ref[g * R:(g + 1) * R, :] * scale).astype(bf16)
        kc = comp_ref[:, g * LANE:g * LANE + HD].astype(bf16)
        vc = comp_ref[:, g * LANE + HD:(g + 1) * LANE].astype(bf16)
        pc = _softmax_rows(jnp.where(c_ok, _dot_nt(q, kc), NEG), c_ok)
        o_c = _dot(pc.astype(bf16), vc)
        imp = _split_dot(_rsum_exact(rsum_ref, pc), ov_ref[...])
        sel = _select_blocks(imp, s_ok, forced, n_sel, jj)
        selr = _dot(rexp_ref[...], sel.astype(bf16))
        kwp = kwT_ref[g * HD:(g + 1) * HD, :].astype(bf16)
        vwp = kwT_ref[2 * HD + g * HD:2 * HD + (g + 1) * HD, :].astype(bf16)
        base = 2 * KV_ROWS
        kwn = newT_ref[base + g * HD:base + (g + 1) * HD, :].astype(bf16)
        vwn = newT_ref[base + 2 * HD + g * HD:base + 2 * HD + (g + 1) * HD, :].astype(bf16)
        s1 = jnp.where(w_ok, _dot(q, kwp), NEG)
        s2 = jnp.where(n_ok, _dot(q, kwn), NEG)
        m = jnp.maximum(jnp.max(s1, axis=-1, keepdims=True), jnp.max(s2, axis=-1, keepdims=True))
        e1 = jnp.exp(s1 - m)
        e2 = jnp.exp(s2 - m)
        den = jnp.sum(e1, axis=-1, keepdims=True) + jnp.sum(e2, axis=-1, keepdims=True)
        o_w = (_dot_nt(e1.astype(bf16), vwp) + _dot_nt(e2.astype(bf16), vwn)) / den
        base = KV_ROWS
        ksn = newT_ref[base + g * HD:base + (g + 1) * HD, :].astype(bf16)
        vsn = newT_ref[base + 2 * HD + g * HD:base + 2 * HD + (g + 1) * HD, :].astype(bf16)
        ksp = jnp.concatenate([pg[g * HD:(g + 1) * HD, :] for pg in pages], axis=1).astype(bf16)
        vsp = jnp.concatenate([pg[2 * HD + g * HD:2 * HD + (g + 1) * HD, :] for pg in pages], axis=1).astype(bf16)
        okp = _dot(selr.astype(bf16), ex_ref[...]) > 0.5
        okn = (u <= tok) & (u < tn) & (selr[:, nb:nb + 1] > 0.5)
        s1 = jnp.where(okp, _dot(q, ksp), NEG)
        s2 = jnp.where(okn, _dot(q, ksn), NEG)
        m = jnp.maximum(jnp.max(s1, axis=-1, keepdims=True), jnp.max(s2, axis=-1, keepdims=True))
        e1 = jnp.exp(s1 - m)
        e2 = jnp.exp(s2 - m)
        den = jnp.sum(e1, axis=-1, keepdims=True) + jnp.sum(e2, axis=-1, keepdims=True)
        o_s = (_dot_nt(e1.astype(bf16), vsp) + _dot_nt(e2.astype(bf16), vsn)) / den
        gg = sg[g * R:(g + 1) * R, :]
        o_ref[g * R:(g + 1) * R, :] = gg[:, 0:1] * o_c + gg[:, 1:2] * o_s + gg[:, 2:3] * o_w


def _rsum_exact(rsum_ref, pc):
    r = rsum_ref[...]
    hi = pc.astype(bf16)
    lo = (pc - hi.astype(f32)).astype(bf16)
    return _dot(r, hi) + _dot(r, lo)


def nsa_sample(q_s, ag_s, comp, selT, layer, page_table, winT, newT, *, tn, past):
    Bd, n_pages = page_table.shape
    page = selT.shape[-1]
    R = HPG * tn
    L = past + tn
    n_cmp = L // CMP_STRIDE - (CMP_BLOCK // CMP_STRIDE) + 1
    n_sel = -(-L // SEL_BLOCK)
    NC = comp.shape[1]
    SW = -(-n_sel // LANE) * LANE
    assert past % page == 0 and page % SEL_BLOCK == 0 and tn <= SEL_BLOCK and n_cmp <= NC
    wlen = winT.shape[-1]
    ov = _overlap_matrix(NC, n_cmp, SW, n_sel)
    rsum = (jnp.arange(tn)[:, None] == (jnp.arange(R)[None, :] % tn)).astype(bf16)
    ex = (jnp.arange(SW)[:, None] == (jnp.arange(past)[None, :] // SEL_BLOCK)).astype(bf16)
    const = lambda *shape: pl.BlockSpec(shape, lambda b, pt: (0,) * len(shape))
    return pl.pallas_call(
        functools.partial(_nsa_sample_body, tn=tn, past=past, n_cmp=n_cmp, n_sel=n_sel, n_pages=n_pages),
        grid_spec=pltpu.PrefetchScalarGridSpec(
            num_scalar_prefetch=1,
            grid=(Bd,),
            in_specs=[pl.BlockSpec((None, 2 * R, HD), lambda b, pt: (b, 0, 0)),
                      pl.BlockSpec((None, 2 * R, 3), lambda b, pt: (b, 0, 0)),
                      pl.BlockSpec((None, NC, 2 * LANE), lambda b, pt: (b, 0, 0)),
                      pl.BlockSpec((None, None, KV_ROWS, wlen), lambda b, pt: (layer, b, 0, 0)),
                      pl.BlockSpec((None, A_KV, LANE), lambda b, pt: (b, 0, 0)),
                      const(NC, SW), const(tn, R), const(R, tn), const(SW, past)]
            + _page_specs(layer, n_pages, page),
            out_specs=pl.BlockSpec((None, 2 * R, HD), lambda b, pt: (b, 0, 0)),
        ),
        out_shape=jax.ShapeDtypeStruct((Bd, 2 * R, HD), f32),
        compiler_params=_cparams(("parallel",)),
        name="nsa_sample",
    )(page_table, q_s, ag_s, comp, winT, newT, ov, rsum, rsum.T, ex, *([selT] * n_pages))


def _retention_body(q_ref, k_ref, v_ref, cos_ref, sin_ref, dm_ref, qd_ref, kd_ref, sd_ref, s0_ref, *rest):
    o_ref, sn_ref, s_scr = rest[-3:]
    c = pl.program_id(1)

    @pl.when(c == 0)
    def _():
        s_scr[...] = s0_ref[...]

    C = dm_ref.shape[1]
    for cc in range(q_ref.shape[0] // C):
        r = slice(cc * C, (cc + 1) * C)
        cos = cos_ref[r, :]
        sin = sin_ref[r, :]

        def rot(x):
            return x * cos + pltpu.roll(x, RET_DK // 2, axis=1) * sin

        for h in range(RET_HEADS):
            q = rot(q_ref[r, h * RET_DK:(h + 1) * RET_DK])
            k = rot(k_ref[r, h * RET_DK:(h + 1) * RET_DK]) * (RET_DK ** -0.5)
            v = v_ref[r, h * RET_DV:(h + 1) * RET_DV].astype(bf16)
            S = s_scr[h]
            inner = _dot_nt(q.astype(bf16), k.astype(bf16)) * dm_ref[h]
            o_ref[r, h * RET_DV:(h + 1) * RET_DV] = (_dot(inner.astype(bf16), v)
                                                     + _dot((q * qd_ref[h]).astype(bf16), S.astype(bf16)))
            s_scr[h] = S * sd_ref[h] + _dot_tn((k * kd_ref[h]).astype(bf16), v)

    @pl.when(c == pl.num_programs(1) - 1)
    def _():
        if len(sn_ref.shape) == 4:
            sn_ref[0] = s_scr[...]
            if sn_ref.shape[0] > 1:
                sn_ref[1:] = jnp.zeros((sn_ref.shape[0] - 1,) + s_scr.shape, f32)
        else:
            sn_ref[...] = s_scr[...]


def retention(proj, pos0, states, layer, out_layer, depth, prev_new):
    B, T, _ = proj.shape
    C = RET_CHUNK if T % RET_CHUNK == 0 else T
    n = T // C
    H = RET_HEADS
    half = RET_DK // 2
    inv = jnp.exp(-jnp.log(ROPE_BASE) * jnp.arange(half, dtype=f32) / half)
    ang = (pos0 + jnp.arange(T)).astype(f32)[:, None] * inv[None, :]
    cos = jnp.concatenate([jnp.cos(ang), jnp.cos(ang)], axis=-1)
    sin = jnp.concatenate([-jnp.sin(ang), jnp.sin(ang)], axis=-1)
    log_g = jnp.log(1.0 - jnp.exp2(-5.0 - jnp.arange(H, dtype=f32)))
    i = jnp.arange(C, dtype=f32)
    diff = i[:, None] - i[None, :]
    dm = jnp.where(diff >= 0, jnp.exp(jnp.maximum(diff, 0.0)[None] * log_g[:, None, None]), 0.0)
    qd = jnp.exp((i + 1.0)[None, :] * log_g[:, None])[..., None]
    kd = jnp.exp((C - 1.0 - i)[None, :] * log_g[:, None])[..., None]
    sd = jnp.exp(C * log_g)[:, None, None]
    tab = lambda a: pl.BlockSpec(a.shape, lambda b, c: (0, 0, 0))
    cps = 4 if n % 4 == 0 else 1
    R = cps * C
    in_specs = [pl.BlockSpec((None, R, B_QK), lambda b, c: (b, c, OFF_RQ // B_QK)),
                pl.BlockSpec((None, R, B_QK), lambda b, c: (b, c, OFF_RK // B_QK)),
                pl.BlockSpec((None, R, B_V), lambda b, c: (b, c, OFF_RV // B_V)),
                pl.BlockSpec((R, RET_DK), lambda b, c: (c, 0)),
                pl.BlockSpec((R, RET_DK), lambda b, c: (c, 0)),
                tab(dm), tab(qd), tab(kd), tab(sd),
                pl.BlockSpec((None, None, H, RET_DK, RET_DV), lambda b, c: (layer, b, 0, 0, 0))]
    args = [proj, proj, proj, cos, sin, dm, qd, kd, sd, states]
    if prev_new is None:
        assert out_layer == 0
        st = pl.BlockSpec((depth, None, H, RET_DK, RET_DV), lambda b, c: (0, b, 0, 0, 0))
        aliases = {}
    else:
        st = pl.BlockSpec((None, None, H, RET_DK, RET_DV), lambda b, c: (out_layer, b, 0, 0, 0))
        in_specs.append(pl.BlockSpec(memory_space=pl.ANY))
        aliases = {len(args): 1}
        args.append(prev_new)
    return pl.pallas_call(
        _retention_body,
        grid=(B, n // cps),
        in_specs=in_specs,
        out_specs=[pl.BlockSpec((None, R, B_V), lambda b, c: (b, c, 0)), st],
        out_shape=[jax.ShapeDtypeStruct((B, T, B_V), f32),
                   jax.ShapeDtypeStruct((depth, B, H, RET_DK, RET_DV), f32)],
        input_output_aliases=aliases,
        scratch_shapes=[pltpu.VMEM((H, RET_DK, RET_DV), f32)],
        compiler_params=_cparams(("parallel", "arbitrary")),
        name="retention",
    )(*args)


def _mix_out_body(rg_ref, ga_ref, gb_ref, attn_ref, ro_ref, x_ref, g1_ref, woa_ref, gnw_ref, wob_ref, wo_ref, o_ref):
    ya = _dot(attn_ref[...].astype(bf16), woa_ref[...])
    parts = []
    for h in range(RET_HEADS):
        r = ro_ref[:, h * RET_DV:(h + 1) * RET_DV]
        d = r - jnp.mean(r, axis=-1, keepdims=True)
        var = jnp.mean(d * d, axis=-1, keepdims=True)
        parts.append(d * lax.rsqrt(var + GN_EPS) * gnw_ref[:, h * RET_DV:(h + 1) * RET_DV])
    ron = jnp.concatenate(parts, axis=-1)
    rg = rg_ref[...]
    yb = _dot((rg * jax.nn.sigmoid(rg) * ron).astype(bf16), wob_ref[...])
    m = jax.nn.sigmoid(ga_ref[...]) * ya + jax.nn.sigmoid(gb_ref[...]) * yb
    o_ref[...] = x_ref[...] + g1_ref[...] * _dot(m.astype(bf16), wo_ref[...])


def _mod_spec(mod, tm, D):
    if mod.shape[1] != 1:
        return pl.BlockSpec((None, tm, D), lambda g, i: (g, i, 0))
    return pl.BlockSpec((None, 1, D), lambda g, i: (g, 0, 0))


def mix_out(proj, attn, ro, x, g1, w_oa, gn_w, w_ob, w_o, *, tm):
    G, T, D = x.shape
    full = lambda a: pl.BlockSpec(a.shape, lambda g, i: (0,) * a.ndim)
    wide = lambda blk: pl.BlockSpec((None, tm, B_V), lambda g, i: (g, i, blk))
    gn_w = gn_w.reshape(1, B_V)
    return pl.pallas_call(
        _mix_out_body,
        grid=(G, T // tm),
        in_specs=[wide(OFF_RG // B_V), wide(OFF_GA // B_V), wide(OFF_GB // B_V),
                  pl.BlockSpec((None, tm, A_Q), lambda g, i: (g, i, 0)),
                  wide(0),
                  pl.BlockSpec((None, tm, D), lambda g, i: (g, i, 0)),
                  _mod_spec(g1, tm, D), full(w_oa), full(gn_w), full(w_ob), full(w_o)],
        out_specs=pl.BlockSpec((None, tm, D), lambda g, i: (g, i, 0)),
        out_shape=jax.ShapeDtypeStruct((G, T, D), f32),
        compiler_params=_cparams(("parallel", "parallel")),
        name="mix_out",
    )(proj, proj, proj, attn, ro, x, g1, w_oa, gn_w, w_ob, w_o)


def _ffn_tail(a, am1, am2, b_ref, cw_ref, cb_ref, wout_ref, x_ref, g2_ref, nf_ref, o_ref, final_norm):
    u = cb_ref[...] + am2 * cw_ref[0:1, :] + am1 * cw_ref[1:2, :] + a * cw_ref[2:3, :]
    y = _dot((jax.nn.gelu(u) * b_ref[...]).astype(bf16), wout_ref[...])
    xo = x_ref[...] + g2_ref[...] * y
    if final_norm:
        xo = xo * lax.rsqrt(jnp.mean(xo * xo, axis=-1, keepdims=True) + RMS_EPS) * nf_ref[...]
    o_ref[...] = xo


def _ffn_seq_body(x_ref, sc_ref, sh_ref, nw_ref, win_ref, prev_ref, cw_ref, cb_ref, wout_ref, g2_ref, nf_ref,
                  o_ref, conv_ref, scr, tail_scr, *, final_norm):
    tm = x_ref.shape[0]
    F = wout_ref.shape[0]
    fc = scr.shape[1]
    h = _norm_mod(x_ref[...], nw_ref, sc_ref, sh_ref)
    first = pl.program_id(1) == 0
    y = jnp.zeros(o_ref.shape, f32)
    for c0 in range(0, F, fc):
        cols = slice(c0, c0 + fc)
        a = _dot(h, win_ref[:, cols])
        scr[8:8 + tm, :] = a
        scr[6:8, :] = jnp.where(first, prev_ref[:, cols], tail_scr[:, cols])
        last2 = scr[pl.ds(tm + 6, 2), :]
        tail_scr[:, cols] = last2
        conv_ref[:, cols] = last2
        b = _dot(h, win_ref[:, F + c0:F + c0 + fc])
        u = (cb_ref[:, cols] + scr[pl.ds(6, tm), :] * cw_ref[0:1, cols] + scr[pl.ds(7, tm), :] * cw_ref[1:2, cols]
             + a * cw_ref[2:3, cols])
        y = y + _dot((jax.nn.gelu(u) * b).astype(bf16), wout_ref[cols, :])
    xo = x_ref[...] + g2_ref[...] * y
    if final_norm:
        xo = xo * lax.rsqrt(jnp.mean(xo * xo, axis=-1, keepdims=True) + RMS_EPS) * nf_ref[...]
    o_ref[...] = xo


def ffn_seq(x, sc, sh, nw, w_in, prev, g2, conv_w, conv_b, w_out, normf_w, *, tm, final_norm):
    B, T, D = x.shape
    F = w_out.shape[0]
    assert T % tm == 0 and tm >= CONV_W - 1
    fc = F
    full = lambda a: pl.BlockSpec(a.shape, lambda g, i: (0,) * a.ndim)
    conv_b = conv_b.reshape(1, F)
    normf_w = normf_w.reshape(1, D)
    nw = nw.reshape(1, D)
    rows = pl.BlockSpec((None, tm, D), lambda g, i: (g, i, 0))
    state = pl.BlockSpec((None, CONV_W - 1, F), lambda g, i: (g, 0, 0))
    return pl.pallas_call(
        functools.partial(_ffn_seq_body, final_norm=final_norm),
        grid=(B, T // tm),
        in_specs=[rows, _mod_spec(sc, tm, D), _mod_spec(sh, tm, D), full(nw), _resident(w_in), state,
                  full(conv_w), full(conv_b), _resident(w_out), _mod_spec(g2, tm, D), full(normf_w)],
        out_specs=[rows, state],
        out_shape=[jax.ShapeDtypeStruct((B, T, D), f32), jax.ShapeDtypeStruct((B, CONV_W - 1, F), f32)],
        scratch_shapes=[pltpu.VMEM((tm + 8, fc), f32), pltpu.VMEM((CONV_W - 1, F), f32)],
        compiler_params=_cparams(("parallel", "arbitrary")),
        name="ffn_seq",
    )(x, sc, sh, nw, w_in, prev, conv_w, conv_b, w_out, g2, normf_w)


def _ffn_out_rows_body(a_ref, am1_ref, am2_ref, b_ref, cw_ref, cb_ref, wout_ref, x_ref, g2_ref, nf_ref, o_ref,
                       *, final_norm):
    _ffn_tail(a_ref[...], am1_ref[...], am2_ref[...], b_ref, cw_ref, cb_ref, wout_ref, x_ref, g2_ref, nf_ref, o_ref,
              final_norm)


def ffn_out_rows(ab, am1, am2, x, g2, conv_w, conv_b, w_out, normf_w, *, tm, final_norm):
    G, T, D = x.shape
    F = w_out.shape[0]
    full = lambda a: pl.BlockSpec(a.shape, lambda g, i: (0,) * a.ndim)
    conv_b = conv_b.reshape(1, F)
    normf_w = normf_w.reshape(1, D)
    rowsF = lambda blk: pl.BlockSpec((None, tm, F), lambda g, i: (g, i, blk))
    return pl.pallas_call(
        functools.partial(_ffn_out_rows_body, final_norm=final_norm),
        grid=(G, T // tm),
        in_specs=[rowsF(0), rowsF(0), rowsF(0), rowsF(1),
                  full(conv_w), full(conv_b), full(w_out),
                  pl.BlockSpec((None, tm, D), lambda g, i: (g, i, 0)),
                  _mod_spec(g2, tm, D), full(normf_w)],
        out_specs=pl.BlockSpec((None, tm, D), lambda g, i: (g, i, 0)),
        out_shape=jax.ShapeDtypeStruct((G, T, D), f32),
        compiler_params=_cparams(("parallel", "parallel")),
        name="ffn_out_rows",
    )(ab, am1, am2, ab, conv_w, conv_b, w_out, x, g2, normf_w)


def _prep_w_in(w):
    D = w.shape[0]
    o = 0
    parts = {}
    for name, n in (("aq", A_Q), ("akv", A_KV), ("ag", A_G), ("rq", B_QK), ("rk", B_QK), ("rv", B_V), ("rg", B_V),
                    ("ga", D), ("gb", D)):
        parts[name] = w[:, o:o + n]
        o += n
    pad = jnp.zeros((D, N_TOK - OFF_AG - A_G), w.dtype)
    tok = jnp.concatenate([parts["rg"], parts["ga"], parts["gb"], parts["rv"], parts["aq"], parts["rq"], parts["rk"],
                           parts["ag"], pad], axis=1)
    return tok.astype(bf16), parts["akv"].T.astype(bf16)


def _prep_cmp(cmp_pos, cmp_w1, cmp_w2):
    G = NSA_KV_HEADS
    r = jnp.arange(LANE)
    perm = ((r % CPP)[:, None] * CMP_STRIDE + (r // CPP)[:, None] == r[None, :]).astype(bf16)
    pos = jnp.concatenate([cmp_pos] * G, axis=-1)
    w1 = cmp_w1.reshape(2, CMP_BLOCK, HD, CMP_HIDDEN)
    eye = jnp.eye(G, dtype=w1.dtype)
    w1bd = jnp.einsum("eldh,gk->elgdkh", w1, eye).reshape(2, CMP_BLOCK, G * HD, G * CMP_HIDDEN)
    w1cat = jnp.concatenate([w1bd[:, :CMP_STRIDE], w1bd[:, CMP_STRIDE:]], axis=-1)
    eye_e = jnp.eye(2, dtype=w1.dtype)
    w2bd = jnp.einsum("ehd,gk,ef->eghkfd", cmp_w2, eye, eye_e).reshape(2, G * CMP_HIDDEN, G * 2 * HD)
    return perm, pos, w1cat.astype(bf16), w2bd.astype(bf16)


def _kv_rows_to_out(rowsT, lead):
    t = rowsT.shape[-1]
    r = rowsT.reshape(*lead, 2, NSA_KV_HEADS, HD, t)
    n = len(lead)
    return r.transpose(*range(n), n + 3, n, n + 1, n + 2)


def _keep_last_lanes(a, n):
    t = a.shape[-1]
    if t >= n:
        return a[..., t - n:]
    return jnp.pad(a, ((0, 0),) * (a.ndim - 1) + ((n - t, 0),))


def kernel(x_prompt, x_sample, c_prompt, c_sample, cache_cmp_kv, cache_sel_kv, state_win_kv, state_ret, state_conv,
           page_table, norm1_w, ada_w, ada_b, w_in, cmp_pos, cmp_w1, cmp_w2, w_oa, ret_gn_w, w_ob, w_o, norm2_w,
           ffn_w_in, ffn_conv_w, ffn_conv_b, ffn_w_out, normf_w):
    B, T, D = x_prompt.shape
    Bd, Td, _ = x_sample.shape
    depth = w_in.shape[0]
    n_phys, page = cache_cmp_kv.shape[1], cache_cmp_kv.shape[2]
    past = page_table.shape[1] * page
    wlen = state_win_kv.shape[2]
    F = ffn_w_out.shape[1]
    Rs = Bd * Td
    tm_p = min(512, T)

    mod = ada_mod(jnp.concatenate([c_prompt, c_sample], axis=0), ada_w, ada_b)
    cmpT = cache_cmp_kv.transpose(0, 1, 3, 4, 5, 2).reshape(depth, n_phys, KV_ROWS, page)
    selT = cache_sel_kv.transpose(0, 1, 3, 4, 5, 2).reshape(depth, n_phys, KV_ROWS, page)
    winT = state_win_kv.transpose(0, 1, 3, 4, 5, 2).reshape(depth, Bd, KV_ROWS, wlen)

    w_in_all, w_oa_all, w_ob_all, w_o_all, ffn_in_all, ffn_out_all = (
        w.astype(bf16) for w in (w_in, w_oa, w_ob, w_o, ffn_w_in, ffn_w_out))
    ret_zero = jnp.zeros((1, B, RET_HEADS, RET_DK, RET_DV), f32)

    xp = x_prompt
    xs = x_sample.reshape(1, Rs, D)
    kv_p = None
    ret_p = ret_s = None
    outs = [[] for _ in range(10)]
    for l in range(depth):
        w_tok, w_kvT = _prep_w_in(w_in_all[l])
        cw = _prep_cmp(cmp_pos[l], cmp_w1[l], cmp_w2[l])
        w_oa_b, w_ob_b, w_o_b = w_oa_all[l], w_ob_all[l], w_o_all[l]
        ffn_in_b, ffn_out_b = ffn_in_all[l], ffn_out_all[l]
        last = l == depth - 1
        mp = [mod[l, :B, k * D:(k + 1) * D].reshape(B, 1, D) for k in range(6)]
        ms = [jnp.repeat(mod[l, B:, k * D:(k + 1) * D], Td, axis=0).reshape(1, Rs, D) for k in range(6)]

        proj, cmp_p, sel_p, win_p = norm_proj_layers(xp, mp[1], mp[0], norm1_w[l], w_tok, w_kvT, l, depth, kv_p,
                                                     tm=tm_p, tn=PROJ_TN)
        kv_p = (cmp_p, sel_p)
        comp = compress_prompt(cmp_p, l, cw)
        attn = nsa_prompt(proj, sel_p, l, win_p, comp)
        ro, ret_p = retention(proj, 0, ret_zero, 0, l, depth, ret_p)
        x1 = mix_out(proj, attn, ro, xp, mp[2], w_oa_b, ret_gn_w[l], w_ob_b, w_o_b, tm=tm_p)
        xp, conv_p = ffn_seq(x1, mp[4], mp[3], norm2_w[l], ffn_in_b, jnp.zeros((B, CONV_W - 1, F), f32), mp[5],
                             ffn_conv_w[l], ffn_conv_b[l], ffn_out_b, normf_w, tm=min(256, T), final_norm=last)
        outs[4].append(_kv_rows_to_out(_keep_last_lanes(win_p, wlen), (B,)))
        outs[8].append(conv_p)

        proj_s, kvT_s = norm_proj(xs, ms[1], ms[0], norm1_w[l], w_tok, w_kvT, tm=Rs, tn=PROJ_TN)
        comp_s = compress_paged(cmpT, l, page_table, cw)
        q_s = proj_s[0, :, OFF_AQ:OFF_AQ + A_Q].reshape(Bd, Td, NSA_KV_HEADS, HPG, HD)
        q_s = q_s.transpose(0, 2, 3, 1, 4).reshape(Bd, NSA_HEADS * Td, HD)
        ag_s = proj_s[0, :, OFF_AG:OFF_AG + A_G].reshape(Bd, Td, NSA_KV_HEADS, HPG, 3)
        ag_s = ag_s.transpose(0, 2, 3, 1, 4).reshape(Bd, NSA_HEADS * Td, 3)
        newT = kvT_s[0].reshape(A_KV, Bd, Td).transpose(1, 0, 2)
        newT_pad = jnp.pad(newT, ((0, 0), (0, 0), (0, LANE - Td)))
        attn_s = nsa_sample(q_s, ag_s, comp_s, selT, l, page_table, winT, newT_pad, tn=Td, past=past)
        attn_s = attn_s.reshape(Bd, NSA_KV_HEADS, HPG, Td, HD).transpose(0, 3, 1, 2, 4).reshape(1, Rs, A_Q)
        ro_s, ret_s = retention(proj_s.reshape(Bd, Td, N_TOK), past, state_ret, l, l, depth, ret_s)
        x1s = mix_out(proj_s, attn_s, ro_s.reshape(1, Rs, B_V), xs, ms[2], w_oa_b, ret_gn_w[l], w_ob_b, w_o_b, tm=Rs)
        ab_s = norm_proj(x1s, ms[4], ms[3], norm2_w[l], ffn_in_b, None, tm=Rs, tn=F)
        a_ext = jnp.concatenate([state_conv[l], ab_s[0, :, :F].reshape(Bd, Td, F)], axis=1)
        am1 = a_ext[:, 1:1 + Td].reshape(1, Rs, F)
        am2 = a_ext[:, 0:Td].reshape(1, Rs, F)
        xs = ffn_out_rows(ab_s, am1, am2, x1s, ms[5], ffn_conv_w[l], ffn_conv_b[l], ffn_out_b, normf_w, tm=Rs,
                          final_norm=last)
        outs[1].append(_kv_rows_to_out(newT[:, 0:KV_ROWS], (Bd,)))
        outs[3].append(_kv_rows_to_out(newT[:, KV_ROWS:2 * KV_ROWS], (Bd,)))
        win_all = jnp.concatenate([winT[l], newT[:, 2 * KV_ROWS:]], axis=-1)
        outs[5].append(_kv_rows_to_out(_keep_last_lanes(win_all, wlen), (Bd,)))
        outs[9].append(a_ext[:, Td:])

    st = lambda k: jnp.stack(outs[k])
    new_cmp_p = _kv_rows_to_out(kv_p[0], (depth, B))
    new_sel_p = _kv_rows_to_out(kv_p[1], (depth, B))
    return (xp, xs.reshape(Bd, Td, D), new_cmp_p, st(1), new_sel_p, st(3), st(4), st(5), ret_p, ret_s, st(8), st(9))
```

```python
import functools

import jax
import jax.numpy as jnp
from jax import lax
from jax.experimental import pallas as pl
from jax.experimental.pallas import tpu as pltpu

f32 = jnp.float32
bf16 = jnp.bfloat16

NSA_HEADS = 8
NSA_KV_HEADS = 2
HPG = NSA_HEADS // NSA_KV_HEADS
HD = 64
CMP_STRIDE = 16
CMP_BLOCK = 32
CMP_HIDDEN = 64
SEL_BLOCK = 64
SEL_SHIFT = 6
N_SEL = 16
WINDOW = 512
RET_HEADS = 4
RET_DK = 128
RET_DV = 256
RET_CHUNK = 128
ROPE_BASE = 10000.0
CONV_W = 3
RMS_EPS = 1e-6
GN_EPS = 1e-5
NEG = -1e30
FORCE_BONUS = 1e4

A_Q = NSA_HEADS * HD
KV_ROWS = 2 * NSA_KV_HEADS * HD
A_KV = 3 * KV_ROWS
A_G = 3 * NSA_HEADS
B_QK = RET_HEADS * RET_DK
B_V = RET_HEADS * RET_DV

OFF_RG, OFF_GA, OFF_GB, OFF_RV = 0, 1024, 2048, 3072
OFF_AQ, OFF_RQ, OFF_RK, OFF_AG = 4096, 4608, 5120, 5632
N_TOK = 5760
PROJ_TN = 1920

LANE = 128
VMEM_LIMIT = 56 * 1024 * 1024


def _cparams(sem):
    return pltpu.CompilerParams(dimension_semantics=sem, vmem_limit_bytes=VMEM_LIMIT)


def _dot(a, b):
    return jnp.dot(a, b, preferred_element_type=f32)


def _dot_nt(a, b):
    return lax.dot_general(a, b, (((1,), (1,)), ((), ())), preferred_element_type=f32)


def _dot_tn(a, b):
    return lax.dot_general(a, b, (((0,), (0,)), ((), ())), preferred_element_type=f32)


def _split_dot(a, b_bf16):
    hi = a.astype(bf16)
    lo = (a - hi.astype(f32)).astype(bf16)
    return _dot(hi, b_bf16) + _dot(lo, b_bf16)


def _ada_body(c_ref, w_ref, b_ref, o_ref):
    c = c_ref[...]
    s = c * jax.nn.sigmoid(c)
    o_ref[...] = _dot(s.astype(bf16), w_ref[...].astype(bf16)) + b_ref[...]


def ada_mod(c_all, ada_w, ada_b):
    depth, d, n = ada_w.shape
    r = c_all.shape[0]
    tn = 1024
    return pl.pallas_call(
        _ada_body,
        grid=(depth, n // tn),
        in_specs=[pl.BlockSpec((r, d), lambda l, j: (0, 0)),
                  pl.BlockSpec((None, d, tn), lambda l, j: (l, 0, j)),
                  pl.BlockSpec((None, 1, tn), lambda l, j: (l, 0, j))],
        out_specs=pl.BlockSpec((None, r, tn), lambda l, j: (l, 0, j)),
        out_shape=jax.ShapeDtypeStruct((depth, r, n), f32),
        compiler_params=_cparams(("parallel", "parallel")),
        name="ada_mod",
    )(c_all, ada_w, ada_b.reshape(depth, 1, n))


def _norm_proj_body(x_ref, sc_ref, sh_ref, nw_ref, w_ref, *rest, has_t):
    if has_t:
        wt_ref, o_ref, ot_ref, h_scr = rest
    else:
        o_ref, h_scr = rest

    j = pl.program_id(2)
    tn = o_ref.shape[1]

    @pl.when(j == 0)
    def _():
        h_scr[...] = _norm_mod(x_ref[...], nw_ref, sc_ref, sh_ref)
        if has_t:
            ot_ref[...] = _dot_nt(wt_ref[...], h_scr[...])

    o_ref[...] = _dot(h_scr[...], w_ref[:, pl.ds(pl.multiple_of(j * tn, tn), tn)])


def _norm_mod(x, nw_ref, sc_ref, sh_ref):
    y = x * lax.rsqrt(jnp.mean(x * x, axis=-1, keepdims=True) + RMS_EPS) * nw_ref[...]
    return (y * (1.0 + sc_ref[...]) + sh_ref[...]).astype(bf16)


def _resident(a):
    return pl.BlockSpec(a.shape, lambda *_: (0,) * a.ndim, pipeline_mode=pl.Buffered(1))


def norm_proj(x, sc, sh, nw, w, wt, *, tm, tn):
    G, T, D = x.shape
    N = w.shape[1]
    per_row = sc.shape[1] != 1
    mr = tm if per_row else 1
    mod_spec = pl.BlockSpec((None, mr, D), (lambda g, i, j: (g, i, 0)) if per_row else (lambda g, i, j: (g, 0, 0)))
    in_specs = [pl.BlockSpec((None, tm, D), lambda g, i, j: (g, i, 0)), mod_spec, mod_spec,
                pl.BlockSpec((1, D), lambda g, i, j: (0, 0)),
                _resident(w)]
    out_specs = [pl.BlockSpec((None, tm, tn), lambda g, i, j: (g, i, j))]
    out_shape = [jax.ShapeDtypeStruct((G, T, N), f32)]
    args = [x, sc, sh, nw.reshape(1, D), w]
    if wt is not None:
        NT = wt.shape[0]
        in_specs.append(_resident(wt))
        out_specs.append(pl.BlockSpec((None, NT, tm), lambda g, i, j: (g, 0, i)))
        out_shape.append(jax.ShapeDtypeStruct((G, NT, T), f32))
        args.append(wt)
    res = pl.pallas_call(
        functools.partial(_norm_proj_body, has_t=wt is not None),
        grid=(G, T // tm, N // tn),
        in_specs=in_specs, out_specs=out_specs, out_shape=out_shape,
        scratch_shapes=[pltpu.VMEM((tm, D), bf16)],
        compiler_params=_cparams(("parallel", "parallel", "arbitrary")),
        name="norm_proj_t" if wt is not None else "norm_proj",
    )(*args)
    return res if wt is not None else res[0]


def _norm_proj_layers_body(x_ref, sc_ref, sh_ref, nw_ref, w_ref, wt_ref, *rest):
    o_ref, cmp_ref, sel_ref, win_ref, h_scr = rest[-5:]
    j = pl.program_id(2)
    tn = o_ref.shape[1]

    @pl.when(j == 0)
    def _():
        h_scr[...] = _norm_mod(x_ref[...], nw_ref, sc_ref, sh_ref)
        for b, ref in enumerate((cmp_ref, sel_ref, win_ref)):
            kv = _dot_nt(wt_ref[b * KV_ROWS:(b + 1) * KV_ROWS, :], h_scr[...])
            if len(ref.shape) == 3:
                ref[0] = kv
                if ref.shape[0] > 1:
                    ref[1:] = jnp.zeros((ref.shape[0] - 1,) + kv.shape, f32)
            else:
                ref[...] = kv

    o_ref[...] = _dot(h_scr[...], w_ref[:, pl.ds(pl.multiple_of(j * tn, tn), tn)])


def norm_proj_layers(x, sc, sh, nw, w, wt, layer, depth, prev, *, tm, tn):
    G, T, D = x.shape
    N = w.shape[1]
    mod_spec = pl.BlockSpec((None, 1, D), lambda g, i, j: (g, 0, 0))
    in_specs = [pl.BlockSpec((None, tm, D), lambda g, i, j: (g, i, 0)), mod_spec, mod_spec,
                pl.BlockSpec((1, D), lambda g, i, j: (0, 0)), _resident(w), _resident(wt)]
    args = [x, sc, sh, nw.reshape(1, D), w, wt]
    aliases = {}
    if prev is not None:
        in_specs += [pl.BlockSpec(memory_space=pl.ANY)] * 2
        aliases = {len(args): 1, len(args) + 1: 2}
        args += list(prev)
    if prev is None:
        assert layer == 0
        layered = pl.BlockSpec((depth, None, KV_ROWS, tm), lambda g, i, j: (0, g, 0, i))
    else:
        layered = pl.BlockSpec((None, None, KV_ROWS, tm), lambda g, i, j: (layer, g, 0, i))
    return pl.pallas_call(
        _norm_proj_layers_body,
        grid=(G, T // tm, N // tn),
        in_specs=in_specs,
        out_specs=[pl.BlockSpec((None, tm, tn), lambda g, i, j: (g, i, j)), layered, layered,
                   pl.BlockSpec((None, KV_ROWS, tm), lambda g, i, j: (g, 0, i))],
        out_shape=[jax.ShapeDtypeStruct((G, T, N), f32), jax.ShapeDtypeStruct((depth, G, KV_ROWS, T), f32),
                   jax.ShapeDtypeStruct((depth, G, KV_ROWS, T), f32), jax.ShapeDtypeStruct((G, KV_ROWS, T), f32)],
        input_output_aliases=aliases,
        scratch_shapes=[pltpu.VMEM((tm, D), bf16)],
        compiler_params=_cparams(("parallel", "parallel", "arbitrary")),
        name="norm_proj_layers",
    )(*args)


CPP = LANE // CMP_STRIDE
assert CMP_BLOCK == 2 * CMP_STRIDE


def _compress_pages(get_page, n_pages, perm_ref, pos_ref, w1_ref, w2_ref, o_ref, xs_ref, acc_ref):
    n = o_ref.shape[0]
    perm = perm_ref[...]
    for k in range(n_pages):
        y = _dot_nt(perm, get_page(k).astype(bf16))
        for e in range(2):
            for s in range(CMP_STRIDE):
                xs_ref[e, s, k * CPP:(k + 1) * CPP, :] = y[s * CPP:(s + 1) * CPP, e * LANE:(e + 1) * LANE]
    for e in range(2):
        for s in range(CMP_STRIDE):
            xs_ref[e, s, n:n + 8, :] = jnp.zeros((8, LANE), f32)
            xs_ref[e, s, n + 1:n + 2, :] = pos_ref[e, s:s + 1, :]
            xs_ref[e, s, n + 2:n + 3, :] = pos_ref[e, CMP_STRIDE + s:CMP_STRIDE + s + 1, :]
    out = jnp.zeros((n, 2 * LANE), f32)
    for e in range(2):
        acc = jnp.zeros((n + 8, 2 * LANE), f32)
        for s in range(CMP_STRIDE):
            acc = acc + _dot(xs_ref[e, s].astype(bf16), w1_ref[e, s])
        acc_ref[...] = acc
        bias = acc_ref[n + 1:n + 2, 0:LANE] + acc_ref[n + 2:n + 3, LANE:2 * LANE]
        hid = acc_ref[0:n, 0:LANE] + acc_ref[pl.ds(1, n), LANE:2 * LANE] + bias
        out = out + _dot(jax.nn.gelu(hid).astype(bf16), w2_ref[e])
    o_ref[...] = out


def _cmp_weight_specs():
    z = lambda nd: (lambda *a: (0,) * nd)
    return [pl.BlockSpec((LANE, LANE), z(2)),
            pl.BlockSpec((2, CMP_BLOCK, LANE), z(3)),
            pl.BlockSpec((2, CMP_STRIDE, LANE, 2 * LANE), z(4)),
            pl.BlockSpec((2, LANE, 2 * LANE), z(3))]


def _cmp_scratch(n):
    return [pltpu.VMEM((2, CMP_STRIDE, n + 8, LANE), f32), pltpu.VMEM((n + 8, 2 * LANE), f32)]


def _compress_prompt_body(kvT_ref, perm_ref, pos_ref, w1_ref, w2_ref, o_ref, xs_ref, acc_ref):
    n_pages = kvT_ref.shape[1] // LANE
    get = lambda k: kvT_ref[:, k * LANE:(k + 1) * LANE]
    _compress_pages(get, n_pages, perm_ref, pos_ref, w1_ref, w2_ref, o_ref, xs_ref, acc_ref)


def compress_prompt(kvT, layer, cw):
    _, B, _, T = kvT.shape
    assert T % LANE == 0
    n = T // CMP_STRIDE
    return pl.pallas_call(
        _compress_prompt_body,
        grid=(B,),
        in_specs=[pl.BlockSpec((None, None, KV_ROWS, T), lambda b: (layer, b, 0, 0))] + _cmp_weight_specs(),
        out_specs=pl.BlockSpec((None, n, 2 * LANE), lambda b: (b, 0, 0)),
        out_shape=jax.ShapeDtypeStruct((B, n, 2 * LANE), f32),
        scratch_shapes=_cmp_scratch(n),
        compiler_params=_cparams(("parallel",)),
        name="compress_prompt",
    )(kvT, *cw)


def _page_specs(layer, n_pages, page):
    return [pl.BlockSpec((None, None, KV_ROWS, page), lambda b, pt, k=k: (layer, pt[b, k], 0, 0))
            for k in range(n_pages)]


def _compress_paged_body(pt_ref, perm_ref, pos_ref, w1_ref, w2_ref, *rest, n_pages):
    pages, (o_ref, xs_ref, acc_ref) = rest[:n_pages], rest[n_pages:]
    get = lambda k: pages[k][...]
    _compress_pages(get, n_pages, perm_ref, pos_ref, w1_ref, w2_ref, o_ref, xs_ref, acc_ref)


def compress_paged(cacheT, layer, page_table, cw):
    Bd, n_pages = page_table.shape
    page = cacheT.shape[-1]
    assert page == LANE
    L = n_pages * page
    n = L // CMP_STRIDE
    return pl.pallas_call(
        functools.partial(_compress_paged_body, n_pages=n_pages),
        grid_spec=pltpu.PrefetchScalarGridSpec(
            num_scalar_prefetch=1,
            grid=(Bd,),
            in_specs=_cmp_weight_specs() + _page_specs(layer, n_pages, page),
            out_specs=pl.BlockSpec((None, n, 2 * LANE), lambda b, pt: (b, 0, 0)),
            scratch_shapes=_cmp_scratch(n),
        ),
        out_shape=jax.ShapeDtypeStruct((Bd, n, 2 * LANE), f32),
        compiler_params=_cparams(("parallel",)),
        name="compress_paged",
    )(page_table, *cw, *([cacheT] * n_pages))


def _select_blocks(imp, s_ok, forced, n_sel, jj):
    k = min(N_SEL, n_sel)
    score = jnp.where(s_ok, imp + jnp.where(forced, FORCE_BONUS, 0.0), NEG)
    rank = jnp.zeros(score.shape, f32)
    for j2 in range(n_sel):
        col = score[:, j2:j2 + 1]
        ge = jnp.where(col >= score, 1.0, 0.0)
        gt = jnp.where(col > score, 1.0, 0.0)
        rank = rank + jnp.where(jj > j2, ge, gt)
    return jnp.where((rank < k) & s_ok, 1.0, 0.0)


def _select_blocks_t(imp, t0, n_sel):
    tq, sw = imp.shape
    rows = -(-n_sel // 8) * 8
    k = min(N_SEL, n_sel)
    jj = lax.broadcasted_iota(jnp.int32, (rows, tq), 0)
    qpos = t0 + lax.broadcasted_iota(jnp.int32, (rows, tq), 1)
    cur = jnp.right_shift(qpos, SEL_SHIFT)
    s_ok = (jj * SEL_BLOCK <= qpos) & (jj < n_sel)
    forced = (jj == 0) | (jj == cur) | (jj == cur - 1)
    imp_t = jnp.concatenate([imp[c * LANE:(c + 1) * LANE].T for c in range(tq // LANE)], axis=1)[:rows]
    score = jnp.where(s_ok, imp_t + jnp.where(forced, FORCE_BONUS, 0.0), NEG)
    rank = jnp.zeros(score.shape, f32)
    for j2 in range(n_sel):
        row = score[j2:j2 + 1, :]
        ge = jnp.where(row >= score, 1.0, 0.0)
        gt = jnp.where(row > score, 1.0, 0.0)
        rank = rank + jnp.where(jj > j2, ge, gt)
    sel_t = jnp.where((rank < k) & s_ok, 1.0, 0.0)
    sel_t = jnp.concatenate([sel_t, jnp.zeros((sw - rows, tq), f32)], axis=0)
    return jnp.concatenate([sel_t[:, c * LANE:(c + 1) * LANE].T for c in range(tq // LANE)], axis=0)


def _softmax_rows(sm, ok):
    m = jnp.max(sm, axis=-1, keepdims=True)
    e = jnp.where(ok, jnp.exp(sm - m), 0.0)
    den = jnp.sum(e, axis=-1, keepdims=True)
    return e / jnp.maximum(den, 1e-30)


def _pv_and_rowsum(e, vT):
    v_aug = jnp.concatenate([vT, jnp.ones(vT.shape, vT.dtype)], axis=0)
    return _dot_nt(e.astype(bf16), v_aug)


def _normalise(acc):
    return (acc / pltpu.roll(acc, HD, axis=1))[:, :HD]


def _nsa_prompt_body(q_ref, ag_ref, comp_ref, ksT_ref, vsT_ref, kwT_ref, vwT_ref, ov_ref, ex_ref, o_ref,
                     *, tq, n_cmp, n_sel, wk, ck):
    i = pl.program_id(1)
    t0 = i * tq
    scale = HD ** -0.5
    NC = comp_ref.shape[0]
    SW = ov_ref.shape[1]
    qpos = t0 + lax.broadcasted_iota(jnp.int32, (tq, 1), 0)
    col_n = lax.broadcasted_iota(jnp.int32, (tq, NC), 1)
    c_ok = ((col_n * CMP_STRIDE + (CMP_BLOCK - 1)) <= qpos) & (col_n < n_cmp)
    jj = lax.broadcasted_iota(jnp.int32, (tq, SW), 1)
    s_ok = (jj * SEL_BLOCK <= qpos) & (jj < n_sel)
    wstart = pl.multiple_of(jnp.maximum(t0 + tq - wk, 0), LANE)
    kposw = wstart + lax.broadcasted_iota(jnp.int32, (tq, wk), 1)
    rel = qpos - kposw
    w_bias4 = jnp.concatenate([jnp.where((rel >= 0) & (rel < WINDOW), 0.0, NEG)] * HPG, axis=0)
    c_ok4 = jnp.concatenate([c_ok] * HPG, axis=0)
    sg = jax.nn.sigmoid(ag_ref[...])
    n_chunks = (t0 + tq + ck - 1) // ck
    kcol = lax.broadcasted_iota(jnp.int32, (tq, ck), 1)
    all_valid_selected = t0 + tq <= min(N_SEL, n_sel) * SEL_BLOCK

    GS = range(NSA_KV_HEADS)
    rows = lambda g: slice(g * HD, (g + 1) * HD)
    kc = [comp_ref[:, g * LANE:g * LANE + HD].astype(bf16) for g in GS]
    vc = [comp_ref[:, g * LANE + HD:(g + 1) * LANE].astype(bf16) for g in GS]
    q4 = [jnp.concatenate([(q_ref[:, (g * HPG + h) * HD:(g * HPG + h + 1) * HD] * scale).astype(bf16)
                           for h in range(HPG)], axis=0) for g in GS]
    kw = [kwT_ref[rows(g), pl.ds(wstart, wk)].astype(bf16) for g in GS]
    vw = [vwT_ref[rows(g), pl.ds(wstart, wk)].astype(bf16) for g in GS]
    s_c = [_dot_nt(q4[g], kc[g]) for g in GS]
    s_w = [_dot(q4[g], kw[g]) + w_bias4 for g in GS]
    p_c = [_softmax_rows(jnp.where(c_ok4, s_c[g], NEG), c_ok4) for g in GS]
    e_w = [jnp.exp(s_w[g] - jnp.max(s_w[g], axis=-1, keepdims=True)) for g in GS]
    o_c = [_dot(p_c[g].astype(bf16), vc[g]) for g in GS]
    o_w = [_normalise(_pv_and_rowsum(e_w[g], vw[g])) for g in GS]
    imp = []
    for g in GS:
        psum = p_c[g][0:tq]
        for h in range(1, HPG):
            psum = psum + p_c[g][h * tq:(h + 1) * tq]
        imp.append(_split_dot(psum, ov_ref[...]))
    sel = lax.cond(all_valid_selected,
                   lambda: tuple(jnp.where(s_ok, 1.0, 0.0) for _ in GS),
                   lambda: tuple(_select_blocks_t(imp[g], t0, n_sel) for g in GS))
    selb = [s.astype(bf16) for s in sel]

    def chunk_step(c, carry):
        k0 = pl.multiple_of(c * ck, ck)
        causal = k0 + kcol <= qpos
        ex = ex_ref[:, pl.ds(k0, ck)]
        kT = [ksT_ref[rows(g), pl.ds(k0, ck)].astype(bf16) for g in GS]
        vT = [vsT_ref[rows(g), pl.ds(k0, ck)].astype(bf16) for g in GS]
        bias = [jnp.where((_dot(selb[g], ex) > 0.5) & causal, 0.0, NEG) for g in GS]
        s = [_dot(q4[g], kT[g]) + jnp.concatenate([bias[g]] * HPG, axis=0) for g in GS]
        m_new = [jnp.maximum(carry[g][0], jnp.max(s[g], axis=-1, keepdims=True)) for g in GS]
        e = [jnp.exp(s[g] - m_new[g]) for g in GS]
        pv = [_pv_and_rowsum(e[g], vT[g]) for g in GS]
        return tuple((m_new[g], jnp.exp(carry[g][0] - m_new[g]) * carry[g][1] + pv[g]) for g in GS)

    init = tuple((jnp.full((HPG * tq, 1), NEG, f32), jnp.zeros((HPG * tq, 2 * HD), f32)) for _ in GS)
    state = lax.fori_loop(0, n_chunks, chunk_step, init)

    for g in GS:
        gate = lambda br: jnp.concatenate(
            [sg[:, 3 * (g * HPG + h) + br:3 * (g * HPG + h) + br + 1] for h in range(HPG)], axis=0)
        o = gate(0) * o_c[g] + gate(1) * _normalise(state[g][1]) + gate(2) * o_w[g]
        for h in range(HPG):
            o_ref[:, (g * HPG + h) * HD:(g * HPG + h + 1) * HD] = o[h * tq:(h + 1) * tq]


def _overlap_matrix(nc_rows, n_cmp, sw_cols, n_sel):
    cs = jnp.arange(nc_rows) * CMP_STRIDE
    ss = jnp.arange(sw_cols) * SEL_BLOCK
    ok = (cs[:, None] < ss[None, :] + SEL_BLOCK) & (cs[:, None] + CMP_BLOCK > ss[None, :])
    ok = ok & (jnp.arange(nc_rows)[:, None] < n_cmp) & (jnp.arange(sw_cols)[None, :] < n_sel)
    return ok.astype(bf16)


def nsa_prompt(proj, selT, layer, winT, comp):
    B, T, _ = proj.shape
    tq = min(256, T)
    n_cmp = T // CMP_STRIDE - (CMP_BLOCK // CMP_STRIDE) + 1
    n_sel = -(-T // SEL_BLOCK)
    NC = comp.shape[1]
    SW = LANE
    assert n_sel <= SW and T % tq == 0
    wk = min(WINDOW + tq, T)
    ck = min(512, T)
    assert T % ck == 0
    ov = _overlap_matrix(NC, n_cmp, SW, n_sel)
    ex = (jnp.arange(SW)[:, None] == (jnp.arange(T)[None, :] // SEL_BLOCK)).astype(bf16)
    sel_spec = lambda blk: pl.BlockSpec((None, None, LANE, T), lambda b, i: (layer, b, blk, 0))
    win_spec = lambda blk: pl.BlockSpec((None, LANE, T), lambda b, i: (b, blk, 0))
    return pl.pallas_call(
        functools.partial(_nsa_prompt_body, tq=tq, n_cmp=n_cmp, n_sel=n_sel, wk=wk, ck=ck),
        grid=(B, T // tq),
        in_specs=[pl.BlockSpec((None, tq, A_Q), lambda b, i: (b, i, OFF_AQ // A_Q)),
                  pl.BlockSpec((None, tq, LANE), lambda b, i: (b, i, OFF_AG // LANE)),
                  pl.BlockSpec((None, NC, 2 * LANE), lambda b, i: (b, 0, 0)),
                  sel_spec(0), sel_spec(1), win_spec(0), win_spec(1),
                  pl.BlockSpec((NC, SW), lambda b, i: (0, 0)),
                  pl.BlockSpec((SW, T), lambda b, i: (0, 0))],
        out_specs=pl.BlockSpec((None, tq, A_Q), lambda b, i: (b, i, 0)),
        out_shape=jax.ShapeDtypeStruct((B, T, A_Q), f32),
        compiler_params=_cparams(("parallel", "arbitrary")),
        name="nsa_prompt",
    )(proj, proj, comp, selT, selT, winT, winT, ov, ex)


def _nsa_sample_body(pt_ref, q_ref, ag_ref, comp_ref, kwT_ref, newT_ref, ov_ref, rsum_ref, rexp_ref, ex_ref, *rest,
                     tn, past, n_cmp, n_sel, n_pages):
    RB = q_ref.shape[0]
    pages, o_ref = rest[:RB * n_pages], rest[RB * n_pages]
    scale = HD ** -0.5
    R = HPG * tn
    NC = comp_ref.shape[1]
    SW = ov_ref.shape[1]
    tok = lax.rem(lax.broadcasted_iota(jnp.int32, (R, 1), 0), tn)
    qpos = past + tok
    col_n = lax.broadcasted_iota(jnp.int32, (R, NC), 1)
    c_ok = ((col_n * CMP_STRIDE + (CMP_BLOCK - 1)) <= qpos) & (col_n < n_cmp)
    jj = lax.broadcasted_iota(jnp.int32, (tn, SW), 1)
    qpos_t = past + lax.broadcasted_iota(jnp.int32, (tn, 1), 0)
    cur = jnp.right_shift(qpos_t, SEL_SHIFT)
    s_ok = (jj * SEL_BLOCK <= qpos_t) & (jj < n_sel)
    forced = (jj == 0) | (jj == cur) | (jj == cur - 1)
    wlen = kwT_ref.shape[2]
    kposw = (past - wlen) + lax.broadcasted_iota(jnp.int32, (R, wlen), 1)
    relw = qpos - kposw
    w_ok = (relw >= 0) & (relw < WINDOW)
    nl = newT_ref.shape[2]
    u = lax.broadcasted_iota(jnp.int32, (R, nl), 1)
    reln = tok - u
    n_ok = (reln >= 0) & (reln < WINDOW) & (u < tn)
    nb = past // SEL_BLOCK
    chains = [(r, g) for r in range(RB) for g in range(NSA_KV_HEADS)]
    each = lambda f: [f(r, g) for r, g in chains]
    krow = lambda g: slice(g * HD, (g + 1) * HD)
    vrow = lambda g: slice(2 * HD + g * HD, 2 * HD + (g + 1) * HD)
    new = lambda r, branch, rows: newT_ref[r, branch * KV_ROWS + rows.start:branch * KV_ROWS + rows.stop, :]

    def masked_attention(qs, parts_of):
        s = [[jnp.where(ok, _dot(q, kT), NEG) for kT, _, ok in parts] for q, parts in zip(qs, parts_of)]
        m = [functools.reduce(jnp.maximum, [jnp.max(x, axis=-1, keepdims=True) for x in si]) for si in s]
        e = [[jnp.exp(x - mi) for x in si] for si, mi in zip(s, m)]
        den = [sum(jnp.sum(x, axis=-1, keepdims=True) for x in ei) for ei in e]
        return [sum(_dot_nt(x.astype(bf16), vT) for x, (_, vT, _) in zip(ei, parts)) / di
                for ei, parts, di in zip(e, parts_of, den)]

    q = each(lambda r, g: (q_ref[r, g * R:(g + 1) * R, :] * scale).astype(bf16))
    kc = each(lambda r, g: comp_ref[r, :, g * LANE:g * LANE + HD].astype(bf16))
    vc = each(lambda r, g: comp_ref[r, :, g * LANE + HD:(g + 1) * LANE].astype(bf16))
    n = range(len(chains))
    pc = [_softmax_rows(jnp.where(c_ok, _dot_nt(q[i], kc[i]), NEG), c_ok) for i in n]
    o_c = [_dot(pc[i].astype(bf16), vc[i]) for i in n]
    imp = [_split_dot(_rsum_exact(rsum_ref, pc[i]), ov_ref[...]) for i in n]
    sel = [_select_blocks(imp[i], s_ok, forced, n_sel, jj) for i in n]
    selr = [_dot(rexp_ref[...], sel[i].astype(bf16)) for i in n]
    o_w = masked_attention(q, [[(kwT_ref[r, krow(g), :].astype(bf16), kwT_ref[r, vrow(g), :].astype(bf16), w_ok),
                                (new(r, 2, krow(g)).astype(bf16), new(r, 2, vrow(g)).astype(bf16), n_ok)]
                               for r, g in chains])
    parts_of = []
    for i, (r, g) in enumerate(chains):
        pgs = pages[r * n_pages:(r + 1) * n_pages]
        ksp = jnp.concatenate([pg[krow(g), :] for pg in pgs], axis=1).astype(bf16)
        vsp = jnp.concatenate([pg[vrow(g), :] for pg in pgs], axis=1).astype(bf16)
        okp = _dot(selr[i].astype(bf16), ex_ref[...]) > 0.5
        okn = (u <= tok) & (u < tn) & (selr[i][:, nb:nb + 1] > 0.5)
        parts_of.append([(ksp, vsp, okp), (new(r, 1, krow(g)).astype(bf16), new(r, 1, vrow(g)).astype(bf16), okn)])
    o_s = masked_attention(q, parts_of)
    for i, (r, g) in enumerate(chains):
        gg = jax.nn.sigmoid(ag_ref[r, g * R:(g + 1) * R, :])
        o_ref[r, g * R:(g + 1) * R, :] = gg[:, 0:1] * o_c[i] + gg[:, 1:2] * o_s[i] + gg[:, 2:3] * o_w[i]


def _rsum_exact(rsum_ref, pc):
    r = rsum_ref[...]
    hi = pc.astype(bf16)
    lo = (pc - hi.astype(f32)).astype(bf16)
    return _dot(r, hi) + _dot(r, lo)


def nsa_sample(q_s, ag_s, comp, selT, layer, page_table, winT, newT, *, tn, past):
    Bd, n_pages = page_table.shape
    page = selT.shape[-1]
    R = HPG * tn
    L = past + tn
    n_cmp = L // CMP_STRIDE - (CMP_BLOCK // CMP_STRIDE) + 1
    n_sel = -(-L // SEL_BLOCK)
    NC = comp.shape[1]
    SW = -(-n_sel // LANE) * LANE
    assert past % page == 0 and page % SEL_BLOCK == 0 and tn <= SEL_BLOCK and n_cmp <= NC
    wlen = winT.shape[-1]
    ov = _overlap_matrix(NC, n_cmp, SW, n_sel)
    rsum = (jnp.arange(tn)[:, None] == (jnp.arange(R)[None, :] % tn)).astype(bf16)
    ex = (jnp.arange(SW)[:, None] == (jnp.arange(past)[None, :] // SEL_BLOCK)).astype(bf16)
    const = lambda *shape: pl.BlockSpec(shape, lambda b, pt: (0,) * len(shape))
    RB = 2 if Bd % 2 == 0 else 1
    page_specs = [pl.BlockSpec((None, None, KV_ROWS, page), lambda b, pt, r=r, k=k: (layer, pt[b * RB + r, k], 0, 0))
                  for r in range(RB) for k in range(n_pages)]
    return pl.pallas_call(
        functools.partial(_nsa_sample_body, tn=tn, past=past, n_cmp=n_cmp, n_sel=n_sel, n_pages=n_pages),
        grid_spec=pltpu.PrefetchScalarGridSpec(
            num_scalar_prefetch=1,
            grid=(Bd // RB,),
            in_specs=[pl.BlockSpec((RB, 2 * R, HD), lambda b, pt: (b, 0, 0)),
                      pl.BlockSpec((RB, 2 * R, 3), lambda b, pt: (b, 0, 0)),
                      pl.BlockSpec((RB, NC, 2 * LANE), lambda b, pt: (b, 0, 0)),
                      pl.BlockSpec((None, RB, KV_ROWS, wlen), lambda b, pt: (layer, b, 0, 0)),
                      pl.BlockSpec((RB, A_KV, LANE), lambda b, pt: (b, 0, 0)),
                      const(NC, SW), const(tn, R), const(R, tn), _resident(ex)]
            + page_specs,
            out_specs=pl.BlockSpec((RB, 2 * R, HD), lambda b, pt: (b, 0, 0)),
        ),
        out_shape=jax.ShapeDtypeStruct((Bd, 2 * R, HD), f32),
        compiler_params=_cparams(("parallel",)),
        name="nsa_sample",
    )(page_table, q_s, ag_s, comp, winT, newT, ov, rsum, rsum.T, ex, *([selT] * (RB * n_pages)))


def _retention_body(q_ref, k_ref, v_ref, cos_ref, sin_ref, dm_ref, qd_ref, kd_ref, sd_ref, s0_ref, *rest):
    o_ref, sn_ref, s_scr = rest[-3:]
    c = pl.program_id(1)

    @pl.when(c == 0)
    def _():
        s_scr[...] = s0_ref[...]

    C = dm_ref.shape[1]
    for cc in range(q_ref.shape[0] // C):
        r = slice(cc * C, (cc + 1) * C)
        cos = cos_ref[r, :]
        sin = sin_ref[r, :]

        def rot(x):
            return x * cos + pltpu.roll(x, RET_DK // 2, axis=1) * sin

        for h in range(RET_HEADS):
            q = rot(q_ref[r, h * RET_DK:(h + 1) * RET_DK])
            k = rot(k_ref[r, h * RET_DK:(h + 1) * RET_DK]) * (RET_DK ** -0.5)
            v = v_ref[r, h * RET_DV:(h + 1) * RET_DV].astype(bf16)
            S = s_scr[h]
            inner = _dot_nt(q.astype(bf16), k.astype(bf16)) * dm_ref[h]
            o_ref[r, h * RET_DV:(h + 1) * RET_DV] = (_dot(inner.astype(bf16), v)
                                                     + _dot((q * qd_ref[h]).astype(bf16), S.astype(bf16)))
            s_scr[h] = S * sd_ref[h] + _dot_tn((k * kd_ref[h]).astype(bf16), v)

    @pl.when(c == pl.num_programs(1) - 1)
    def _():
        if len(sn_ref.shape) == 4:
            sn_ref[0] = s_scr[...]
            if sn_ref.shape[0] > 1:
                sn_ref[1:] = jnp.zeros((sn_ref.shape[0] - 1,) + s_scr.shape, f32)
        else:
            sn_ref[...] = s_scr[...]


def retention(proj, pos0, states, layer, out_layer, depth, prev_new):
    B, T, _ = proj.shape
    C = RET_CHUNK if T % RET_CHUNK == 0 else T
    n = T // C
    H = RET_HEADS
    half = RET_DK // 2
    inv = jnp.exp(-jnp.log(ROPE_BASE) * jnp.arange(half, dtype=f32) / half)
    ang = (pos0 + jnp.arange(T)).astype(f32)[:, None] * inv[None, :]
    cos = jnp.concatenate([jnp.cos(ang), jnp.cos(ang)], axis=-1)
    sin = jnp.concatenate([-jnp.sin(ang), jnp.sin(ang)], axis=-1)
    log_g = jnp.log(1.0 - jnp.exp2(-5.0 - jnp.arange(H, dtype=f32)))
    i = jnp.arange(C, dtype=f32)
    diff = i[:, None] - i[None, :]
    dm = jnp.where(diff >= 0, jnp.exp(jnp.maximum(diff, 0.0)[None] * log_g[:, None, None]), 0.0)
    qd = jnp.exp((i + 1.0)[None, :] * log_g[:, None])[..., None]
    kd = jnp.exp((C - 1.0 - i)[None, :] * log_g[:, None])[..., None]
    sd = jnp.exp(C * log_g)[:, None, None]
    tab = lambda a: pl.BlockSpec(a.shape, lambda b, c: (0, 0, 0))
    cps = 4 if n % 4 == 0 else 1
    R = cps * C
    in_specs = [pl.BlockSpec((None, R, B_QK), lambda b, c: (b, c, OFF_RQ // B_QK)),
                pl.BlockSpec((None, R, B_QK), lambda b, c: (b, c, OFF_RK // B_QK)),
                pl.BlockSpec((None, R, B_V), lambda b, c: (b, c, OFF_RV // B_V)),
                pl.BlockSpec((R, RET_DK), lambda b, c: (c, 0)),
                pl.BlockSpec((R, RET_DK), lambda b, c: (c, 0)),
                tab(dm), tab(qd), tab(kd), tab(sd),
                pl.BlockSpec((None, None, H, RET_DK, RET_DV), lambda b, c: (layer, b, 0, 0, 0))]
    args = [proj, proj, proj, cos, sin, dm, qd, kd, sd, states]
    if prev_new is None:
        assert out_layer == 0
        st = pl.BlockSpec((depth, None, H, RET_DK, RET_DV), lambda b, c: (0, b, 0, 0, 0))
        aliases = {}
    else:
        st = pl.BlockSpec((None, None, H, RET_DK, RET_DV), lambda b, c: (out_layer, b, 0, 0, 0))
        in_specs.append(pl.BlockSpec(memory_space=pl.ANY))
        aliases = {len(args): 1}
        args.append(prev_new)
    return pl.pallas_call(
        _retention_body,
        grid=(B, n // cps),
        in_specs=in_specs,
        out_specs=[pl.BlockSpec((None, R, B_V), lambda b, c: (b, c, 0)), st],
        out_shape=[jax.ShapeDtypeStruct((B, T, B_V), f32),
                   jax.ShapeDtypeStruct((depth, B, H, RET_DK, RET_DV), f32)],
        input_output_aliases=aliases,
        scratch_shapes=[pltpu.VMEM((H, RET_DK, RET_DV), f32)],
        compiler_params=_cparams(("parallel", "arbitrary")),
        name="retention",
    )(*args)


def _mix_out_body(rg_ref, ga_ref, gb_ref, attn_ref, ro_ref, x_ref, g1_ref, woa_ref, gnw_ref, wob_ref, wo_ref, o_ref):
    ya = _dot(attn_ref[...].astype(bf16), woa_ref[...])
    parts = []
    for h in range(RET_HEADS):
        r = ro_ref[:, h * RET_DV:(h + 1) * RET_DV]
        d = r - jnp.mean(r, axis=-1, keepdims=True)
        var = jnp.mean(d * d, axis=-1, keepdims=True)
        parts.append(d * lax.rsqrt(var + GN_EPS) * gnw_ref[:, h * RET_DV:(h + 1) * RET_DV])
    ron = jnp.concatenate(parts, axis=-1)
    rg = rg_ref[...]
    yb = _dot((rg * jax.nn.sigmoid(rg) * ron).astype(bf16), wob_ref[...])
    m = jax.nn.sigmoid(ga_ref[...]) * ya + jax.nn.sigmoid(gb_ref[...]) * yb
    o_ref[...] = x_ref[...] + g1_ref[...] * _dot(m.astype(bf16), wo_ref[...])


def _mod_spec(mod, tm, D):
    if mod.shape[1] != 1:
        return pl.BlockSpec((None, tm, D), lambda g, i: (g, i, 0))
    return pl.BlockSpec((None, 1, D), lambda g, i: (g, 0, 0))


def mix_out(proj, attn, ro, x, g1, w_oa, gn_w, w_ob, w_o, *, tm):
    G, T, D = x.shape
    full = lambda a: pl.BlockSpec(a.shape, lambda g, i: (0,) * a.ndim)
    wide = lambda blk: pl.BlockSpec((None, tm, B_V), lambda g, i: (g, i, blk))
    gn_w = gn_w.reshape(1, B_V)
    return pl.pallas_call(
        _mix_out_body,
        grid=(G, T // tm),
        in_specs=[wide(OFF_RG // B_V), wide(OFF_GA // B_V), wide(OFF_GB // B_V),
                  pl.BlockSpec((None, tm, A_Q), lambda g, i: (g, i, 0)),
                  wide(0),
                  pl.BlockSpec((None, tm, D), lambda g, i: (g, i, 0)),
                  _mod_spec(g1, tm, D), full(w_oa), full(gn_w), full(w_ob), full(w_o)],
        out_specs=pl.BlockSpec((None, tm, D), lambda g, i: (g, i, 0)),
        out_shape=jax.ShapeDtypeStruct((G, T, D), f32),
        compiler_params=_cparams(("parallel", "parallel")),
        name="mix_out",
    )(proj, proj, proj, attn, ro, x, g1, w_oa, gn_w, w_ob, w_o)


def _ffn_tail(a, am1, am2, b_ref, cw_ref, cb_ref, wout_ref, x_ref, g2_ref, nf_ref, o_ref, final_norm):
    u = cb_ref[...] + am2 * cw_ref[0:1, :] + am1 * cw_ref[1:2, :] + a * cw_ref[2:3, :]
    y = _dot((jax.nn.gelu(u) * b_ref[...]).astype(bf16), wout_ref[...])
    xo = x_ref[...] + g2_ref[...] * y
    if final_norm:
        xo = xo * lax.rsqrt(jnp.mean(xo * xo, axis=-1, keepdims=True) + RMS_EPS) * nf_ref[...]
    o_ref[...] = xo


def _ffn_seq_body(x_ref, sc_ref, sh_ref, nw_ref, win_ref, prev_ref, cw_ref, cb_ref, wout_ref, g2_ref, nf_ref,
                  o_ref, conv_ref, scr, tail_scr, *, final_norm):
    tm = x_ref.shape[0]
    F = wout_ref.shape[0]
    fc = scr.shape[1]
    h = _norm_mod(x_ref[...], nw_ref, sc_ref, sh_ref)
    first = pl.program_id(1) == 0
    y = jnp.zeros(o_ref.shape, f32)
    for c0 in range(0, F, fc):
        cols = slice(c0, c0 + fc)
        a = _dot(h, win_ref[:, cols])
        scr[8:8 + tm, :] = a
        scr[6:8, :] = jnp.where(first, prev_ref[:, cols], tail_scr[:, cols])
        last2 = scr[pl.ds(tm + 6, 2), :]
        tail_scr[:, cols] = last2
        conv_ref[:, cols] = last2
        b = _dot(h, win_ref[:, F + c0:F + c0 + fc])
        u = (cb_ref[:, cols] + scr[pl.ds(6, tm), :] * cw_ref[0:1, cols] + scr[pl.ds(7, tm), :] * cw_ref[1:2, cols]
             + a * cw_ref[2:3, cols])
        y = y + _dot((jax.nn.gelu(u) * b).astype(bf16), wout_ref[cols, :])
    xo = x_ref[...] + g2_ref[...] * y
    if final_norm:
        xo = xo * lax.rsqrt(jnp.mean(xo * xo, axis=-1, keepdims=True) + RMS_EPS) * nf_ref[...]
    o_ref[...] = xo


def ffn_seq(x, sc, sh, nw, w_in, prev, g2, conv_w, conv_b, w_out, normf_w, *, tm, final_norm):
    B, T, D = x.shape
    F = w_out.shape[0]
    assert T % tm == 0 and tm >= CONV_W - 1
    fc = F
    full = lambda a: pl.BlockSpec(a.shape, lambda g, i: (0,) * a.ndim)
    conv_b = conv_b.reshape(1, F)
    normf_w = normf_w.reshape(1, D)
    nw = nw.reshape(1, D)
    rows = pl.BlockSpec((None, tm, D), lambda g, i: (g, i, 0))
    state = pl.BlockSpec((None, CONV_W - 1, F), lambda g, i: (g, 0, 0))
    return pl.pallas_call(
        functools.partial(_ffn_seq_body, final_norm=final_norm),
        grid=(B, T // tm),
        in_specs=[rows, _mod_spec(sc, tm, D), _mod_spec(sh, tm, D), full(nw), _resident(w_in), state,
                  full(conv_w), full(conv_b), _resident(w_out), _mod_spec(g2, tm, D), full(normf_w)],
        out_specs=[rows, state],
        out_shape=[jax.ShapeDtypeStruct((B, T, D), f32), jax.ShapeDtypeStruct((B, CONV_W - 1, F), f32)],
        scratch_shapes=[pltpu.VMEM((tm + 8, fc), f32), pltpu.VMEM((CONV_W - 1, F), f32)],
        compiler_params=_cparams(("parallel", "arbitrary")),
        name="ffn_seq",
    )(x, sc, sh, nw, w_in, prev, conv_w, conv_b, w_out, g2, normf_w)


def _ffn_out_rows_body(a_ref, am1_ref, am2_ref, b_ref, cw_ref, cb_ref, wout_ref, x_ref, g2_ref, nf_ref, o_ref,
                       *, final_norm):
    _ffn_tail(a_ref[...], am1_ref[...], am2_ref[...], b_ref, cw_ref, cb_ref, wout_ref, x_ref, g2_ref, nf_ref, o_ref,
              final_norm)


def ffn_out_rows(ab, am1, am2, x, g2, conv_w, conv_b, w_out, normf_w, *, tm, final_norm):
    G, T, D = x.shape
    F = w_out.shape[0]
    full = lambda a: pl.BlockSpec(a.shape, lambda g, i: (0,) * a.ndim)
    conv_b = conv_b.reshape(1, F)
    normf_w = normf_w.reshape(1, D)
    rowsF = lambda blk: pl.BlockSpec((None, tm, F), lambda g, i: (g, i, blk))
    return pl.pallas_call(
        functools.partial(_ffn_out_rows_body, final_norm=final_norm),
        grid=(G, T // tm),
        in_specs=[rowsF(0), rowsF(0), rowsF(0), rowsF(1),
                  full(conv_w), full(conv_b), full(w_out),
                  pl.BlockSpec((None, tm, D), lambda g, i: (g, i, 0)),
                  _mod_spec(g2, tm, D), full(normf_w)],
        out_specs=pl.BlockSpec((None, tm, D), lambda g, i: (g, i, 0)),
        out_shape=jax.ShapeDtypeStruct((G, T, D), f32),
        compiler_params=_cparams(("parallel", "parallel")),
        name="ffn_out_rows",
    )(ab, am1, am2, ab, conv_w, conv_b, w_out, x, g2, normf_w)


def _prep_w_in(w):
    D = w.shape[0]
    o = 0
    parts = {}
    for name, n in (("aq", A_Q), ("akv", A_KV), ("ag", A_G), ("rq", B_QK), ("rk", B_QK), ("rv", B_V), ("rg", B_V),
                    ("ga", D), ("gb", D)):
        parts[name] = w[:, o:o + n]
        o += n
    pad = jnp.zeros((D, N_TOK - OFF_AG - A_G), w.dtype)
    tok = jnp.concatenate([parts["rg"], parts["ga"], parts["gb"], parts["rv"], parts["aq"], parts["rq"], parts["rk"],
                           parts["ag"], pad], axis=1)
    return tok.astype(bf16), parts["akv"].T.astype(bf16)


def _prep_cmp(cmp_pos, cmp_w1, cmp_w2):
    G = NSA_KV_HEADS
    r = jnp.arange(LANE)
    perm = ((r % CPP)[:, None] * CMP_STRIDE + (r // CPP)[:, None] == r[None, :]).astype(bf16)
    pos = jnp.concatenate([cmp_pos] * G, axis=-1)
    w1 = cmp_w1.reshape(2, CMP_BLOCK, HD, CMP_HIDDEN)
    eye = jnp.eye(G, dtype=w1.dtype)
    w1bd = jnp.einsum("eldh,gk->elgdkh", w1, eye).reshape(2, CMP_BLOCK, G * HD, G * CMP_HIDDEN)
    w1cat = jnp.concatenate([w1bd[:, :CMP_STRIDE], w1bd[:, CMP_STRIDE:]], axis=-1)
    eye_e = jnp.eye(2, dtype=w1.dtype)
    w2bd = jnp.einsum("ehd,gk,ef->eghkfd", cmp_w2, eye, eye_e).reshape(2, G * CMP_HIDDEN, G * 2 * HD)
    return perm, pos, w1cat.astype(bf16), w2bd.astype(bf16)


def _kv_rows_to_out(rowsT, lead):
    t = rowsT.shape[-1]
    r = rowsT.reshape(*lead, 2, NSA_KV_HEADS, HD, t)
    n = len(lead)
    return r.transpose(*range(n), n + 3, n, n + 1, n + 2)


def _keep_last_lanes(a, n):
    t = a.shape[-1]
    if t >= n:
        return a[..., t - n:]
    return jnp.pad(a, ((0, 0),) * (a.ndim - 1) + ((n - t, 0),))


def kernel(x_prompt, x_sample, c_prompt, c_sample, cache_cmp_kv, cache_sel_kv, state_win_kv, state_ret, state_conv,
           page_table, norm1_w, ada_w, ada_b, w_in, cmp_pos, cmp_w1, cmp_w2, w_oa, ret_gn_w, w_ob, w_o, norm2_w,
           ffn_w_in, ffn_conv_w, ffn_conv_b, ffn_w_out, normf_w):
    B, T, D = x_prompt.shape
    Bd, Td, _ = x_sample.shape
    depth = w_in.shape[0]
    n_phys, page = cache_cmp_kv.shape[1], cache_cmp_kv.shape[2]
    past = page_table.shape[1] * page
    wlen = state_win_kv.shape[2]
    F = ffn_w_out.shape[1]
    Rs = Bd * Td
    tm_p = min(512, T)

    mod = ada_mod(jnp.concatenate([c_prompt, c_sample], axis=0), ada_w, ada_b)
    cmpT = cache_cmp_kv.transpose(0, 1, 3, 4, 5, 2).reshape(depth, n_phys, KV_ROWS, page)
    selT = cache_sel_kv.transpose(0, 1, 3, 4, 5, 2).reshape(depth, n_phys, KV_ROWS, page)
    winT = state_win_kv.transpose(0, 1, 3, 4, 5, 2).reshape(depth, Bd, KV_ROWS, wlen)

    w_in_all, w_oa_all, w_ob_all, w_o_all, ffn_in_all, ffn_out_all = (
        w.astype(bf16) for w in (w_in, w_oa, w_ob, w_o, ffn_w_in, ffn_w_out))
    ret_zero = jnp.zeros((1, B, RET_HEADS, RET_DK, RET_DV), f32)

    xp = x_prompt
    xs = x_sample.reshape(1, Rs, D)
    kv_p = None
    ret_p = ret_s = None
    outs = [[] for _ in range(10)]
    for l in range(depth):
        w_tok, w_kvT = _prep_w_in(w_in_all[l])
        cw = _prep_cmp(cmp_pos[l], cmp_w1[l], cmp_w2[l])
        w_oa_b, w_ob_b, w_o_b = w_oa_all[l], w_ob_all[l], w_o_all[l]
        ffn_in_b, ffn_out_b = ffn_in_all[l], ffn_out_all[l]
        last = l == depth - 1
        mp = [mod[l, :B, k * D:(k + 1) * D].reshape(B, 1, D) for k in range(6)]
        ms = [jnp.repeat(mod[l, B:, k * D:(k + 1) * D], Td, axis=0).reshape(1, Rs, D) for k in range(6)]

        proj, cmp_p, sel_p, win_p = norm_proj_layers(xp, mp[1], mp[0], norm1_w[l], w_tok, w_kvT, l, depth, kv_p,
                                                     tm=tm_p, tn=PROJ_TN)
        kv_p = (cmp_p, sel_p)
        comp = compress_prompt(cmp_p, l, cw)
        attn = nsa_prompt(proj, sel_p, l, win_p, comp)
        ro, ret_p = retention(proj, 0, ret_zero, 0, l, depth, ret_p)
        x1 = mix_out(proj, attn, ro, xp, mp[2], w_oa_b, ret_gn_w[l], w_ob_b, w_o_b, tm=tm_p)
        xp, conv_p = ffn_seq(x1, mp[4], mp[3], norm2_w[l], ffn_in_b, jnp.zeros((B, CONV_W - 1, F), f32), mp[5],
                             ffn_conv_w[l], ffn_conv_b[l], ffn_out_b, normf_w, tm=min(256, T), final_norm=last)
        outs[4].append(_kv_rows_to_out(_keep_last_lanes(win_p, wlen), (B,)))
        outs[8].append(conv_p)

        proj_s, kvT_s = norm_proj(xs, ms[1], ms[0], norm1_w[l], w_tok, w_kvT, tm=Rs, tn=PROJ_TN)
        comp_s = compress_paged(cmpT, l, page_table, cw)
        q_s = proj_s[0, :, OFF_AQ:OFF_AQ + A_Q].reshape(Bd, Td, NSA_KV_HEADS, HPG, HD)
        q_s = q_s.transpose(0, 2, 3, 1, 4).reshape(Bd, NSA_HEADS * Td, HD)
        ag_s = proj_s[0, :, OFF_AG:OFF_AG + A_G].reshape(Bd, Td, NSA_KV_HEADS, HPG, 3)
        ag_s = ag_s.transpose(0, 2, 3, 1, 4).reshape(Bd, NSA_HEADS * Td, 3)
        newT = kvT_s[0].reshape(A_KV, Bd, Td).transpose(1, 0, 2)
        newT_pad = jnp.pad(newT, ((0, 0), (0, 0), (0, LANE - Td)))
        attn_s = nsa_sample(q_s, ag_s, comp_s, selT, l, page_table, winT, newT_pad, tn=Td, past=past)
        attn_s = attn_s.reshape(Bd, NSA_KV_HEADS, HPG, Td, HD).transpose(0, 3, 1, 2, 4).reshape(1, Rs, A_Q)
        ro_s, ret_s = retention(proj_s.reshape(Bd, Td, N_TOK), past, state_ret, l, l, depth, ret_s)
        x1s = mix_out(proj_s, attn_s, ro_s.reshape(1, Rs, B_V), xs, ms[2], w_oa_b, ret_gn_w[l], w_ob_b, w_o_b, tm=Rs)
        ab_s = norm_proj(x1s, ms[4], ms[3], norm2_w[l], ffn_in_b, None, tm=Rs, tn=F)
        a_ext = jnp.concatenate([state_conv[l], ab_s[0, :, :F].reshape(Bd, Td, F)], axis=1)
        am1 = a_ext[:, 1:1 + Td].reshape(1, Rs, F)
        am2 = a_ext[:, 0:Td].reshape(1, Rs, F)
        xs = ffn_out_rows(ab_s, am1, am2, x1s, ms[5], ffn_conv_w[l], ffn_conv_b[l], ffn_out_b, normf_w, tm=Rs,
                          final_norm=last)
        outs[1].append(_kv_rows_to_out(newT[:, 0:KV_ROWS], (Bd,)))
        outs[3].append(_kv_rows_to_out(newT[:, KV_ROWS:2 * KV_ROWS], (Bd,)))
        win_all = jnp.concatenate([winT[l], newT[:, 2 * KV_ROWS:]], axis=-1)
        outs[5].append(_kv_rows_to_out(_keep_last_lanes(win_all, wlen), (Bd,)))
        outs[9].append(a_ext[:, Td:])

    st = lambda k: jnp.stack(outs[k])
    new_cmp_p = _kv_rows_to_out(kv_p[0], (depth, B))
    new_sel_p = _kv_rows_to_out(kv_p[1], (depth, B))
    return (xp, xs.reshape(Bd, Td, D), new_cmp_p, st(1), new_sel_p, st(3), st(4), st(5), ret_p, ret_s, st(8), st(9))
```

```python
import functools

import jax
import jax.numpy as jnp
from jax import lax
from jax.experimental import pallas as pl
from jax.experimental.pallas import tpu as pltpu

f32 = jnp.float32
bf16 = jnp.bfloat16

NSA_HEADS = 8
NSA_KV_HEADS = 2
HPG = NSA_HEADS // NSA_KV_HEADS
HD = 64
CMP_STRIDE = 16
CMP_BLOCK = 32
CMP_HIDDEN = 64
SEL_BLOCK = 64
SEL_SHIFT = 6
N_SEL = 16
WINDOW = 512
RET_HEADS = 4
RET_DK = 128
RET_DV = 256
RET_CHUNK = 128
ROPE_BASE = 10000.0
CONV_W = 3
RMS_EPS = 1e-6
GN_EPS = 1e-5
NEG = -1e30
FORCE_BONUS = 1e4

A_Q = NSA_HEADS * HD
KV_ROWS = 2 * NSA_KV_HEADS * HD
A_KV = 3 * KV_ROWS
A_G = 3 * NSA_HEADS
B_QK = RET_HEADS * RET_DK
B_V = RET_HEADS * RET_DV

OFF_RG, OFF_GA, OFF_GB, OFF_RV = 0, 1024, 2048, 3072
OFF_AQ, OFF_RQ, OFF_RK, OFF_AG = 4096, 4608, 5120, 5632
N_TOK = 5760
PROJ_TN = 1920

LANE = 128
VMEM_LIMIT = 56 * 1024 * 1024


def _cparams(sem):
    return pltpu.CompilerParams(dimension_semantics=sem, vmem_limit_bytes=VMEM_LIMIT)


def _dot(a, b):
    return jnp.dot(a, b, preferred_element_type=f32)


def _dot_nt(a, b):
    return lax.dot_general(a, b, (((1,), (1,)), ((), ())), preferred_element_type=f32)


def _dot_tn(a, b):
    return lax.dot_general(a, b, (((0,), (0,)), ((), ())), preferred_element_type=f32)


def _split_dot(a, b_bf16):
    hi = a.astype(bf16)
    lo = (a - hi.astype(f32)).astype(bf16)
    return _dot(hi, b_bf16) + _dot(lo, b_bf16)


def _ada_body(c_ref, w_ref, b_ref, o_ref):
    c = c_ref[...]
    s = c * jax.nn.sigmoid(c)
    o_ref[...] = _dot(s.astype(bf16), w_ref[...].astype(bf16)) + b_ref[...]


def ada_mod(c_all, ada_w, ada_b):
    depth, d, n = ada_w.shape
    r = c_all.shape[0]
    tn = 1024
    return pl.pallas_call(
        _ada_body,
        grid=(depth, n // tn),
        in_specs=[pl.BlockSpec((r, d), lambda l, j: (0, 0)),
                  pl.BlockSpec((None, d, tn), lambda l, j: (l, 0, j)),
                  pl.BlockSpec((None, 1, tn), lambda l, j: (l, 0, j))],
        out_specs=pl.BlockSpec((None, r, tn), lambda l, j: (l, 0, j)),
        out_shape=jax.ShapeDtypeStruct((depth, r, n), f32),
        compiler_params=_cparams(("parallel", "parallel")),
        name="ada_mod",
    )(c_all, ada_w, ada_b.reshape(depth, 1, n))


def _norm_proj_body(x_ref, sc_ref, sh_ref, nw_ref, w_ref, *rest, has_t):
    if has_t:
        wt_ref, o_ref, ot_ref, h_scr = rest
    else:
        o_ref, h_scr = rest

    j = pl.program_id(2)
    tn = o_ref.shape[1]

    @pl.when(j == 0)
    def _():
        h_scr[...] = _norm_mod(x_ref[...], nw_ref, sc_ref, sh_ref)
        if has_t:
            ot_ref[...] = _dot_nt(wt_ref[...], h_scr[...])

    o_ref[...] = _dot(h_scr[...], w_ref[:, pl.ds(pl.multiple_of(j * tn, tn), tn)])


def _norm_mod(x, nw_ref, sc_ref, sh_ref):
    y = x * lax.rsqrt(jnp.mean(x * x, axis=-1, keepdims=True) + RMS_EPS) * nw_ref[...]
    return (y * (1.0 + sc_ref[...]) + sh_ref[...]).astype(bf16)


def _resident(a):
    return pl.BlockSpec(a.shape, lambda *_: (0,) * a.ndim, pipeline_mode=pl.Buffered(1))


def norm_proj(x, sc, sh, nw, w, wt, *, tm, tn):
    G, T, D = x.shape
    N = w.shape[1]
    per_row = sc.shape[1] != 1
    mr = tm if per_row else 1
    mod_spec = pl.BlockSpec((None, mr, D), (lambda g, i, j: (g, i, 0)) if per_row else (lambda g, i, j: (g, 0, 0)))
    in_specs = [pl.BlockSpec((None, tm, D), lambda g, i, j: (g, i, 0)), mod_spec, mod_spec,
                pl.BlockSpec((1, D), lambda g, i, j: (0, 0)),
                _resident(w)]
    out_specs = [pl.BlockSpec((None, tm, tn), lambda g, i, j: (g, i, j))]
    out_shape = [jax.ShapeDtypeStruct((G, T, N), f32)]
    args = [x, sc, sh, nw.reshape(1, D), w]
    if wt is not None:
        NT = wt.shape[0]
        in_specs.append(_resident(wt))
        out_specs.append(pl.BlockSpec((None, NT, tm), lambda g, i, j: (g, 0, i)))
        out_shape.append(jax.ShapeDtypeStruct((G, NT, T), f32))
        args.append(wt)
    res = pl.pallas_call(
        functools.partial(_norm_proj_body, has_t=wt is not None),
        grid=(G, T // tm, N // tn),
        in_specs=in_specs, out_specs=out_specs, out_shape=out_shape,
        scratch_shapes=[pltpu.VMEM((tm, D), bf16)],
        compiler_params=_cparams(("parallel", "parallel", "arbitrary")),
        name="norm_proj_t" if wt is not None else "norm_proj",
    )(*args)
    return res if wt is not None else res[0]


def _norm_proj_layers_body(x_ref, sc_ref, sh_ref, nw_ref, w_ref, wt_ref, *rest):
    o_ref, cmp_ref, sel_ref, win_ref, h_scr = rest[-5:]
    j = pl.program_id(2)
    tn = o_ref.shape[1]

    @pl.when(j == 0)
    def _():
        h_scr[...] = _norm_mod(x_ref[...], nw_ref, sc_ref, sh_ref)
        for b, ref in enumerate((cmp_ref, sel_ref, win_ref)):
            kv = _dot_nt(wt_ref[b * KV_ROWS:(b + 1) * KV_ROWS, :], h_scr[...])
            if len(ref.shape) == 3:
                ref[0] = kv
                if ref.shape[0] > 1:
                    ref[1:] = jnp.zeros((ref.shape[0] - 1,) + kv.shape, f32)
            else:
                ref[...] = kv

    o_ref[...] = _dot(h_scr[...], w_ref[:, pl.ds(pl.multiple_of(j * tn, tn), tn)])


def norm_proj_layers(x, sc, sh, nw, w, wt, layer, depth, prev, *, tm, tn):
    G, T, D = x.shape
    N = w.shape[1]
    mod_spec = pl.BlockSpec((None, 1, D), lambda g, i, j: (g, 0, 0))
    in_specs = [pl.BlockSpec((None, tm, D), lambda g, i, j: (g, i, 0)), mod_spec, mod_spec,
                pl.BlockSpec((1, D), lambda g, i, j: (0, 0)), _resident(w), _resident(wt)]
    args = [x, sc, sh, nw.reshape(1, D), w, wt]
    aliases = {}
    if prev is not None:
        in_specs += [pl.BlockSpec(memory_space=pl.ANY)] * 2
        aliases = {len(args): 1, len(args) + 1: 2}
        args += list(prev)
    if prev is None:
        assert layer == 0
        layered = pl.BlockSpec((depth, None, KV_ROWS, tm), lambda g, i, j: (0, g, 0, i))
    else:
        layered = pl.BlockSpec((None, None, KV_ROWS, tm), lambda g, i, j: (layer, g, 0, i))
    return pl.pallas_call(
        _norm_proj_layers_body,
        grid=(G, T // tm, N // tn),
        in_specs=in_specs,
        out_specs=[pl.BlockSpec((None, tm, tn), lambda g, i, j: (g, i, j)), layered, layered,
                   pl.BlockSpec((None, KV_ROWS, tm), lambda g, i, j: (g, 0, i))],
        out_shape=[jax.ShapeDtypeStruct((G, T, N), f32), jax.ShapeDtypeStruct((depth, G, KV_ROWS, T), f32),
                   jax.ShapeDtypeStruct((depth, G, KV_ROWS, T), f32), jax.ShapeDtypeStruct((G, KV_ROWS, T), f32)],
        input_output_aliases=aliases,
        scratch_shapes=[pltpu.VMEM((tm, D), bf16)],
        compiler_params=_cparams(("parallel", "parallel", "arbitrary")),
        name="norm_proj_layers",
    )(*args)


CPP = LANE // CMP_STRIDE
assert CMP_BLOCK == 2 * CMP_STRIDE


def _compress_pages(get_page, n_pages, perm_ref, pos_ref, w1_ref, w2_ref, o_ref, xs_ref, acc_ref):
    n = o_ref.shape[0]
    perm = perm_ref[...]
    for k in range(n_pages):
        y = _dot_nt(perm, get_page(k).astype(bf16))
        for e in range(2):
            for s in range(CMP_STRIDE):
                xs_ref[e, s, k * CPP:(k + 1) * CPP, :] = y[s * CPP:(s + 1) * CPP, e * LANE:(e + 1) * LANE]
    for e in range(2):
        for s in range(CMP_STRIDE):
            xs_ref[e, s, n:n + 8, :] = jnp.zeros((8, LANE), f32)
            xs_ref[e, s, n + 1:n + 2, :] = pos_ref[e, s:s + 1, :]
            xs_ref[e, s, n + 2:n + 3, :] = pos_ref[e, CMP_STRIDE + s:CMP_STRIDE + s + 1, :]
    out = jnp.zeros((n, 2 * LANE), f32)
    for e in range(2):
        acc = jnp.zeros((n + 8, 2 * LANE), f32)
        for s in range(CMP_STRIDE):
            acc = acc + _dot(xs_ref[e, s].astype(bf16), w1_ref[e, s])
        acc_ref[...] = acc
        bias = acc_ref[n + 1:n + 2, 0:LANE] + acc_ref[n + 2:n + 3, LANE:2 * LANE]
        hid = acc_ref[0:n, 0:LANE] + acc_ref[pl.ds(1, n), LANE:2 * LANE] + bias
        out = out + _dot(jax.nn.gelu(hid).astype(bf16), w2_ref[e])
    o_ref[...] = out


def _cmp_weight_specs():
    z = lambda nd: (lambda *a: (0,) * nd)
    return [pl.BlockSpec((LANE, LANE), z(2)),
            pl.BlockSpec((2, CMP_BLOCK, LANE), z(3)),
            pl.BlockSpec((2, CMP_STRIDE, LANE, 2 * LANE), z(4)),
            pl.BlockSpec((2, LANE, 2 * LANE), z(3))]


def _cmp_scratch(n):
    return [pltpu.VMEM((2, CMP_STRIDE, n + 8, LANE), f32), pltpu.VMEM((n + 8, 2 * LANE), f32)]


def _compress_prompt_body(kvT_ref, perm_ref, pos_ref, w1_ref, w2_ref, o_ref, xs_ref, acc_ref):
    n_pages = kvT_ref.shape[1] // LANE
    get = lambda k: kvT_ref[:, k * LANE:(k + 1) * LANE]
    _compress_pages(get, n_pages, perm_ref, pos_ref, w1_ref, w2_ref, o_ref, xs_ref, acc_ref)


def compress_prompt(kvT, layer, cw):
    _, B, _, T = kvT.shape
    assert T % LANE == 0
    n = T // CMP_STRIDE
    return pl.pallas_call(
        _compress_prompt_body,
        grid=(B,),
        in_specs=[pl.BlockSpec((None, None, KV_ROWS, T), lambda b: (layer, b, 0, 0))] + _cmp_weight_specs(),
        out_specs=pl.BlockSpec((None, n, 2 * LANE), lambda b: (b, 0, 0)),
        out_shape=jax.ShapeDtypeStruct((B, n, 2 * LANE), f32),
        scratch_shapes=_cmp_scratch(n),
        compiler_params=_cparams(("parallel",)),
        name="compress_prompt",
    )(kvT, *cw)


def _page_specs(layer, n_pages, page):
    return [pl.BlockSpec((None, None, KV_ROWS, page), lambda b, pt, k=k: (layer, pt[b, k], 0, 0))
            for k in range(n_pages)]


def _compress_paged_body(pt_ref, perm_ref, pos_ref, w1_ref, w2_ref, *rest, n_pages):
    pages, (o_ref, xs_ref, acc_ref) = rest[:n_pages], rest[n_pages:]
    get = lambda k: pages[k][...]
    _compress_pages(get, n_pages, perm_ref, pos_ref, w1_ref, w2_ref, o_ref, xs_ref, acc_ref)


def compress_paged(cacheT, layer, page_table, cw):
    Bd, n_pages = page_table.shape
    page = cacheT.shape[-1]
    assert page == LANE
    L = n_pages * page
    n = L // CMP_STRIDE
    return pl.pallas_call(
        functools.partial(_compress_paged_body, n_pages=n_pages),
        grid_spec=pltpu.PrefetchScalarGridSpec(
            num_scalar_prefetch=1,
            grid=(Bd,),
            in_specs=_cmp_weight_specs() + _page_specs(layer, n_pages, page),
            out_specs=pl.BlockSpec((None, n, 2 * LANE), lambda b, pt: (b, 0, 0)),
            scratch_shapes=_cmp_scratch(n),
        ),
        out_shape=jax.ShapeDtypeStruct((Bd, n, 2 * LANE), f32),
        compiler_params=_cparams(("parallel",)),
        name="compress_paged",
    )(page_table, *cw, *([cacheT] * n_pages))


def _select_blocks(imp, s_ok, forced, n_sel, jj):
    k = min(N_SEL, n_sel)
    score = jnp.where(s_ok, imp + jnp.where(forced, FORCE_BONUS, 0.0), NEG)
    rank = jnp.zeros(score.shape, f32)
    for j2 in range(n_sel):
        col = score[:, j2:j2 + 1]
        ge = jnp.where(col >= score, 1.0, 0.0)
        gt = jnp.where(col > score, 1.0, 0.0)
        rank = rank + jnp.where(jj > j2, ge, gt)
    return jnp.where((rank < k) & s_ok, 1.0, 0.0)


def _select_blocks_t(imp, t0, n_sel):
    tq, sw = imp.shape
    rows = -(-n_sel // 8) * 8
    k = min(N_SEL, n_sel)
    jj = lax.broadcasted_iota(jnp.int32, (rows, tq), 0)
    qpos = t0 + lax.broadcasted_iota(jnp.int32, (rows, tq), 1)
    cur = jnp.right_shift(qpos, SEL_SHIFT)
    s_ok = (jj * SEL_BLOCK <= qpos) & (jj < n_sel)
    forced = (jj == 0) | (jj == cur) | (jj == cur - 1)
    imp_t = jnp.concatenate([imp[c * LANE:(c + 1) * LANE].T for c in range(tq // LANE)], axis=1)[:rows]
    score = jnp.where(s_ok, imp_t + jnp.where(forced, FORCE_BONUS, 0.0), NEG)
    rank = jnp.zeros(score.shape, f32)
    for j2 in range(n_sel):
        row = score[j2:j2 + 1, :]
        ge = jnp.where(row >= score, 1.0, 0.0)
        gt = jnp.where(row > score, 1.0, 0.0)
        rank = rank + jnp.where(jj > j2, ge, gt)
    sel_t = jnp.where((rank < k) & s_ok, 1.0, 0.0)
    sel_t = jnp.concatenate([sel_t, jnp.zeros((sw - rows, tq), f32)], axis=0)
    return jnp.concatenate([sel_t[:, c * LANE:(c + 1) * LANE].T for c in range(tq // LANE)], axis=0)


def _softmax_rows(sm, ok):
    m = jnp.max(sm, axis=-1, keepdims=True)
    e = jnp.where(ok, jnp.exp(sm - m), 0.0)
    den = jnp.sum(e, axis=-1, keepdims=True)
    return e / jnp.maximum(den, 1e-30)


def _pv_and_rowsum(e, vT):
    v_aug = jnp.concatenate([vT, jnp.ones(vT.shape, vT.dtype)], axis=0)
    return _dot_nt(e.astype(bf16), v_aug)


def _normalise(acc):
    return (acc / pltpu.roll(acc, HD, axis=1))[:, :HD]


def _nsa_prompt_body(q_ref, ag_ref, comp_ref, ksT_ref, vsT_ref, kwT_ref, vwT_ref, ov_ref, ex_ref, o_ref,
                     *, tq, n_cmp, n_sel, wk, ck):
    i = pl.program_id(1)
    t0 = i * tq
    scale = HD ** -0.5
    NC = comp_ref.shape[0]
    SW = ov_ref.shape[1]
    qpos = t0 + lax.broadcasted_iota(jnp.int32, (tq, 1), 0)
    col_n = lax.broadcasted_iota(jnp.int32, (tq, NC), 1)
    c_ok = ((col_n * CMP_STRIDE + (CMP_BLOCK - 1)) <= qpos) & (col_n < n_cmp)
    jj = lax.broadcasted_iota(jnp.int32, (tq, SW), 1)
    s_ok = (jj * SEL_BLOCK <= qpos) & (jj < n_sel)
    wstart = pl.multiple_of(jnp.maximum(t0 + tq - wk, 0), LANE)
    kposw = wstart + lax.broadcasted_iota(jnp.int32, (tq, wk), 1)
    rel = qpos - kposw
    w_bias4 = jnp.concatenate([jnp.where((rel >= 0) & (rel < WINDOW), 0.0, NEG)] * HPG, axis=0)
    c_ok4 = jnp.concatenate([c_ok] * HPG, axis=0)
    sg = jax.nn.sigmoid(ag_ref[...])
    n_chunks = (t0 + tq + ck - 1) // ck
    kcol = lax.broadcasted_iota(jnp.int32, (tq, ck), 1)
    all_valid_selected = t0 + tq <= min(N_SEL, n_sel) * SEL_BLOCK

    GS = range(NSA_KV_HEADS)
    rows = lambda g: slice(g * HD, (g + 1) * HD)
    kc = [comp_ref[:, g * LANE:g * LANE + HD].astype(bf16) for g in GS]
    vc = [comp_ref[:, g * LANE + HD:(g + 1) * LANE].astype(bf16) for g in GS]
    q4 = [jnp.concatenate([(q_ref[:, (g * HPG + h) * HD:(g * HPG + h + 1) * HD] * scale).astype(bf16)
                           for h in range(HPG)], axis=0) for g in GS]
    kw = [kwT_ref[rows(g), pl.ds(wstart, wk)].astype(bf16) for g in GS]
    vw = [vwT_ref[rows(g), pl.ds(wstart, wk)].astype(bf16) for g in GS]
    s_c = [_dot_nt(q4[g], kc[g]) for g in GS]
    s_w = [_dot(q4[g], kw[g]) + w_bias4 for g in GS]
    p_c = [_softmax_rows(jnp.where(c_ok4, s_c[g], NEG), c_ok4) for g in GS]
    e_w = [jnp.exp(s_w[g] - jnp.max(s_w[g], axis=-1, keepdims=True)) for g in GS]
    o_c = [_dot(p_c[g].astype(bf16), vc[g]) for g in GS]
    o_w = [_normalise(_pv_and_rowsum(e_w[g], vw[g])) for g in GS]
    imp = []
    for g in GS:
        psum = p_c[g][0:tq]
        for h in range(1, HPG):
            psum = psum + p_c[g][h * tq:(h + 1) * tq]
        imp.append(_split_dot(psum, ov_ref[...]))
    sel = lax.cond(all_valid_selected,
                   lambda: tuple(jnp.where(s_ok, 1.0, 0.0) for _ in GS),
                   lambda: tuple(_select_blocks_t(imp[g], t0, n_sel) for g in GS))
    selb = [s.astype(bf16) for s in sel]

    def chunk_step(c, carry):
        k0 = pl.multiple_of(c * ck, ck)
        causal = k0 + kcol <= qpos
        ex = ex_ref[:, pl.ds(k0, ck)]
        kT = [ksT_ref[rows(g), pl.ds(k0, ck)].astype(bf16) for g in GS]
        vT = [vsT_ref[rows(g), pl.ds(k0, ck)].astype(bf16) for g in GS]
        bias = [jnp.where((_dot(selb[g], ex) > 0.5) & causal, 0.0, NEG) for g in GS]
        s = [_dot(q4[g], kT[g]) + jnp.concatenate([bias[g]] * HPG, axis=0) for g in GS]
        m_new = [jnp.maximum(carry[g][0], jnp.max(s[g], axis=-1, keepdims=True)) for g in GS]
        e = [jnp.exp(s[g] - m_new[g]) for g in GS]
        pv = [_pv_and_rowsum(e[g], vT[g]) for g in GS]
        return tuple((m_new[g], jnp.exp(carry[g][0] - m_new[g]) * carry[g][1] + pv[g]) for g in GS)

    init = tuple((jnp.full((HPG * tq, 1), NEG, f32), jnp.zeros((HPG * tq, 2 * HD), f32)) for _ in GS)
    state = lax.fori_loop(0, n_chunks, chunk_step, init)

    for g in GS:
        gate = lambda br: jnp.concatenate(
            [sg[:, 3 * (g * HPG + h) + br:3 * (g * HPG + h) + br + 1] for h in range(HPG)], axis=0)
        o = gate(0) * o_c[g] + gate(1) * _normalise(state[g][1]) + gate(2) * o_w[g]
        for h in range(HPG):
            o_ref[:, (g * HPG + h) * HD:(g * HPG + h + 1) * HD] = o[h * tq:(h + 1) * tq]


def _overlap_matrix(nc_rows, n_cmp, sw_cols, n_sel):
    cs = jnp.arange(nc_rows) * CMP_STRIDE
    ss = jnp.arange(sw_cols) * SEL_BLOCK
    ok = (cs[:, None] < ss[None, :] + SEL_BLOCK) & (cs[:, None] + CMP_BLOCK > ss[None, :])
    ok = ok & (jnp.arange(nc_rows)[:, None] < n_cmp) & (jnp.arange(sw_cols)[None, :] < n_sel)
    return ok.astype(bf16)


def nsa_prompt(proj, selT, layer, winT, comp):
    B, T, _ = proj.shape
    tq = min(256, T)
    n_cmp = T // CMP_STRIDE - (CMP_BLOCK // CMP_STRIDE) + 1
    n_sel = -(-T // SEL_BLOCK)
    NC = comp.shape[1]
    SW = LANE
    assert n_sel <= SW and T % tq == 0
    wk = min(WINDOW + tq, T)
    ck = min(512, T)
    assert T % ck == 0
    ov = _overlap_matrix(NC, n_cmp, SW, n_sel)
    ex = (jnp.arange(SW)[:, None] == (jnp.arange(T)[None, :] // SEL_BLOCK)).astype(bf16)
    sel_spec = lambda blk: pl.BlockSpec((None, None, LANE, T), lambda b, i: (layer, b, blk, 0))
    win_spec = lambda blk: pl.BlockSpec((None, LANE, T), lambda b, i: (b, blk, 0))
    return pl.pallas_call(
        functools.partial(_nsa_prompt_body, tq=tq, n_cmp=n_cmp, n_sel=n_sel, wk=wk, ck=ck),
        grid=(B, T // tq),
        in_specs=[pl.BlockSpec((None, tq, A_Q), lambda b, i: (b, i, OFF_AQ // A_Q)),
                  pl.BlockSpec((None, tq, LANE), lambda b, i: (b, i, OFF_AG // LANE)),
                  pl.BlockSpec((None, NC, 2 * LANE), lambda b, i: (b, 0, 0)),
                  sel_spec(0), sel_spec(1), win_spec(0), win_spec(1),
                  pl.BlockSpec((NC, SW), lambda b, i: (0, 0)),
                  pl.BlockSpec((SW, T), lambda b, i: (0, 0))],
        out_specs=pl.BlockSpec((None, tq, A_Q), lambda b, i: (b, i, 0)),
        out_shape=jax.ShapeDtypeStruct((B, T, A_Q), f32),
        compiler_params=_cparams(("parallel", "arbitrary")),
        name="nsa_prompt",
    )(proj, proj, comp, selT, selT, winT, winT, ov, ex)


def _nsa_sample_body(pt_ref, q_ref, ag_ref, comp_ref, kwT_ref, newT_ref, ov_ref, rsum_ref, rexp_ref, ex_ref, *rest,
                     tn, past, n_cmp, n_sel, n_pages):
    RB = q_ref.shape[0]
    pages, o_ref = rest[:RB * n_pages], rest[RB * n_pages]
    scale = HD ** -0.5
    R = HPG * tn
    NC = comp_ref.shape[1]
    SW = ov_ref.shape[1]
    tok = lax.rem(lax.broadcasted_iota(jnp.int32, (R, 1), 0), tn)
    qpos = past + tok
    col_n = lax.broadcasted_iota(jnp.int32, (R, NC), 1)
    c_ok = ((col_n * CMP_STRIDE + (CMP_BLOCK - 1)) <= qpos) & (col_n < n_cmp)
    jj = lax.broadcasted_iota(jnp.int32, (tn, SW), 1)
    qpos_t = past + lax.broadcasted_iota(jnp.int32, (tn, 1), 0)
    cur = jnp.right_shift(qpos_t, SEL_SHIFT)
    s_ok = (jj * SEL_BLOCK <= qpos_t) & (jj < n_sel)
    forced = (jj == 0) | (jj == cur) | (jj == cur - 1)
    wlen = kwT_ref.shape[2]
    kposw = (past - wlen) + lax.broadcasted_iota(jnp.int32, (R, wlen), 1)
    relw = qpos - kposw
    w_ok = (relw >= 0) & (relw < WINDOW)
    nl = newT_ref.shape[2]
    u = lax.broadcasted_iota(jnp.int32, (R, nl), 1)
    reln = tok - u
    n_ok = (reln >= 0) & (reln < WINDOW) & (u < tn)
    nb = past // SEL_BLOCK
    chains = [(r, g) for r in range(RB) for g in range(NSA_KV_HEADS)]
    each = lambda f: [f(r, g) for r, g in chains]
    krow = lambda g: slice(g * HD, (g + 1) * HD)
    vrow = lambda g: slice(2 * HD + g * HD, 2 * HD + (g + 1) * HD)
    new = lambda r, branch, rows: newT_ref[r, branch * KV_ROWS + rows.start:branch * KV_ROWS + rows.stop, :]

    def masked_attention(qs, parts_of):
        s = [[jnp.where(ok, _dot(q, kT), NEG) for kT, _, ok in parts] for q, parts in zip(qs, parts_of)]
        m = [functools.reduce(jnp.maximum, [jnp.max(x, axis=-1, keepdims=True) for x in si]) for si in s]
        e = [[jnp.exp(x - mi) for x in si] for si, mi in zip(s, m)]
        den = [sum(jnp.sum(x, axis=-1, keepdims=True) for x in ei) for ei in e]
        return [sum(_dot_nt(x.astype(bf16), vT) for x, (_, vT, _) in zip(ei, parts)) / di
                for ei, parts, di in zip(e, parts_of, den)]

    q = each(lambda r, g: (q_ref[r, g * R:(g + 1) * R, :] * scale).astype(bf16))
    kc = each(lambda r, g: comp_ref[r, :, g * LANE:g * LANE + HD].astype(bf16))
    vc = each(lambda r, g: comp_ref[r, :, g * LANE + HD:(g + 1) * LANE].astype(bf16))
    n = range(len(chains))
    pc = [_softmax_rows(jnp.where(c_ok, _dot_nt(q[i], kc[i]), NEG), c_ok) for i in n]
    o_c = [_dot(pc[i].astype(bf16), vc[i]) for i in n]
    imp = [_split_dot(_rsum_exact(rsum_ref, pc[i]), ov_ref[...]) for i in n]
    sel = [_select_blocks(imp[i], s_ok, forced, n_sel, jj) for i in n]
    selr = [_dot(rexp_ref[...], sel[i].astype(bf16)) for i in n]
    o_w = masked_attention(q, [[(kwT_ref[r, krow(g), :].astype(bf16), kwT_ref[r, vrow(g), :].astype(bf16), w_ok),
                                (new(r, 2, krow(g)).astype(bf16), new(r, 2, vrow(g)).astype(bf16), n_ok)]
                               for r, g in chains])
    parts_of = []
    for i, (r, g) in enumerate(chains):
        pgs = pages[r * n_pages:(r + 1) * n_pages]
        ksp = jnp.concatenate([pg[krow(g), :] for pg in pgs], axis=1).astype(bf16)
        vsp = jnp.concatenate([pg[vrow(g), :] for pg in pgs], axis=1).astype(bf16)
        okp = _dot(selr[i].astype(bf16), ex_ref[...]) > 0.5
        okn = (u <= tok) & (u < tn) & (selr[i][:, nb:nb + 1] > 0.5)
        parts_of.append([(ksp, vsp, okp), (new(r, 1, krow(g)).astype(bf16), new(r, 1, vrow(g)).astype(bf16), okn)])
    o_s = masked_attention(q, parts_of)
    for i, (r, g) in enumerate(chains):
        gg = jax.nn.sigmoid(ag_ref[r, g * R:(g + 1) * R, :])
        o_ref[r, g * R:(g + 1) * R, :] = gg[:, 0:1] * o_c[i] + gg[:, 1:2] * o_s[i] + gg[:, 2:3] * o_w[i]


def _rsum_exact(rsum_ref, pc):
    r = rsum_ref[...]
    hi = pc.astype(bf16)
    lo = (pc - hi.astype(f32)).astype(bf16)
    return _dot(r, hi) + _dot(r, lo)


def nsa_sample(q_s, ag_s, comp, selT, layer, page_table, winT, newT, *, tn, past):
    Bd, n_pages = page_table.shape
    page = selT.shape[-1]
    R = HPG * tn
    L = past + tn
    n_cmp = L // CMP_STRIDE - (CMP_BLOCK // CMP_STRIDE) + 1
    n_sel = -(-L // SEL_BLOCK)
    NC = comp.shape[1]
    SW = -(-n_sel // LANE) * LANE
    assert past % page == 0 and page % SEL_BLOCK == 0 and tn <= SEL_BLOCK and n_cmp <= NC
    wlen = winT.shape[-1]
    ov = _overlap_matrix(NC, n_cmp, SW, n_sel)
    rsum = (jnp.arange(tn)[:, None] == (jnp.arange(R)[None, :] % tn)).astype(bf16)
    ex = (jnp.arange(SW)[:, None] == (jnp.arange(past)[None, :] // SEL_BLOCK)).astype(bf16)
    const = lambda *shape: pl.BlockSpec(shape, lambda b, pt: (0,) * len(shape))
    RB = 2 if Bd % 2 == 0 else 1
    page_specs = [pl.BlockSpec((None, None, KV_ROWS, page), lambda b, pt, r=r, k=k: (layer, pt[b * RB + r, k], 0, 0))
                  for r in range(RB) for k in range(n_pages)]
    return pl.pallas_call(
        functools.partial(_nsa_sample_body, tn=tn, past=past, n_cmp=n_cmp, n_sel=n_sel, n_pages=n_pages),
        grid_spec=pltpu.PrefetchScalarGridSpec(
            num_scalar_prefetch=1,
            grid=(Bd // RB,),
            in_specs=[pl.BlockSpec((RB, 2 * R, HD), lambda b, pt: (b, 0, 0)),
                      pl.BlockSpec((RB, 2 * R, 3), lambda b, pt: (b, 0, 0)),
                      pl.BlockSpec((RB, NC, 2 * LANE), lambda b, pt: (b, 0, 0)),
                      pl.BlockSpec((None, RB, KV_ROWS, wlen), lambda b, pt: (layer, b, 0, 0)),
                      pl.BlockSpec((RB, A_KV, LANE), lambda b, pt: (b, 0, 0)),
                      const(NC, SW), const(tn, R), const(R, tn), _resident(ex)]
            + page_specs,
            out_specs=pl.BlockSpec((RB, 2 * R, HD), lambda b, pt: (b, 0, 0)),
        ),
        out_shape=jax.ShapeDtypeStruct((Bd, 2 * R, HD), f32),
        compiler_params=_cparams(("parallel",)),
        name="nsa_sample",
    )(page_table, q_s, ag_s, comp, winT, newT, ov, rsum, rsum.T, ex, *([selT] * (RB * n_pages)))


def _retention_body(*refs):
    _retention_step(*refs[:10], *refs[-3:])


def _retention_mix_body(*refs):
    o_ref, sn_ref, s_scr, ro_scr = refs[-4:]
    _retention_step(*refs[:10], ro_scr, sn_ref, s_scr)
    rg_ref, ga_ref, gb_ref, attn_ref, x_ref, g1_ref, woa_ref, gnw_ref, wob_ref, wo_ref = refs[10:20]
    _mix_out_body(rg_ref, ga_ref, gb_ref, attn_ref, ro_scr, x_ref, g1_ref, woa_ref, gnw_ref, wob_ref, wo_ref, o_ref)


def _retention_step(q_ref, k_ref, v_ref, cos_ref, sin_ref, dm_ref, qd_ref, kd_ref, sd_ref, s0_ref, o_ref, sn_ref, s_scr):
    c = pl.program_id(1)

    @pl.when(c == 0)
    def _():
        s_scr[...] = s0_ref[...]

    C = dm_ref.shape[1]
    for cc in range(q_ref.shape[0] // C):
        r = slice(cc * C, (cc + 1) * C)
        cos = cos_ref[r, :]
        sin = sin_ref[r, :]

        def rot(x):
            return x * cos + pltpu.roll(x, RET_DK // 2, axis=1) * sin

        for h in range(RET_HEADS):
            q = rot(q_ref[r, h * RET_DK:(h + 1) * RET_DK])
            k = rot(k_ref[r, h * RET_DK:(h + 1) * RET_DK]) * (RET_DK ** -0.5)
            v = v_ref[r, h * RET_DV:(h + 1) * RET_DV].astype(bf16)
            S = s_scr[h]
            inner = _dot_nt(q.astype(bf16), k.astype(bf16)) * dm_ref[h]
            o_ref[r, h * RET_DV:(h + 1) * RET_DV] = (_dot(inner.astype(bf16), v)
                                                     + _dot((q * qd_ref[h]).astype(bf16), S.astype(bf16)))
            s_scr[h] = S * sd_ref[h] + _dot_tn((k * kd_ref[h]).astype(bf16), v)

    @pl.when(c == pl.num_programs(1) - 1)
    def _():
        if len(sn_ref.shape) == 4:
            sn_ref[0] = s_scr[...]
            if sn_ref.shape[0] > 1:
                sn_ref[1:] = jnp.zeros((sn_ref.shape[0] - 1,) + s_scr.shape, f32)
        else:
            sn_ref[...] = s_scr[...]


def retention(proj, pos0, states, layer, out_layer, depth, prev_new):
    B, T, _ = proj.shape
    R, in_specs, args = _retention_inputs(proj, pos0, states, layer)
    st, alias_spec = _layer_slot_spec(depth, out_layer, prev_new)
    aliases = {}
    if prev_new is not None:
        in_specs.append(alias_spec)
        aliases = {len(args): 1}
        args.append(prev_new)
    return pl.pallas_call(
        _retention_body,
        grid=(B, T // R),
        in_specs=in_specs,
        out_specs=[pl.BlockSpec((None, R, B_V), lambda b, c: (b, c, 0)), st],
        out_shape=[jax.ShapeDtypeStruct((B, T, B_V), f32),
                   jax.ShapeDtypeStruct((depth, B, RET_HEADS, RET_DK, RET_DV), f32)],
        input_output_aliases=aliases,
        scratch_shapes=[pltpu.VMEM((RET_HEADS, RET_DK, RET_DV), f32)],
        compiler_params=_cparams(("parallel", "arbitrary")),
        name="retention",
    )(*args)


def _retention_inputs(proj, pos0, states, layer):
    B, T, _ = proj.shape
    C = RET_CHUNK if T % RET_CHUNK == 0 else T
    n = T // C
    H = RET_HEADS
    half = RET_DK // 2
    inv = jnp.exp(-jnp.log(ROPE_BASE) * jnp.arange(half, dtype=f32) / half)
    ang = (pos0 + jnp.arange(T)).astype(f32)[:, None] * inv[None, :]
    cos = jnp.concatenate([jnp.cos(ang), jnp.cos(ang)], axis=-1)
    sin = jnp.concatenate([-jnp.sin(ang), jnp.sin(ang)], axis=-1)
    log_g = jnp.log(1.0 - jnp.exp2(-5.0 - jnp.arange(H, dtype=f32)))
    i = jnp.arange(C, dtype=f32)
    diff = i[:, None] - i[None, :]
    dm = jnp.where(diff >= 0, jnp.exp(jnp.maximum(diff, 0.0)[None] * log_g[:, None, None]), 0.0)
    qd = jnp.exp((i + 1.0)[None, :] * log_g[:, None])[..., None]
    kd = jnp.exp((C - 1.0 - i)[None, :] * log_g[:, None])[..., None]
    sd = jnp.exp(C * log_g)[:, None, None]
    tab = lambda a: pl.BlockSpec(a.shape, lambda b, c: (0, 0, 0))
    cps = 4 if n % 4 == 0 else 1
    R = cps * C
    in_specs = [pl.BlockSpec((None, R, B_QK), lambda b, c: (b, c, OFF_RQ // B_QK)),
                pl.BlockSpec((None, R, B_QK), lambda b, c: (b, c, OFF_RK // B_QK)),
                pl.BlockSpec((None, R, B_V), lambda b, c: (b, c, OFF_RV // B_V)),
                pl.BlockSpec((R, RET_DK), lambda b, c: (c, 0)),
                pl.BlockSpec((R, RET_DK), lambda b, c: (c, 0)),
                tab(dm), tab(qd), tab(kd), tab(sd),
                pl.BlockSpec((None, None, H, RET_DK, RET_DV), lambda b, c: (layer, b, 0, 0, 0))]
    return R, in_specs, [proj, proj, proj, cos, sin, dm, qd, kd, sd, states]


def _layer_slot_spec(depth, out_layer, prev_new):
    tail = (RET_HEADS, RET_DK, RET_DV)
    if prev_new is None:
        assert out_layer == 0
        return pl.BlockSpec((depth, None) + tail, lambda b, c: (0, b, 0, 0, 0)), None
    return (pl.BlockSpec((None, None) + tail, lambda b, c: (out_layer, b, 0, 0, 0)),
            pl.BlockSpec(memory_space=pl.ANY))


def retention_mix(proj, attn, x, g1, w_oa, gn_w, w_ob, w_o, pos0, states, layer, out_layer, depth, prev_new):
    B, T, D = x.shape
    R, in_specs, args = _retention_inputs(proj, pos0, states, layer)
    gn_w = gn_w.reshape(1, B_V)
    wide = lambda blk: pl.BlockSpec((None, R, B_V), lambda b, c: (b, c, blk))
    in_specs += [wide(OFF_RG // B_V), wide(OFF_GA // B_V), wide(OFF_GB // B_V),
                 pl.BlockSpec((None, R, A_Q), lambda b, c: (b, c, 0)),
                 pl.BlockSpec((None, R, D), lambda b, c: (b, c, 0)),
                 _mod_spec(g1, R, D), _resident(w_oa), _resident(gn_w), _resident(w_ob), _resident(w_o)]
    args += [proj, proj, proj, attn, x, g1, w_oa, gn_w, w_ob, w_o]
    st, alias_spec = _layer_slot_spec(depth, out_layer, prev_new)
    aliases = {}
    if prev_new is not None:
        in_specs.append(alias_spec)
        aliases = {len(args): 1}
        args.append(prev_new)
    return pl.pallas_call(
        _retention_mix_body,
        grid=(B, T // R),
        in_specs=in_specs,
        out_specs=[pl.BlockSpec((None, R, D), lambda b, c: (b, c, 0)), st],
        out_shape=[jax.ShapeDtypeStruct((B, T, D), f32),
                   jax.ShapeDtypeStruct((depth, B, RET_HEADS, RET_DK, RET_DV), f32)],
        input_output_aliases=aliases,
        scratch_shapes=[pltpu.VMEM((RET_HEADS, RET_DK, RET_DV), f32), pltpu.VMEM((R, B_V), f32)],
        compiler_params=_cparams(("parallel", "arbitrary")),
        name="retention_mix",
    )(*args)


def _mix_out_body(rg_ref, ga_ref, gb_ref, attn_ref, ro_ref, x_ref, g1_ref, woa_ref, gnw_ref, wob_ref, wo_ref, o_ref):
    ya = _dot(attn_ref[...].astype(bf16), woa_ref[...])
    parts = []
    for h in range(RET_HEADS):
        r = ro_ref[:, h * RET_DV:(h + 1) * RET_DV]
        d = r - jnp.mean(r, axis=-1, keepdims=True)
        var = jnp.mean(d * d, axis=-1, keepdims=True)
        parts.append(d * lax.rsqrt(var + GN_EPS) * gnw_ref[:, h * RET_DV:(h + 1) * RET_DV])
    ron = jnp.concatenate(parts, axis=-1)
    rg = rg_ref[...]
    yb = _dot((rg * jax.nn.sigmoid(rg) * ron).astype(bf16), wob_ref[...])
    m = jax.nn.sigmoid(ga_ref[...]) * ya + jax.nn.sigmoid(gb_ref[...]) * yb
    o_ref[...] = x_ref[...] + g1_ref[...] * _dot(m.astype(bf16), wo_ref[...])


def _mod_spec(mod, tm, D):
    if mod.shape[1] != 1:
        return pl.BlockSpec((None, tm, D), lambda g, i: (g, i, 0))
    return pl.BlockSpec((None, 1, D), lambda g, i: (g, 0, 0))


def mix_out(proj, attn, ro, x, g1, w_oa, gn_w, w_ob, w_o, *, tm):
    G, T, D = x.shape
    full = lambda a: pl.BlockSpec(a.shape, lambda g, i: (0,) * a.ndim)
    wide = lambda blk: pl.BlockSpec((None, tm, B_V), lambda g, i: (g, i, blk))
    gn_w = gn_w.reshape(1, B_V)
    return pl.pallas_call(
        _mix_out_body,
        grid=(G, T // tm),
        in_specs=[wide(OFF_RG // B_V), wide(OFF_GA // B_V), wide(OFF_GB // B_V),
                  pl.BlockSpec((None, tm, A_Q), lambda g, i: (g, i, 0)),
                  wide(0),
                  pl.BlockSpec((None, tm, D), lambda g, i: (g, i, 0)),
                  _mod_spec(g1, tm, D), full(w_oa), full(gn_w), full(w_ob), full(w_o)],
        out_specs=pl.BlockSpec((None, tm, D), lambda g, i: (g, i, 0)),
        out_shape=jax.ShapeDtypeStruct((G, T, D), f32),
        compiler_params=_cparams(("parallel", "parallel")),
        name="mix_out",
    )(proj, proj, proj, attn, ro, x, g1, w_oa, gn_w, w_ob, w_o)


def _ffn_tail(a, am1, am2, b_ref, cw_ref, cb_ref, wout_ref, x_ref, g2_ref, nf_ref, o_ref, final_norm):
    u = cb_ref[...] + am2 * cw_ref[0:1, :] + am1 * cw_ref[1:2, :] + a * cw_ref[2:3, :]
    y = _dot((jax.nn.gelu(u) * b_ref[...]).astype(bf16), wout_ref[...])
    xo = x_ref[...] + g2_ref[...] * y
    if final_norm:
        xo = xo * lax.rsqrt(jnp.mean(xo * xo, axis=-1, keepdims=True) + RMS_EPS) * nf_ref[...]
    o_ref[...] = xo


def _ffn_seq_body(x_ref, sc_ref, sh_ref, nw_ref, win_ref, prev_ref, cw_ref, cb_ref, wout_ref, g2_ref, nf_ref,
                  o_ref, conv_ref, scr, tail_scr, *, final_norm):
    tm = x_ref.shape[0]
    F = wout_ref.shape[0]
    fc = scr.shape[1]
    h = _norm_mod(x_ref[...], nw_ref, sc_ref, sh_ref)
    first = pl.program_id(1) == 0
    y = jnp.zeros(o_ref.shape, f32)
    for c0 in range(0, F, fc):
        cols = slice(c0, c0 + fc)
        a = _dot(h, win_ref[:, cols])
        scr[8:8 + tm, :] = a
        scr[6:8, :] = jnp.where(first, prev_ref[:, cols], tail_scr[:, cols])
        last2 = scr[pl.ds(tm + 6, 2), :]
        tail_scr[:, cols] = last2
        conv_ref[:, cols] = last2
        b = _dot(h, win_ref[:, F + c0:F + c0 + fc])
        u = (cb_ref[:, cols] + scr[pl.ds(6, tm), :] * cw_ref[0:1, cols] + scr[pl.ds(7, tm), :] * cw_ref[1:2, cols]
             + a * cw_ref[2:3, cols])
        y = y + _dot((jax.nn.gelu(u) * b).astype(bf16), wout_ref[cols, :])
    xo = x_ref[...] + g2_ref[...] * y
    if final_norm:
        xo = xo * lax.rsqrt(jnp.mean(xo * xo, axis=-1, keepdims=True) + RMS_EPS) * nf_ref[...]
    o_ref[...] = xo


def ffn_seq(x, sc, sh, nw, w_in, prev, g2, conv_w, conv_b, w_out, normf_w, *, tm, final_norm):
    B, T, D = x.shape
    F = w_out.shape[0]
    assert T % tm == 0 and tm >= CONV_W - 1
    fc = F
    full = lambda a: pl.BlockSpec(a.shape, lambda g, i: (0,) * a.ndim)
    conv_b = conv_b.reshape(1, F)
    normf_w = normf_w.reshape(1, D)
    nw = nw.reshape(1, D)
    rows = pl.BlockSpec((None, tm, D), lambda g, i: (g, i, 0))
    state = pl.BlockSpec((None, CONV_W - 1, F), lambda g, i: (g, 0, 0))
    return pl.pallas_call(
        functools.partial(_ffn_seq_body, final_norm=final_norm),
        grid=(B, T // tm),
        in_specs=[rows, _mod_spec(sc, tm, D), _mod_spec(sh, tm, D), full(nw), _resident(w_in), state,
                  full(conv_w), full(conv_b), _resident(w_out), _mod_spec(g2, tm, D), full(normf_w)],
        out_specs=[rows, state],
        out_shape=[jax.ShapeDtypeStruct((B, T, D), f32), jax.ShapeDtypeStruct((B, CONV_W - 1, F), f32)],
        scratch_shapes=[pltpu.VMEM((tm + 8, fc), f32), pltpu.VMEM((CONV_W - 1, F), f32)],
        compiler_params=_cparams(("parallel", "arbitrary")),
        name="ffn_seq",
    )(x, sc, sh, nw, w_in, prev, conv_w, conv_b, w_out, g2, normf_w)


def _ffn_out_rows_body(a_ref, am1_ref, am2_ref, b_ref, cw_ref, cb_ref, wout_ref, x_ref, g2_ref, nf_ref, o_ref,
                       *, final_norm):
    _ffn_tail(a_ref[...], am1_ref[...], am2_ref[...], b_ref, cw_ref, cb_ref, wout_ref, x_ref, g2_ref, nf_ref, o_ref,
              final_norm)


def ffn_out_rows(ab, am1, am2, x, g2, conv_w, conv_b, w_out, normf_w, *, tm, final_norm):
    G, T, D = x.shape
    F = w_out.shape[0]
    full = lambda a: pl.BlockSpec(a.shape, lambda g, i: (0,) * a.ndim)
    conv_b = conv_b.reshape(1, F)
    normf_w = normf_w.reshape(1, D)
    rowsF = lambda blk: pl.BlockSpec((None, tm, F), lambda g, i: (g, i, blk))
    return pl.pallas_call(
        functools.partial(_ffn_out_rows_body, final_norm=final_norm),
        grid=(G, T // tm),
        in_specs=[rowsF(0), rowsF(0), rowsF(0), rowsF(1),
                  full(conv_w), full(conv_b), full(w_out),
                  pl.BlockSpec((None, tm, D), lambda g, i: (g, i, 0)),
                  _mod_spec(g2, tm, D), full(normf_w)],
        out_specs=pl.BlockSpec((None, tm, D), lambda g, i: (g, i, 0)),
        out_shape=jax.ShapeDtypeStruct((G, T, D), f32),
        compiler_params=_cparams(("parallel", "parallel")),
        name="ffn_out_rows",
    )(ab, am1, am2, ab, conv_w, conv_b, w_out, x, g2, normf_w)


def _prep_w_in(w):
    D = w.shape[0]
    o = 0
    parts = {}
    for name, n in (("aq", A_Q), ("akv", A_KV), ("ag", A_G), ("rq", B_QK), ("rk", B_QK), ("rv", B_V), ("rg", B_V),
                    ("ga", D), ("gb", D)):
        parts[name] = w[:, o:o + n]
        o += n
    pad = jnp.zeros((D, N_TOK - OFF_AG - A_G), w.dtype)
    tok = jnp.concatenate([parts["rg"], parts["ga"], parts["gb"], parts["rv"], parts["aq"], parts["rq"], parts["rk"],
                           parts["ag"], pad], axis=1)
    return tok.astype(bf16), parts["akv"].T.astype(bf16)


def _prep_cmp(cmp_pos, cmp_w1, cmp_w2):
    G = NSA_KV_HEADS
    r = jnp.arange(LANE)
    perm = ((r % CPP)[:, None] * CMP_STRIDE + (r // CPP)[:, None] == r[None, :]).astype(bf16)
    pos = jnp.concatenate([cmp_pos] * G, axis=-1)
    w1 = cmp_w1.reshape(2, CMP_BLOCK, HD, CMP_HIDDEN)
    eye = jnp.eye(G, dtype=w1.dtype)
    w1bd = jnp.einsum("eldh,gk->elgdkh", w1, eye).reshape(2, CMP_BLOCK, G * HD, G * CMP_HIDDEN)
    w1cat = jnp.concatenate([w1bd[:, :CMP_STRIDE], w1bd[:, CMP_STRIDE:]], axis=-1)
    eye_e = jnp.eye(2, dtype=w1.dtype)
    w2bd = jnp.einsum("ehd,gk,ef->eghkfd", cmp_w2, eye, eye_e).reshape(2, G * CMP_HIDDEN, G * 2 * HD)
    return perm, pos, w1cat.astype(bf16), w2bd.astype(bf16)


def _kv_rows_to_out(rowsT, lead):
    t = rowsT.shape[-1]
    r = rowsT.reshape(*lead, 2, NSA_KV_HEADS, HD, t)
    n = len(lead)
    return r.transpose(*range(n), n + 3, n, n + 1, n + 2)


def _keep_last_lanes(a, n):
    t = a.shape[-1]
    if t >= n:
        return a[..., t - n:]
    return jnp.pad(a, ((0, 0),) * (a.ndim - 1) + ((n - t, 0),))


def kernel(x_prompt, x_sample, c_prompt, c_sample, cache_cmp_kv, cache_sel_kv, state_win_kv, state_ret, state_conv,
           page_table, norm1_w, ada_w, ada_b, w_in, cmp_pos, cmp_w1, cmp_w2, w_oa, ret_gn_w, w_ob, w_o, norm2_w,
           ffn_w_in, ffn_conv_w, ffn_conv_b, ffn_w_out, normf_w):
    B, T, D = x_prompt.shape
    Bd, Td, _ = x_sample.shape
    depth = w_in.shape[0]
    n_phys, page = cache_cmp_kv.shape[1], cache_cmp_kv.shape[2]
    past = page_table.shape[1] * page
    wlen = state_win_kv.shape[2]
    F = ffn_w_out.shape[1]
    Rs = Bd * Td
    tm_p = min(512, T)

    mod = ada_mod(jnp.concatenate([c_prompt, c_sample], axis=0), ada_w, ada_b)
    cmpT = cache_cmp_kv.transpose(0, 1, 3, 4, 5, 2).reshape(depth, n_phys, KV_ROWS, page)
    selT = cache_sel_kv.transpose(0, 1, 3, 4, 5, 2).reshape(depth, n_phys, KV_ROWS, page)
    winT = state_win_kv.transpose(0, 1, 3, 4, 5, 2).reshape(depth, Bd, KV_ROWS, wlen)

    w_in_all, w_oa_all, w_ob_all, w_o_all, ffn_in_all, ffn_out_all = (
        w.astype(bf16) for w in (w_in, w_oa, w_ob, w_o, ffn_w_in, ffn_w_out))
    ret_zero = jnp.zeros((1, B, RET_HEADS, RET_DK, RET_DV), f32)

    xp = x_prompt
    xs = x_sample.reshape(1, Rs, D)
    kv_p = None
    ret_p = ret_s = None
    outs = [[] for _ in range(10)]
    for l in range(depth):
        w_tok, w_kvT = _prep_w_in(w_in_all[l])
        cw = _prep_cmp(cmp_pos[l], cmp_w1[l], cmp_w2[l])
        w_oa_b, w_ob_b, w_o_b = w_oa_all[l], w_ob_all[l], w_o_all[l]
        ffn_in_b, ffn_out_b = ffn_in_all[l], ffn_out_all[l]
        last = l == depth - 1
        mp = [mod[l, :B, k * D:(k + 1) * D].reshape(B, 1, D) for k in range(6)]
        ms = [jnp.repeat(mod[l, B:, k * D:(k + 1) * D], Td, axis=0).reshape(1, Rs, D) for k in range(6)]

        proj, cmp_p, sel_p, win_p = norm_proj_layers(xp, mp[1], mp[0], norm1_w[l], w_tok, w_kvT, l, depth, kv_p,
                                                     tm=tm_p, tn=PROJ_TN)
        kv_p = (cmp_p, sel_p)
        comp = compress_prompt(cmp_p, l, cw)
        attn = nsa_prompt(proj, sel_p, l, win_p, comp)
        x1, ret_p = retention_mix(proj, attn, xp, mp[2], w_oa_b, ret_gn_w[l], w_ob_b, w_o_b, 0, ret_zero, 0, l, depth,
                                  ret_p)
        xp, conv_p = ffn_seq(x1, mp[4], mp[3], norm2_w[l], ffn_in_b, jnp.zeros((B, CONV_W - 1, F), f32), mp[5],
                             ffn_conv_w[l], ffn_conv_b[l], ffn_out_b, normf_w, tm=min(256, T), final_norm=last)
        outs[4].append(_kv_rows_to_out(_keep_last_lanes(win_p, wlen), (B,)))
        outs[8].append(conv_p)

        proj_s, kvT_s = norm_proj(xs, ms[1], ms[0], norm1_w[l], w_tok, w_kvT, tm=Rs, tn=PROJ_TN)
        comp_s = compress_paged(cmpT, l, page_table, cw)
        q_s = proj_s[0, :, OFF_AQ:OFF_AQ + A_Q].reshape(Bd, Td, NSA_KV_HEADS, HPG, HD)
        q_s = q_s.transpose(0, 2, 3, 1, 4).reshape(Bd, NSA_HEADS * Td, HD)
        ag_s = proj_s[0, :, OFF_AG:OFF_AG + A_G].reshape(Bd, Td, NSA_KV_HEADS, HPG, 3)
        ag_s = ag_s.transpose(0, 2, 3, 1, 4).reshape(Bd, NSA_HEADS * Td, 3)
        newT = kvT_s[0].reshape(A_KV, Bd, Td).transpose(1, 0, 2)
        newT_pad = jnp.pad(newT, ((0, 0), (0, 0), (0, LANE - Td)))
        attn_s = nsa_sample(q_s, ag_s, comp_s, selT, l, page_table, winT, newT_pad, tn=Td, past=past)
        attn_s = attn_s.reshape(Bd, NSA_KV_HEADS, HPG, Td, HD).transpose(0, 3, 1, 2, 4).reshape(1, Rs, A_Q)
        ro_s, ret_s = retention(proj_s.reshape(Bd, Td, N_TOK), past, state_ret, l, l, depth, ret_s)
        x1s = mix_out(proj_s, attn_s, ro_s.reshape(1, Rs, B_V), xs, ms[2], w_oa_b, ret_gn_w[l], w_ob_b, w_o_b, tm=Rs)
        ab_s = norm_proj(x1s, ms[4], ms[3], norm2_w[l], ffn_in_b, None, tm=Rs, tn=F)
        a_ext = jnp.concatenate([state_conv[l], ab_s[0, :, :F].reshape(Bd, Td, F)], axis=1)
        am1 = a_ext[:, 1:1 + Td].reshape(1, Rs, F)
        am2 = a_ext[:, 0:Td].reshape(1, Rs, F)
        xs = ffn_out_rows(ab_s, am1, am2, x1s, ms[5], ffn_conv_w[l], ffn_conv_b[l], ffn_out_b, normf_w, tm=Rs,
                          final_norm=last)
        outs[1].append(_kv_rows_to_out(newT[:, 0:KV_ROWS], (Bd,)))
        outs[3].append(_kv_rows_to_out(newT[:, KV_ROWS:2 * KV_ROWS], (Bd,)))
        win_all = jnp.concatenate([winT[l], newT[:, 2 * KV_ROWS:]], axis=-1)
        outs[5].append(_kv_rows_to_out(_keep_last_lanes(win_all, wlen), (Bd,)))
        outs[9].append(a_ext[:, Td:])

    st = lambda k: jnp.stack(outs[k])
    new_cmp_p = _kv_rows_to_out(kv_p[0], (depth, B))
    new_sel_p = _kv_rows_to_out(kv_p[1], (depth, B))
    return (xp, xs.reshape(Bd, Td, D), new_cmp_p, st(1), new_sel_p, st(3), st(4), st(5), ret_p, ret_s, st(8), st(9))
```

```python
import functools

import jax
import jax.numpy as jnp
from jax import lax
from jax.experimental import pallas as pl
from jax.experimental.pallas import tpu as pltpu

f32 = jnp.float32
bf16 = jnp.bfloat16

NSA_HEADS = 8
NSA_KV_HEADS = 2
HPG = NSA_HEADS // NSA_KV_HEADS
HD = 64
CMP_STRIDE = 16
CMP_BLOCK = 32
CMP_HIDDEN = 64
SEL_BLOCK = 64
SEL_SHIFT = 6
N_SEL = 16
WINDOW = 512
RET_HEADS = 4
RET_DK = 128
RET_DV = 256
RET_CHUNK = 128
ROPE_BASE = 10000.0
CONV_W = 3
RMS_EPS = 1e-6
GN_EPS = 1e-5
NEG = -1e30
FORCE_BONUS = 1e4

A_Q = NSA_HEADS * HD
KV_ROWS = 2 * NSA_KV_HEADS * HD
A_KV = 3 * KV_ROWS
A_G = 3 * NSA_HEADS
B_QK = RET_HEADS * RET_DK
B_V = RET_HEADS * RET_DV

OFF_RG, OFF_GA, OFF_GB, OFF_RV = 0, 1024, 2048, 3072
OFF_AQ, OFF_RQ, OFF_RK, OFF_AG = 4096, 4608, 5120, 5632
N_TOK = 5760
PROJ_TN = 1920

LANE = 128
VMEM_LIMIT = 56 * 1024 * 1024


def _cparams(sem):
    return pltpu.CompilerParams(dimension_semantics=sem, vmem_limit_bytes=VMEM_LIMIT)


def _dot(a, b):
    return jnp.dot(a, b, preferred_element_type=f32)


def _dot_nt(a, b):
    return lax.dot_general(a, b, (((1,), (1,)), ((), ())), preferred_element_type=f32)


def _dot_tn(a, b):
    return lax.dot_general(a, b, (((0,), (0,)), ((), ())), preferred_element_type=f32)


def _split_dot(a, b_bf16):
    hi = a.astype(bf16)
    lo = (a - hi.astype(f32)).astype(bf16)
    return _dot(hi, b_bf16) + _dot(lo, b_bf16)


def _ada_body(c_ref, w_ref, b_ref, o_ref):
    c = c_ref[...]
    s = c * jax.nn.sigmoid(c)
    o_ref[...] = _dot(s.astype(bf16), w_ref[...].astype(bf16)) + b_ref[...]


def ada_mod(c_all, ada_w, ada_b):
    depth, d, n = ada_w.shape
    r = c_all.shape[0]
    tn = 1024
    return pl.pallas_call(
        _ada_body,
        grid=(depth, n // tn),
        in_specs=[pl.BlockSpec((r, d), lambda l, j: (0, 0)),
                  pl.BlockSpec((None, d, tn), lambda l, j: (l, 0, j)),
                  pl.BlockSpec((None, 1, tn), lambda l, j: (l, 0, j))],
        out_specs=pl.BlockSpec((None, r, tn), lambda l, j: (l, 0, j)),
        out_shape=jax.ShapeDtypeStruct((depth, r, n), f32),
        compiler_params=_cparams(("parallel", "parallel")),
        name="ada_mod",
    )(c_all, ada_w, ada_b.reshape(depth, 1, n))


def _norm_proj_body(x_ref, sc_ref, sh_ref, nw_ref, w_ref, *rest, has_t):
    if has_t:
        wt_ref, o_ref, ot_ref, h_scr = rest
    else:
        o_ref, h_scr = rest

    j = pl.program_id(2)
    tn = o_ref.shape[1]

    @pl.when(j == 0)
    def _():
        h_scr[...] = _norm_mod(x_ref[...], nw_ref, sc_ref, sh_ref)
        if has_t:
            ot_ref[...] = _dot_nt(wt_ref[...], h_scr[...])

    o_ref[...] = _dot(h_scr[...], w_ref[:, pl.ds(pl.multiple_of(j * tn, tn), tn)])


def _norm_mod(x, nw_ref, sc_ref, sh_ref):
    y = x * lax.rsqrt(jnp.mean(x * x, axis=-1, keepdims=True) + RMS_EPS) * nw_ref[...]
    return (y * (1.0 + sc_ref[...]) + sh_ref[...]).astype(bf16)


def _resident(a):
    return pl.BlockSpec(a.shape, lambda *_: (0,) * a.ndim, pipeline_mode=pl.Buffered(1))


def norm_proj(x, sc, sh, nw, w, wt, *, tm, tn):
    G, T, D = x.shape
    N = w.shape[1]
    per_row = sc.shape[1] != 1
    mr = tm if per_row else 1
    mod_spec = pl.BlockSpec((None, mr, D), (lambda g, i, j: (g, i, 0)) if per_row else (lambda g, i, j: (g, 0, 0)))
    in_specs = [pl.BlockSpec((None, tm, D), lambda g, i, j: (g, i, 0)), mod_spec, mod_spec,
                pl.BlockSpec((1, D), lambda g, i, j: (0, 0)),
                _resident(w)]
    out_specs = [pl.BlockSpec((None, tm, tn), lambda g, i, j: (g, i, j))]
    out_shape = [jax.ShapeDtypeStruct((G, T, N), f32)]
    args = [x, sc, sh, nw.reshape(1, D), w]
    if wt is not None:
        NT = wt.shape[0]
        in_specs.append(_resident(wt))
        out_specs.append(pl.BlockSpec((None, NT, tm), lambda g, i, j: (g, 0, i)))
        out_shape.append(jax.ShapeDtypeStruct((G, NT, T), f32))
        args.append(wt)
    res = pl.pallas_call(
        functools.partial(_norm_proj_body, has_t=wt is not None),
        grid=(G, T // tm, N // tn),
        in_specs=in_specs, out_specs=out_specs, out_shape=out_shape,
        scratch_shapes=[pltpu.VMEM((tm, D), bf16)],
        compiler_params=_cparams(("parallel", "parallel", "arbitrary")),
        name="norm_proj_t" if wt is not None else "norm_proj",
    )(*args)
    return res if wt is not None else res[0]


def _norm_proj_layers_body(x_ref, sc_ref, sh_ref, nw_ref, w_ref, wt_ref, *rest):
    o_ref, cmp_ref, sel_ref, win_ref, h_scr = rest[-5:]
    j = pl.program_id(2)
    tn = o_ref.shape[1]

    @pl.when(j == 0)
    def _():
        h_scr[...] = _norm_mod(x_ref[...], nw_ref, sc_ref, sh_ref)
        for b, ref in enumerate((cmp_ref, sel_ref, win_ref)):
            kv = _dot_nt(wt_ref[b * KV_ROWS:(b + 1) * KV_ROWS, :], h_scr[...])
            if len(ref.shape) == 3:
                ref[0] = kv
                if ref.shape[0] > 1:
                    ref[1:] = jnp.zeros((ref.shape[0] - 1,) + kv.shape, f32)
            else:
                ref[...] = kv

    o_ref[...] = _dot(h_scr[...], w_ref[:, pl.ds(pl.multiple_of(j * tn, tn), tn)])


def norm_proj_layers(x, sc, sh, nw, w, wt, layer, depth, prev, *, tm, tn):
    G, T, D = x.shape
    N = w.shape[1]
    mod_spec = pl.BlockSpec((None, 1, D), lambda g, i, j: (g, 0, 0))
    in_specs = [pl.BlockSpec((None, tm, D), lambda g, i, j: (g, i, 0)), mod_spec, mod_spec,
                pl.BlockSpec((1, D), lambda g, i, j: (0, 0)), _resident(w), _resident(wt)]
    args = [x, sc, sh, nw.reshape(1, D), w, wt]
    aliases = {}
    if prev is not None:
        in_specs += [pl.BlockSpec(memory_space=pl.ANY)] * 2
        aliases = {len(args): 1, len(args) + 1: 2}
        args += list(prev)
    if prev is None:
        assert layer == 0
        layered = pl.BlockSpec((depth, None, KV_ROWS, tm), lambda g, i, j: (0, g, 0, i))
    else:
        layered = pl.BlockSpec((None, None, KV_ROWS, tm), lambda g, i, j: (layer, g, 0, i))
    return pl.pallas_call(
        _norm_proj_layers_body,
        grid=(G, T // tm, N // tn),
        in_specs=in_specs,
        out_specs=[pl.BlockSpec((None, tm, tn), lambda g, i, j: (g, i, j)), layered, layered,
                   pl.BlockSpec((None, KV_ROWS, tm), lambda g, i, j: (g, 0, i))],
        out_shape=[jax.ShapeDtypeStruct((G, T, N), f32), jax.ShapeDtypeStruct((depth, G, KV_ROWS, T), f32),
                   jax.ShapeDtypeStruct((depth, G, KV_ROWS, T), f32), jax.ShapeDtypeStruct((G, KV_ROWS, T), f32)],
        input_output_aliases=aliases,
        scratch_shapes=[pltpu.VMEM((tm, D), bf16)],
        compiler_params=_cparams(("parallel", "parallel", "arbitrary")),
        name="norm_proj_layers",
    )(*args)


CPP = LANE // CMP_STRIDE
assert CMP_BLOCK == 2 * CMP_STRIDE


def _compress_pages(get_page, n_pages, perm_ref, pos_ref, w1_ref, w2_ref, o_ref, xs_ref, acc_ref):
    n = o_ref.shape[0]
    perm = perm_ref[...]
    for k in range(n_pages):
        y = _dot_nt(perm, get_page(k).astype(bf16))
        for e in range(2):
            for s in range(CMP_STRIDE):
                xs_ref[e, s, k * CPP:(k + 1) * CPP, :] = y[s * CPP:(s + 1) * CPP, e * LANE:(e + 1) * LANE]
    for e in range(2):
        for s in range(CMP_STRIDE):
            xs_ref[e, s, n:n + 8, :] = jnp.zeros((8, LANE), f32)
            xs_ref[e, s, n + 1:n + 2, :] = pos_ref[e, s:s + 1, :]
            xs_ref[e, s, n + 2:n + 3, :] = pos_ref[e, CMP_STRIDE + s:CMP_STRIDE + s + 1, :]
    out = jnp.zeros((n, 2 * LANE), f32)
    for e in range(2):
        acc = jnp.zeros((n + 8, 2 * LANE), f32)
        for j in range(CMP_STRIDE // 2):
            lhs = jnp.concatenate([xs_ref[e, 2 * j], xs_ref[e, 2 * j + 1]], axis=1).astype(bf16)
            acc = acc + _dot(lhs, w1_ref[e, j])
        acc_ref[...] = acc
        bias = acc_ref[n + 1:n + 2, 0:LANE] + acc_ref[n + 2:n + 3, LANE:2 * LANE]
        hid = acc_ref[0:n, 0:LANE] + acc_ref[pl.ds(1, n), LANE:2 * LANE] + bias
        out = out + _dot(jax.nn.gelu(hid).astype(bf16), w2_ref[e])
    o_ref[...] = out


def _cmp_weight_specs():
    z = lambda nd: (lambda *a: (0,) * nd)
    return [pl.BlockSpec((LANE, LANE), z(2)),
            pl.BlockSpec((2, CMP_BLOCK, LANE), z(3)),
            pl.BlockSpec((2, CMP_STRIDE // 2, 2 * LANE, 2 * LANE), z(4)),
            pl.BlockSpec((2, LANE, 2 * LANE), z(3))]


def _cmp_scratch(n):
    return [pltpu.VMEM((2, CMP_STRIDE, n + 8, LANE), f32), pltpu.VMEM((n + 8, 2 * LANE), f32)]


def _compress_prompt_body(kvT_ref, perm_ref, pos_ref, w1_ref, w2_ref, o_ref, xs_ref, acc_ref):
    n_pages = kvT_ref.shape[1] // LANE
    get = lambda k: kvT_ref[:, k * LANE:(k + 1) * LANE]
    _compress_pages(get, n_pages, perm_ref, pos_ref, w1_ref, w2_ref, o_ref, xs_ref, acc_ref)


def compress_prompt(kvT, layer, cw):
    _, B, _, T = kvT.shape
    assert T % LANE == 0
    n = T // CMP_STRIDE
    return pl.pallas_call(
        _compress_prompt_body,
        grid=(B,),
        in_specs=[pl.BlockSpec((None, None, KV_ROWS, T), lambda b: (layer, b, 0, 0))] + _cmp_weight_specs(),
        out_specs=pl.BlockSpec((None, n, 2 * LANE), lambda b: (b, 0, 0)),
        out_shape=jax.ShapeDtypeStruct((B, n, 2 * LANE), f32),
        scratch_shapes=_cmp_scratch(n),
        compiler_params=_cparams(("parallel",)),
        name="compress_prompt",
    )(kvT, *cw)


def _page_specs(layer, n_pages, page):
    return [pl.BlockSpec((None, None, KV_ROWS, page), lambda b, pt, k=k: (layer, pt[b, k], 0, 0))
            for k in range(n_pages)]


def _compress_paged_body(pt_ref, perm_ref, pos_ref, w1_ref, w2_ref, *rest, n_pages):
    pages, (o_ref, xs_ref, acc_ref) = rest[:n_pages], rest[n_pages:]
    get = lambda k: pages[k][...]
    _compress_pages(get, n_pages, perm_ref, pos_ref, w1_ref, w2_ref, o_ref, xs_ref, acc_ref)


def compress_paged(cacheT, layer, page_table, cw):
    Bd, n_pages = page_table.shape
    page = cacheT.shape[-1]
    assert page == LANE
    L = n_pages * page
    n = L // CMP_STRIDE
    return pl.pallas_call(
        functools.partial(_compress_paged_body, n_pages=n_pages),
        grid_spec=pltpu.PrefetchScalarGridSpec(
            num_scalar_prefetch=1,
            grid=(Bd,),
            in_specs=_cmp_weight_specs() + _page_specs(layer, n_pages, page),
            out_specs=pl.BlockSpec((None, n, 2 * LANE), lambda b, pt: (b, 0, 0)),
            scratch_shapes=_cmp_scratch(n),
        ),
        out_shape=jax.ShapeDtypeStruct((Bd, n, 2 * LANE), f32),
        compiler_params=_cparams(("parallel",)),
        name="compress_paged",
    )(page_table, *cw, *([cacheT] * n_pages))


def _select_blocks(imp, s_ok, forced, n_sel, jj):
    k = min(N_SEL, n_sel)
    score = jnp.where(s_ok, imp + jnp.where(forced, FORCE_BONUS, 0.0), NEG)
    rank = jnp.zeros(score.shape, f32)
    for j2 in range(n_sel):
        col = score[:, j2:j2 + 1]
        ge = jnp.where(col >= score, 1.0, 0.0)
        gt = jnp.where(col > score, 1.0, 0.0)
        rank = rank + jnp.where(jj > j2, ge, gt)
    return jnp.where((rank < k) & s_ok, 1.0, 0.0)


def _select_blocks_t(imp, t0, n_sel):
    tq, sw = imp.shape
    rows = -(-n_sel // 8) * 8
    k = min(N_SEL, n_sel)
    jj = lax.broadcasted_iota(jnp.int32, (rows, tq), 0)
    qpos = t0 + lax.broadcasted_iota(jnp.int32, (rows, tq), 1)
    cur = jnp.right_shift(qpos, SEL_SHIFT)
    s_ok = (jj * SEL_BLOCK <= qpos) & (jj < n_sel)
    forced = (jj == 0) | (jj == cur) | (jj == cur - 1)
    imp_t = jnp.concatenate([imp[c * LANE:(c + 1) * LANE].T for c in range(tq // LANE)], axis=1)[:rows]
    score = jnp.where(s_ok, imp_t + jnp.where(forced, FORCE_BONUS, 0.0), NEG)
    rank = jnp.zeros(score.shape, f32)
    for j2 in range(n_sel):
        row = score[j2:j2 + 1, :]
        ge = jnp.where(row >= score, 1.0, 0.0)
        gt = jnp.where(row > score, 1.0, 0.0)
        rank = rank + jnp.where(jj > j2, ge, gt)
    sel_t = jnp.where((rank < k) & s_ok, 1.0, 0.0)
    sel_t = jnp.concatenate([sel_t, jnp.zeros((sw - rows, tq), f32)], axis=0)
    return jnp.concatenate([sel_t[:, c * LANE:(c + 1) * LANE].T for c in range(tq // LANE)], axis=0)


def _softmax_rows(sm, ok):
    m = jnp.max(sm, axis=-1, keepdims=True)
    e = jnp.where(ok, jnp.exp(sm - m), 0.0)
    den = jnp.sum(e, axis=-1, keepdims=True)
    return e / jnp.maximum(den, 1e-30)


def _pv_and_rowsum(e, vT):
    v_aug = jnp.concatenate([vT, jnp.ones(vT.shape, vT.dtype)], axis=0)
    return _dot_nt(e.astype(bf16), v_aug)


def _normalise(acc):
    return (acc / pltpu.roll(acc, HD, axis=1))[:, :HD]


def _nsa_prompt_body(q_ref, ag_ref, comp_ref, ksT_ref, vsT_ref, kwT_ref, vwT_ref, ov_ref, ex_ref, o_ref,
                     *, tq, n_cmp, n_sel, wk, ck):
    i = pl.program_id(1)
    t0 = i * tq
    scale = HD ** -0.5
    NC = comp_ref.shape[0]
    SW = ov_ref.shape[1]
    qpos = t0 + lax.broadcasted_iota(jnp.int32, (tq, 1), 0)
    col_n = lax.broadcasted_iota(jnp.int32, (tq, NC), 1)
    c_ok = ((col_n * CMP_STRIDE + (CMP_BLOCK - 1)) <= qpos) & (col_n < n_cmp)
    jj = lax.broadcasted_iota(jnp.int32, (tq, SW), 1)
    s_ok = (jj * SEL_BLOCK <= qpos) & (jj < n_sel)
    wstart = pl.multiple_of(jnp.maximum(t0 + tq - wk, 0), LANE)
    kposw = wstart + lax.broadcasted_iota(jnp.int32, (tq, wk), 1)
    rel = qpos - kposw
    w_bias4 = jnp.concatenate([jnp.where((rel >= 0) & (rel < WINDOW), 0.0, NEG)] * HPG, axis=0)
    c_ok4 = jnp.concatenate([c_ok] * HPG, axis=0)
    sg = jax.nn.sigmoid(ag_ref[...])
    n_chunks = (t0 + tq + ck - 1) // ck
    kcol = lax.broadcasted_iota(jnp.int32, (tq, ck), 1)
    all_valid_selected = t0 + tq <= min(N_SEL, n_sel) * SEL_BLOCK

    GS = range(NSA_KV_HEADS)
    rows = lambda g: slice(g * HD, (g + 1) * HD)
    kc = [comp_ref[:, g * LANE:g * LANE + HD].astype(bf16) for g in GS]
    vc = [comp_ref[:, g * LANE + HD:(g + 1) * LANE].astype(bf16) for g in GS]
    q4 = [jnp.concatenate([(q_ref[:, (g * HPG + h) * HD:(g * HPG + h + 1) * HD] * scale).astype(bf16)
                           for h in range(HPG)], axis=0) for g in GS]
    kw = [kwT_ref[rows(g), pl.ds(wstart, wk)].astype(bf16) for g in GS]
    vw = [vwT_ref[rows(g), pl.ds(wstart, wk)].astype(bf16) for g in GS]
    s_c = [_dot_nt(q4[g], kc[g]) for g in GS]
    s_w = [_dot(q4[g], kw[g]) + w_bias4 for g in GS]
    p_c = [_softmax_rows(jnp.where(c_ok4, s_c[g], NEG), c_ok4) for g in GS]
    e_w = [jnp.exp(s_w[g] - jnp.max(s_w[g], axis=-1, keepdims=True)) for g in GS]
    o_c = [_dot(p_c[g].astype(bf16), vc[g]) for g in GS]
    o_w = [_normalise(_pv_and_rowsum(e_w[g], vw[g])) for g in GS]
    imp = []
    for g in GS:
        psum = p_c[g][0:tq]
        for h in range(1, HPG):
            psum = psum + p_c[g][h * tq:(h + 1) * tq]
        imp.append(_split_dot(psum, ov_ref[...]))
    sel = lax.cond(all_valid_selected,
                   lambda: tuple(jnp.where(s_ok, 1.0, 0.0) for _ in GS),
                   lambda: tuple(_select_blocks_t(imp[g], t0, n_sel) for g in GS))
    selb = [s.astype(bf16) for s in sel]

    def chunk_step(c, carry):
        k0 = pl.multiple_of(c * ck, ck)
        causal = k0 + kcol <= qpos
        ex = ex_ref[:, pl.ds(k0, ck)]
        kT = [ksT_ref[rows(g), pl.ds(k0, ck)].astype(bf16) for g in GS]
        vT = [vsT_ref[rows(g), pl.ds(k0, ck)].astype(bf16) for g in GS]
        bias = [jnp.where((_dot(selb[g], ex) > 0.5) & causal, 0.0, NEG) for g in GS]
        s = [_dot(q4[g], kT[g]) + jnp.concatenate([bias[g]] * HPG, axis=0) for g in GS]
        m_new = [jnp.maximum(carry[g][0], jnp.max(s[g], axis=-1, keepdims=True)) for g in GS]
        e = [jnp.exp(s[g] - m_new[g]) for g in GS]
        pv = [_pv_and_rowsum(e[g], vT[g]) for g in GS]
        return tuple((m_new[g], jnp.exp(carry[g][0] - m_new[g]) * carry[g][1] + pv[g]) for g in GS)

    init = tuple((jnp.full((HPG * tq, 1), NEG, f32), jnp.zeros((HPG * tq, 2 * HD), f32)) for _ in GS)
    state = lax.fori_loop(0, n_chunks, chunk_step, init)

    for g in GS:
        gate = lambda br: jnp.concatenate(
            [sg[:, 3 * (g * HPG + h) + br:3 * (g * HPG + h) + br + 1] for h in range(HPG)], axis=0)
        o = gate(0) * o_c[g] + gate(1) * _normalise(state[g][1]) + gate(2) * o_w[g]
        for h in range(HPG):
            o_ref[:, (g * HPG + h) * HD:(g * HPG + h + 1) * HD] = o[h * tq:(h + 1) * tq]


def _overlap_matrix(nc_rows, n_cmp, sw_cols, n_sel):
    cs = jnp.arange(nc_rows) * CMP_STRIDE
    ss = jnp.arange(sw_cols) * SEL_BLOCK
    ok = (cs[:, None] < ss[None, :] + SEL_BLOCK) & (cs[:, None] + CMP_BLOCK > ss[None, :])
    ok = ok & (jnp.arange(nc_rows)[:, None] < n_cmp) & (jnp.arange(sw_cols)[None, :] < n_sel)
    return ok.astype(bf16)


def nsa_prompt(proj, selT, layer, winT, comp):
    B, T, _ = proj.shape
    tq = min(256, T)
    n_cmp = T // CMP_STRIDE - (CMP_BLOCK // CMP_STRIDE) + 1
    n_sel = -(-T // SEL_BLOCK)
    NC = comp.shape[1]
    SW = LANE
    assert n_sel <= SW and T % tq == 0
    wk = min(WINDOW + tq, T)
    ck = min(512, T)
    assert T % ck == 0
    ov = _overlap_matrix(NC, n_cmp, SW, n_sel)
    ex = (jnp.arange(SW)[:, None] == (jnp.arange(T)[None, :] // SEL_BLOCK)).astype(bf16)
    sel_spec = lambda blk: pl.BlockSpec((None, None, LANE, T), lambda b, i: (layer, b, blk, 0))
    win_spec = lambda blk: pl.BlockSpec((None, LANE, T), lambda b, i: (b, blk, 0))
    return pl.pallas_call(
        functools.partial(_nsa_prompt_body, tq=tq, n_cmp=n_cmp, n_sel=n_sel, wk=wk, ck=ck),
        grid=(B, T // tq),
        in_specs=[pl.BlockSpec((None, tq, A_Q), lambda b, i: (b, i, OFF_AQ // A_Q)),
                  pl.BlockSpec((None, tq, LANE), lambda b, i: (b, i, OFF_AG // LANE)),
                  pl.BlockSpec((None, NC, 2 * LANE), lambda b, i: (b, 0, 0)),
                  sel_spec(0), sel_spec(1), win_spec(0), win_spec(1),
                  pl.BlockSpec((NC, SW), lambda b, i: (0, 0)),
                  pl.BlockSpec((SW, T), lambda b, i: (0, 0))],
        out_specs=pl.BlockSpec((None, tq, A_Q), lambda b, i: (b, i, 0)),
        out_shape=jax.ShapeDtypeStruct((B, T, A_Q), f32),
        compiler_params=_cparams(("parallel", "arbitrary")),
        name="nsa_prompt",
    )(proj, proj, comp, selT, selT, winT, winT, ov, ex)


def _nsa_sample_body(pt_ref, q_ref, ag_ref, comp_ref, kwT_ref, newT_ref, ov_ref, rsum_ref, rexp_ref, ex_ref, *rest,
                     tn, past, n_cmp, n_sel, n_pages):
    RB = q_ref.shape[0]
    pages, o_ref = rest[:RB * n_pages], rest[RB * n_pages]
    scale = HD ** -0.5
    R = HPG * tn
    NC = comp_ref.shape[1]
    SW = ov_ref.shape[1]
    tok = lax.rem(lax.broadcasted_iota(jnp.int32, (R, 1), 0), tn)
    qpos = past + tok
    col_n = lax.broadcasted_iota(jnp.int32, (R, NC), 1)
    c_ok = ((col_n * CMP_STRIDE + (CMP_BLOCK - 1)) <= qpos) & (col_n < n_cmp)
    jj = lax.broadcasted_iota(jnp.int32, (tn, SW), 1)
    qpos_t = past + lax.broadcasted_iota(jnp.int32, (tn, 1), 0)
    cur = jnp.right_shift(qpos_t, SEL_SHIFT)
    s_ok = (jj * SEL_BLOCK <= qpos_t) & (jj < n_sel)
    forced = (jj == 0) | (jj == cur) | (jj == cur - 1)
    wlen = kwT_ref.shape[2]
    kposw = (past - wlen) + lax.broadcasted_iota(jnp.int32, (R, wlen), 1)
    relw = qpos - kposw
    w_ok = (relw >= 0) & (relw < WINDOW)
    nl = newT_ref.shape[2]
    u = lax.broadcasted_iota(jnp.int32, (R, nl), 1)
    reln = tok - u
    n_ok = (reln >= 0) & (reln < WINDOW) & (u < tn)
    nb = past // SEL_BLOCK
    chains = [(r, g) for r in range(RB) for g in range(NSA_KV_HEADS)]
    each = lambda f: [f(r, g) for r, g in chains]
    krow = lambda g: slice(g * HD, (g + 1) * HD)
    vrow = lambda g: slice(2 * HD + g * HD, 2 * HD + (g + 1) * HD)
    new = lambda r, branch, rows: newT_ref[r, branch * KV_ROWS + rows.start:branch * KV_ROWS + rows.stop, :]

    def masked_attention(qs, parts_of):
        s = [[jnp.where(ok, _dot(q, kT), NEG) for kT, _, ok in parts] for q, parts in zip(qs, parts_of)]
        m = [functools.reduce(jnp.maximum, [jnp.max(x, axis=-1, keepdims=True) for x in si]) for si in s]
        e = [[jnp.exp(x - mi) for x in si] for si, mi in zip(s, m)]
        den = [sum(jnp.sum(x, axis=-1, keepdims=True) for x in ei) for ei in e]
        return [sum(_dot_nt(x.astype(bf16), vT) for x, (_, vT, _) in zip(ei, parts)) / di
                for ei, parts, di in zip(e, parts_of, den)]

    q = each(lambda r, g: (q_ref[r, g * R:(g + 1) * R, :] * scale).astype(bf16))
    kc = each(lambda r, g: comp_ref[r, :, g * LANE:g * LANE + HD].astype(bf16))
    vc = each(lambda r, g: comp_ref[r, :, g * LANE + HD:(g + 1) * LANE].astype(bf16))
    n = range(len(chains))
    pc = [_softmax_rows(jnp.where(c_ok, _dot_nt(q[i], kc[i]), NEG), c_ok) for i in n]
    o_c = [_dot(pc[i].astype(bf16), vc[i]) for i in n]
    imp = [_split_dot(_rsum_exact(rsum_ref, pc[i]), ov_ref[...]) for i in n]
    sel = [_select_blocks(imp[i], s_ok, forced, n_sel, jj) for i in n]
    selr = [_dot(rexp_ref[...], sel[i].astype(bf16)) for i in n]
    o_w = masked_attention(q, [[(kwT_ref[r, krow(g), :].astype(bf16), kwT_ref[r, vrow(g), :].astype(bf16), w_ok),
                                (new(r, 2, krow(g)).astype(bf16), new(r, 2, vrow(g)).astype(bf16), n_ok)]
                               for r, g in chains])
    parts_of = []
    for i, (r, g) in enumerate(chains):
        pgs = pages[r * n_pages:(r + 1) * n_pages]
        ksp = jnp.concatenate([pg[krow(g), :] for pg in pgs], axis=1).astype(bf16)
        vsp = jnp.concatenate([pg[vrow(g), :] for pg in pgs], axis=1).astype(bf16)
        okp = _dot(selr[i].astype(bf16), ex_ref[...]) > 0.5
        okn = (u <= tok) & (u < tn) & (selr[i][:, nb:nb + 1] > 0.5)
        parts_of.append([(ksp, vsp, okp), (new(r, 1, krow(g)).astype(bf16), new(r, 1, vrow(g)).astype(bf16), okn)])
    o_s = masked_attention(q, parts_of)
    for i, (r, g) in enumerate(chains):
        gg = jax.nn.sigmoid(ag_ref[r, g * R:(g + 1) * R, :])
        o_ref[r, g * R:(g + 1) * R, :] = gg[:, 0:1] * o_c[i] + gg[:, 1:2] * o_s[i] + gg[:, 2:3] * o_w[i]


def _rsum_exact(rsum_ref, pc):
    r = rsum_ref[...]
    hi = pc.astype(bf16)
    lo = (pc - hi.astype(f32)).astype(bf16)
    return _dot(r, hi) + _dot(r, lo)


def nsa_sample(q_s, ag_s, comp, selT, layer, page_table, winT, newT, *, tn, past):
    Bd, n_pages = page_table.shape
    page = selT.shape[-1]
    R = HPG * tn
    L = past + tn
    n_cmp = L // CMP_STRIDE - (CMP_BLOCK // CMP_STRIDE) + 1
    n_sel = -(-L // SEL_BLOCK)
    NC = comp.shape[1]
    SW = -(-n_sel // LANE) * LANE
    assert past % page == 0 and page % SEL_BLOCK == 0 and tn <= SEL_BLOCK and n_cmp <= NC
    wlen = winT.shape[-1]
    ov = _overlap_matrix(NC, n_cmp, SW, n_sel)
    rsum = (jnp.arange(tn)[:, None] == (jnp.arange(R)[None, :] % tn)).astype(bf16)
    ex = (jnp.arange(SW)[:, None] == (jnp.arange(past)[None, :] // SEL_BLOCK)).astype(bf16)
    const = lambda *shape: pl.BlockSpec(shape, lambda b, pt: (0,) * len(shape))
    RB = 2 if Bd % 2 == 0 else 1
    page_specs = [pl.BlockSpec((None, None, KV_ROWS, page), lambda b, pt, r=r, k=k: (layer, pt[b * RB + r, k], 0, 0))
                  for r in range(RB) for k in range(n_pages)]
    return pl.pallas_call(
        functools.partial(_nsa_sample_body, tn=tn, past=past, n_cmp=n_cmp, n_sel=n_sel, n_pages=n_pages),
        grid_spec=pltpu.PrefetchScalarGridSpec(
            num_scalar_prefetch=1,
            grid=(Bd // RB,),
            in_specs=[pl.BlockSpec((RB, 2 * R, HD), lambda b, pt: (b, 0, 0)),
                      pl.BlockSpec((RB, 2 * R, 3), lambda b, pt: (b, 0, 0)),
                      pl.BlockSpec((RB, NC, 2 * LANE), lambda b, pt: (b, 0, 0)),
                      pl.BlockSpec((None, RB, KV_ROWS, wlen), lambda b, pt: (layer, b, 0, 0)),
                      pl.BlockSpec((RB, A_KV, LANE), lambda b, pt: (b, 0, 0)),
                      const(NC, SW), const(tn, R), const(R, tn), _resident(ex)]
            + page_specs,
            out_specs=pl.BlockSpec((RB, 2 * R, HD), lambda b, pt: (b, 0, 0)),
        ),
        out_shape=jax.ShapeDtypeStruct((Bd, 2 * R, HD), f32),
        compiler_params=_cparams(("parallel",)),
        name="nsa_sample",
    )(page_table, q_s, ag_s, comp, winT, newT, ov, rsum, rsum.T, ex, *([selT] * (RB * n_pages)))


def _retention_body(*refs):
    _retention_step(*refs[:10], *refs[-3:])


def _retention_mix_body(*refs):
    o_ref, sn_ref, s_scr, ro_scr = refs[-4:]
    _retention_step(*refs[:10], ro_scr, sn_ref, s_scr)
    rg_ref, ga_ref, gb_ref, attn_ref, x_ref, g1_ref, woa_ref, gnw_ref, wob_ref, wo_ref = refs[10:20]
    _mix_out_body(rg_ref, ga_ref, gb_ref, attn_ref, ro_scr, x_ref, g1_ref, woa_ref, gnw_ref, wob_ref, wo_ref, o_ref)


def _retention_step(q_ref, k_ref, v_ref, cos_ref, sin_ref, dm_ref, qd_ref, kd_ref, sd_ref, s0_ref, o_ref, sn_ref, s_scr):
    c = pl.program_id(1)

    @pl.when(c == 0)
    def _():
        s_scr[...] = s0_ref[...]

    C = dm_ref.shape[1]
    for cc in range(q_ref.shape[0] // C):
        r = slice(cc * C, (cc + 1) * C)
        cos = cos_ref[r, :]
        sin = sin_ref[r, :]

        def rot(x):
            return x * cos + pltpu.roll(x, RET_DK // 2, axis=1) * sin

        for h in range(RET_HEADS):
            q = rot(q_ref[r, h * RET_DK:(h + 1) * RET_DK])
            k = rot(k_ref[r, h * RET_DK:(h + 1) * RET_DK]) * (RET_DK ** -0.5)
            v = v_ref[r, h * RET_DV:(h + 1) * RET_DV].astype(bf16)
            S = s_scr[h]
            inner = _dot_nt(q.astype(bf16), k.astype(bf16)) * dm_ref[h]
            o_ref[r, h * RET_DV:(h + 1) * RET_DV] = (_dot(inner.astype(bf16), v)
                                                     + _dot((q * qd_ref[h]).astype(bf16), S.astype(bf16)))
            s_scr[h] = S * sd_ref[h] + _dot_tn((k * kd_ref[h]).astype(bf16), v)

    @pl.when(c == pl.num_programs(1) - 1)
    def _():
        if len(sn_ref.shape) == 4:
            sn_ref[0] = s_scr[...]
            if sn_ref.shape[0] > 1:
                sn_ref[1:] = jnp.zeros((sn_ref.shape[0] - 1,) + s_scr.shape, f32)
        else:
            sn_ref[...] = s_scr[...]


def retention(proj, pos0, states, layer, out_layer, depth, prev_new):
    B, T, _ = proj.shape
    R, in_specs, args = _retention_inputs(proj, pos0, states, layer)
    st, alias_spec = _layer_slot_spec(depth, out_layer, prev_new)
    aliases = {}
    if prev_new is not None:
        in_specs.append(alias_spec)
        aliases = {len(args): 1}
        args.append(prev_new)
    return pl.pallas_call(
        _retention_body,
        grid=(B, T // R),
        in_specs=in_specs,
        out_specs=[pl.BlockSpec((None, R, B_V), lambda b, c: (b, c, 0)), st],
        out_shape=[jax.ShapeDtypeStruct((B, T, B_V), f32),
                   jax.ShapeDtypeStruct((depth, B, RET_HEADS, RET_DK, RET_DV), f32)],
        input_output_aliases=aliases,
        scratch_shapes=[pltpu.VMEM((RET_HEADS, RET_DK, RET_DV), f32)],
        compiler_params=_cparams(("parallel", "arbitrary")),
        name="retention",
    )(*args)


def _retention_inputs(proj, pos0, states, layer):
    B, T, _ = proj.shape
    C = RET_CHUNK if T % RET_CHUNK == 0 else T
    n = T // C
    H = RET_HEADS
    half = RET_DK // 2
    inv = jnp.exp(-jnp.log(ROPE_BASE) * jnp.arange(half, dtype=f32) / half)
    ang = (pos0 + jnp.arange(T)).astype(f32)[:, None] * inv[None, :]
    cos = jnp.concatenate([jnp.cos(ang), jnp.cos(ang)], axis=-1)
    sin = jnp.concatenate([-jnp.sin(ang), jnp.sin(ang)], axis=-1)
    log_g = jnp.log(1.0 - jnp.exp2(-5.0 - jnp.arange(H, dtype=f32)))
    i = jnp.arange(C, dtype=f32)
    diff = i[:, None] - i[None, :]
    dm = jnp.where(diff >= 0, jnp.exp(jnp.maximum(diff, 0.0)[None] * log_g[:, None, None]), 0.0)
    qd = jnp.exp((i + 1.0)[None, :] * log_g[:, None])[..., None]
    kd = jnp.exp((C - 1.0 - i)[None, :] * log_g[:, None])[..., None]
    sd = jnp.exp(C * log_g)[:, None, None]
    tab = lambda a: pl.BlockSpec(a.shape, lambda b, c: (0, 0, 0))
    cps = 4 if n % 4 == 0 else 1
    R = cps * C
    in_specs = [pl.BlockSpec((None, R, B_QK), lambda b, c: (b, c, OFF_RQ // B_QK)),
                pl.BlockSpec((None, R, B_QK), lambda b, c: (b, c, OFF_RK // B_QK)),
                pl.BlockSpec((None, R, B_V), lambda b, c: (b, c, OFF_RV // B_V)),
                pl.BlockSpec((R, RET_DK), lambda b, c: (c, 0)),
                pl.BlockSpec((R, RET_DK), lambda b, c: (c, 0)),
                tab(dm), tab(qd), tab(kd), tab(sd),
                pl.BlockSpec((None, None, H, RET_DK, RET_DV), lambda b, c: (layer, b, 0, 0, 0))]
    return R, in_specs, [proj, proj, proj, cos, sin, dm, qd, kd, sd, states]


def _layer_slot_spec(depth, out_layer, prev_new):
    tail = (RET_HEADS, RET_DK, RET_DV)
    if prev_new is None:
        assert out_layer == 0
        return pl.BlockSpec((depth, None) + tail, lambda b, c: (0, b, 0, 0, 0)), None
    return (pl.BlockSpec((None, None) + tail, lambda b, c: (out_layer, b, 0, 0, 0)),
            pl.BlockSpec(memory_space=pl.ANY))


def retention_mix(proj, attn, x, g1, w_oa, gn_w, w_ob, w_o, pos0, states, layer, out_layer, depth, prev_new):
    B, T, D = x.shape
    R, in_specs, args = _retention_inputs(proj, pos0, states, layer)
    gn_w = gn_w.reshape(1, B_V)
    wide = lambda blk: pl.BlockSpec((None, R, B_V), lambda b, c: (b, c, blk))
    in_specs += [wide(OFF_RG // B_V), wide(OFF_GA // B_V), wide(OFF_GB // B_V),
                 pl.BlockSpec((None, R, A_Q), lambda b, c: (b, c, 0)),
                 pl.BlockSpec((None, R, D), lambda b, c: (b, c, 0)),
                 _mod_spec(g1, R, D), _resident(w_oa), _resident(gn_w), _resident(w_ob), _resident(w_o)]
    args += [proj, proj, proj, attn, x, g1, w_oa, gn_w, w_ob, w_o]
    st, alias_spec = _layer_slot_spec(depth, out_layer, prev_new)
    aliases = {}
    if prev_new is not None:
        in_specs.append(alias_spec)
        aliases = {len(args): 1}
        args.append(prev_new)
    return pl.pallas_call(
        _retention_mix_body,
        grid=(B, T // R),
        in_specs=in_specs,
        out_specs=[pl.BlockSpec((None, R, D), lambda b, c: (b, c, 0)), st],
        out_shape=[jax.ShapeDtypeStruct((B, T, D), f32),
                   jax.ShapeDtypeStruct((depth, B, RET_HEADS, RET_DK, RET_DV), f32)],
        input_output_aliases=aliases,
        scratch_shapes=[pltpu.VMEM((RET_HEADS, RET_DK, RET_DV), f32), pltpu.VMEM((R, B_V), f32)],
        compiler_params=_cparams(("parallel", "arbitrary")),
        name="retention_mix",
    )(*args)


def _mix_out_body(rg_ref, ga_ref, gb_ref, attn_ref, ro_ref, x_ref, g1_ref, woa_ref, gnw_ref, wob_ref, wo_ref, o_ref):
    ya = _dot(attn_ref[...].astype(bf16), woa_ref[...])
    parts = []
    for h in range(RET_HEADS):
        r = ro_ref[:, h * RET_DV:(h + 1) * RET_DV]
        d = r - jnp.mean(r, axis=-1, keepdims=True)
        var = jnp.mean(d * d, axis=-1, keepdims=True)
        parts.append(d * lax.rsqrt(var + GN_EPS) * gnw_ref[:, h * RET_DV:(h + 1) * RET_DV])
    ron = jnp.concatenate(parts, axis=-1)
    rg = rg_ref[...]
    yb = _dot((rg * jax.nn.sigmoid(rg) * ron).astype(bf16), wob_ref[...])
    m = jax.nn.sigmoid(ga_ref[...]) * ya + jax.nn.sigmoid(gb_ref[...]) * yb
    o_ref[...] = x_ref[...] + g1_ref[...] * _dot(m.astype(bf16), wo_ref[...])


def _mod_spec(mod, tm, D):
    if mod.shape[1] != 1:
        return pl.BlockSpec((None, tm, D), lambda g, i: (g, i, 0))
    return pl.BlockSpec((None, 1, D), lambda g, i: (g, 0, 0))


def mix_out(proj, attn, ro, x, g1, w_oa, gn_w, w_ob, w_o, *, tm):
    G, T, D = x.shape
    full = lambda a: pl.BlockSpec(a.shape, lambda g, i: (0,) * a.ndim)
    wide = lambda blk: pl.BlockSpec((None, tm, B_V), lambda g, i: (g, i, blk))
    gn_w = gn_w.reshape(1, B_V)
    return pl.pallas_call(
        _mix_out_body,
        grid=(G, T // tm),
        in_specs=[wide(OFF_RG // B_V), wide(OFF_GA // B_V), wide(OFF_GB // B_V),
                  pl.BlockSpec((None, tm, A_Q), lambda g, i: (g, i, 0)),
                  wide(0),
                  pl.BlockSpec((None, tm, D), lambda g, i: (g, i, 0)),
                  _mod_spec(g1, tm, D), full(w_oa), full(gn_w), full(w_ob), full(w_o)],
        out_specs=pl.BlockSpec((None, tm, D), lambda g, i: (g, i, 0)),
        out_shape=jax.ShapeDtypeStruct((G, T, D), f32),
        compiler_params=_cparams(("parallel", "parallel")),
        name="mix_out",
    )(proj, proj, proj, attn, ro, x, g1, w_oa, gn_w, w_ob, w_o)


def _ffn_tail(a, am1, am2, b_ref, cw_ref, cb_ref, wout_ref, x_ref, g2_ref, nf_ref, o_ref, final_norm):
    u = cb_ref[...] + am2 * cw_ref[0:1, :] + am1 * cw_ref[1:2, :] + a * cw_ref[2:3, :]
    y = _dot((jax.nn.gelu(u) * b_ref[...]).astype(bf16), wout_ref[...])
    xo = x_ref[...] + g2_ref[...] * y
    if final_norm:
        xo = xo * lax.rsqrt(jnp.mean(xo * xo, axis=-1, keepdims=True) + RMS_EPS) * nf_ref[...]
    o_ref[...] = xo


def _ffn_seq_body(x_ref, sc_ref, sh_ref, nw_ref, win_ref, prev_ref, cw_ref, cb_ref, wout_ref, g2_ref, nf_ref,
                  o_ref, conv_ref, scr, tail_scr, *, final_norm):
    tm = x_ref.shape[0]
    F = wout_ref.shape[0]
    fc = scr.shape[1]
    h = _norm_mod(x_ref[...], nw_ref, sc_ref, sh_ref)
    first = pl.program_id(1) == 0
    y = jnp.zeros(o_ref.shape, f32)
    for c0 in range(0, F, fc):
        cols = slice(c0, c0 + fc)
        a = _dot(h, win_ref[:, cols])
        scr[8:8 + tm, :] = a
        scr[6:8, :] = jnp.where(first, prev_ref[:, cols], tail_scr[:, cols])
        last2 = scr[pl.ds(tm + 6, 2), :]
        tail_scr[:, cols] = last2
        conv_ref[:, cols] = last2
        b = _dot(h, win_ref[:, F + c0:F + c0 + fc])
        u = (cb_ref[:, cols] + scr[pl.ds(6, tm), :] * cw_ref[0:1, cols] + scr[pl.ds(7, tm), :] * cw_ref[1:2, cols]
             + a * cw_ref[2:3, cols])
        y = y + _dot((jax.nn.gelu(u) * b).astype(bf16), wout_ref[cols, :])
    xo = x_ref[...] + g2_ref[...] * y
    if final_norm:
        xo = xo * lax.rsqrt(jnp.mean(xo * xo, axis=-1, keepdims=True) + RMS_EPS) * nf_ref[...]
    o_ref[...] = xo


def ffn_seq(x, sc, sh, nw, w_in, prev, g2, conv_w, conv_b, w_out, normf_w, *, tm, final_norm):
    B, T, D = x.shape
    F = w_out.shape[0]
    assert T % tm == 0 and tm >= CONV_W - 1
    fc = F
    full = lambda a: pl.BlockSpec(a.shape, lambda g, i: (0,) * a.ndim)
    conv_b = conv_b.reshape(1, F)
    normf_w = normf_w.reshape(1, D)
    nw = nw.reshape(1, D)
    rows = pl.BlockSpec((None, tm, D), lambda g, i: (g, i, 0))
    state = pl.BlockSpec((None, CONV_W - 1, F), lambda g, i: (g, 0, 0))
    return pl.pallas_call(
        functools.partial(_ffn_seq_body, final_norm=final_norm),
        grid=(B, T // tm),
        in_specs=[rows, _mod_spec(sc, tm, D), _mod_spec(sh, tm, D), full(nw), _resident(w_in), state,
                  full(conv_w), full(conv_b), _resident(w_out), _mod_spec(g2, tm, D), full(normf_w)],
        out_specs=[rows, state],
        out_shape=[jax.ShapeDtypeStruct((B, T, D), f32), jax.ShapeDtypeStruct((B, CONV_W - 1, F), f32)],
        scratch_shapes=[pltpu.VMEM((tm + 8, fc), f32), pltpu.VMEM((CONV_W - 1, F), f32)],
        compiler_params=_cparams(("parallel", "arbitrary")),
        name="ffn_seq",
    )(x, sc, sh, nw, w_in, prev, conv_w, conv_b, w_out, g2, normf_w)


def _ffn_out_rows_body(a_ref, am1_ref, am2_ref, b_ref, cw_ref, cb_ref, wout_ref, x_ref, g2_ref, nf_ref, o_ref,
                       *, final_norm):
    _ffn_tail(a_ref[...], am1_ref[...], am2_ref[...], b_ref, cw_ref, cb_ref, wout_ref, x_ref, g2_ref, nf_ref, o_ref,
              final_norm)


def ffn_out_rows(ab, am1, am2, x, g2, conv_w, conv_b, w_out, normf_w, *, tm, final_norm):
    G, T, D = x.shape
    F = w_out.shape[0]
    full = lambda a: pl.BlockSpec(a.shape, lambda g, i: (0,) * a.ndim)
    conv_b = conv_b.reshape(1, F)
    normf_w = normf_w.reshape(1, D)
    rowsF = lambda blk: pl.BlockSpec((None, tm, F), lambda g, i: (g, i, blk))
    return pl.pallas_call(
        functools.partial(_ffn_out_rows_body, final_norm=final_norm),
        grid=(G, T // tm),
        in_specs=[rowsF(0), rowsF(0), rowsF(0), rowsF(1),
                  full(conv_w), full(conv_b), full(w_out),
                  pl.BlockSpec((None, tm, D), lambda g, i: (g, i, 0)),
                  _mod_spec(g2, tm, D), full(normf_w)],
        out_specs=pl.BlockSpec((None, tm, D), lambda g, i: (g, i, 0)),
        out_shape=jax.ShapeDtypeStruct((G, T, D), f32),
        compiler_params=_cparams(("parallel", "parallel")),
        name="ffn_out_rows",
    )(ab, am1, am2, ab, conv_w, conv_b, w_out, x, g2, normf_w)


def _prep_w_in(w, w_src):
    D = w.shape[0]
    o = 0
    parts = {}
    for name, n in (("aq", A_Q), ("akv", A_KV), ("ag", A_G), ("rq", B_QK), ("rk", B_QK), ("rv", B_V), ("rg", B_V),
                    ("ga", D), ("gb", D)):
        parts[name] = w[:, o:o + n]
        o += n
    pad = jnp.zeros((D, N_TOK - OFF_AG - A_G), w.dtype)
    tok = jnp.concatenate([parts["rg"], parts["ga"], parts["gb"], parts["rv"], parts["aq"], parts["rq"], parts["rk"],
                           parts["ag"], pad], axis=1)
    return tok.astype(bf16), w_src[:, A_Q:A_Q + A_KV].T.astype(bf16)


def _prep_cmp(cmp_pos, cmp_w1, cmp_w2):
    G = NSA_KV_HEADS
    r = jnp.arange(LANE)
    perm = ((r % CPP)[:, None] * CMP_STRIDE + (r // CPP)[:, None] == r[None, :]).astype(bf16)
    pos = jnp.concatenate([cmp_pos] * G, axis=-1)
    w1 = cmp_w1.reshape(2, CMP_BLOCK, HD, CMP_HIDDEN)
    eye = jnp.eye(G, dtype=w1.dtype)
    w1bd = jnp.einsum("eldh,gk->elgdkh", w1, eye).reshape(2, CMP_BLOCK, G * HD, G * CMP_HIDDEN)
    w1cat = jnp.concatenate([w1bd[:, :CMP_STRIDE], w1bd[:, CMP_STRIDE:]], axis=-1)
    w1cat = w1cat.reshape(2, CMP_STRIDE // 2, 2 * G * HD, 2 * G * CMP_HIDDEN)
    eye_e = jnp.eye(2, dtype=w1.dtype)
    w2bd = jnp.einsum("ehd,gk,ef->eghkfd", cmp_w2, eye, eye_e).reshape(2, G * CMP_HIDDEN, G * 2 * HD)
    return perm, pos, w1cat.astype(bf16), w2bd.astype(bf16)


def _kv_rows_to_out(rowsT, lead):
    t = rowsT.shape[-1]
    r = rowsT.reshape(*lead, 2, NSA_KV_HEADS, HD, t)
    n = len(lead)
    return r.transpose(*range(n), n + 3, n, n + 1, n + 2)


def _keep_last_lanes(a, n):
    t = a.shape[-1]
    if t >= n:
        return a[..., t - n:]
    return jnp.pad(a, ((0, 0),) * (a.ndim - 1) + ((n - t, 0),))


def kernel(x_prompt, x_sample, c_prompt, c_sample, cache_cmp_kv, cache_sel_kv, state_win_kv, state_ret, state_conv,
           page_table, norm1_w, ada_w, ada_b, w_in, cmp_pos, cmp_w1, cmp_w2, w_oa, ret_gn_w, w_ob, w_o, norm2_w,
           ffn_w_in, ffn_conv_w, ffn_conv_b, ffn_w_out, normf_w):
    B, T, D = x_prompt.shape
    Bd, Td, _ = x_sample.shape
    depth = w_in.shape[0]
    n_phys, page = cache_cmp_kv.shape[1], cache_cmp_kv.shape[2]
    past = page_table.shape[1] * page
    wlen = state_win_kv.shape[2]
    F = ffn_w_out.shape[1]
    Rs = Bd * Td
    tm_p = min(512, T)

    mod = ada_mod(jnp.concatenate([c_prompt, c_sample], axis=0), ada_w, ada_b)
    cmpT = cache_cmp_kv.transpose(0, 1, 3, 4, 5, 2).reshape(depth, n_phys, KV_ROWS, page)
    selT = cache_sel_kv.transpose(0, 1, 3, 4, 5, 2).reshape(depth, n_phys, KV_ROWS, page)
    winT = state_win_kv.transpose(0, 1, 3, 4, 5, 2).reshape(depth, Bd, KV_ROWS, wlen)

    w_in_all, w_oa_all, w_ob_all, w_o_all, ffn_in_all, ffn_out_all = (
        w.astype(bf16) for w in (w_in, w_oa, w_ob, w_o, ffn_w_in, ffn_w_out))
    ret_zero = jnp.zeros((1, B, RET_HEADS, RET_DK, RET_DV), f32)

    xp = x_prompt
    xs = x_sample.reshape(1, Rs, D)
    kv_p = None
    ret_p = ret_s = None
    outs = [[] for _ in range(10)]
    for l in range(depth):
        w_tok, w_kvT = _prep_w_in(w_in_all[l], w_in[l])
        cw = _prep_cmp(cmp_pos[l], cmp_w1[l], cmp_w2[l])
        w_oa_b, w_ob_b, w_o_b = w_oa_all[l], w_ob_all[l], w_o_all[l]
        ffn_in_b, ffn_out_b = ffn_in_all[l], ffn_out_all[l]
        last = l == depth - 1
        mp = [mod[l, :B, k * D:(k + 1) * D].reshape(B, 1, D) for k in range(6)]
        ms = [jnp.repeat(mod[l, B:, k * D:(k + 1) * D], Td, axis=0).reshape(1, Rs, D) for k in range(6)]

        proj, cmp_p, sel_p, win_p = norm_proj_layers(xp, mp[1], mp[0], norm1_w[l], w_tok, w_kvT, l, depth, kv_p,
                                                     tm=tm_p, tn=PROJ_TN)
        kv_p = (cmp_p, sel_p)
        comp = compress_prompt(cmp_p, l, cw)
        attn = nsa_prompt(proj, sel_p, l, win_p, comp)
        x1, ret_p = retention_mix(proj, attn, xp, mp[2], w_oa_b, ret_gn_w[l], w_ob_b, w_o_b, 0, ret_zero, 0, l, depth,
                                  ret_p)
        xp, conv_p = ffn_seq(x1, mp[4], mp[3], norm2_w[l], ffn_in_b, jnp.zeros((B, CONV_W - 1, F), f32), mp[5],
                             ffn_conv_w[l], ffn_conv_b[l], ffn_out_b, normf_w, tm=min(256, T), final_norm=last)
        outs[4].append(_kv_rows_to_out(_keep_last_lanes(win_p, wlen), (B,)))
        outs[8].append(conv_p)

        proj_s, kvT_s = norm_proj(xs, ms[1], ms[0], norm1_w[l], w_tok, w_kvT, tm=Rs, tn=PROJ_TN)
        comp_s = compress_paged(cmpT, l, page_table, cw)
        q_s = proj_s[0, :, OFF_AQ:OFF_AQ + A_Q].reshape(Bd, Td, NSA_KV_HEADS, HPG, HD)
        q_s = q_s.transpose(0, 2, 3, 1, 4).reshape(Bd, NSA_HEADS * Td, HD)
        ag_s = proj_s[0, :, OFF_AG:OFF_AG + A_G].reshape(Bd, Td, NSA_KV_HEADS, HPG, 3)
        ag_s = ag_s.transpose(0, 2, 3, 1, 4).reshape(Bd, NSA_HEADS * Td, 3)
        newT = kvT_s[0].reshape(A_KV, Bd, Td).transpose(1, 0, 2)
        newT_pad = jnp.pad(newT, ((0, 0), (0, 0), (0, LANE - Td)))
        attn_s = nsa_sample(q_s, ag_s, comp_s, selT, l, page_table, winT, newT_pad, tn=Td, past=past)
        attn_s = attn_s.reshape(Bd, NSA_KV_HEADS, HPG, Td, HD).transpose(0, 3, 1, 2, 4).reshape(1, Rs, A_Q)
        ro_s, ret_s = retention(proj_s.reshape(Bd, Td, N_TOK), past, state_ret, l, l, depth, ret_s)
        x1s = mix_out(proj_s, attn_s, ro_s.reshape(1, Rs, B_V), xs, ms[2], w_oa_b, ret_gn_w[l], w_ob_b, w_o_b, tm=Rs)
        ab_s = norm_proj(x1s, ms[4], ms[3], norm2_w[l], ffn_in_b, None, tm=Rs, tn=F)
        a_ext = jnp.concatenate([state_conv[l], ab_s[0, :, :F].reshape(Bd, Td, F)], axis=1)
        am1 = a_ext[:, 1:1 + Td].reshape(1, Rs, F)
        am2 = a_ext[:, 0:Td].reshape(1, Rs, F)
        xs = ffn_out_rows(ab_s, am1, am2, x1s, ms[5], ffn_conv_w[l], ffn_conv_b[l], ffn_out_b, normf_w, tm=Rs,
                          final_norm=last)
        outs[1].append(_kv_rows_to_out(newT[:, 0:KV_ROWS], (Bd,)))
        outs[3].append(_kv_rows_to_out(newT[:, KV_ROWS:2 * KV_ROWS], (Bd,)))
        if wlen >= Td:
            win_new = jnp.concatenate([winT[l, :, :, Td:], newT[:, 2 * KV_ROWS:]], axis=-1)
        else:
            win_new = newT[:, 2 * KV_ROWS:, Td - wlen:]
        outs[5].append(_kv_rows_to_out(win_new, (Bd,)))
        outs[9].append(a_ext[:, Td:])

    st = lambda k: jnp.stack(outs[k])
    new_cmp_p = _kv_rows_to_out(kv_p[0], (depth, B))
    new_sel_p = _kv_rows_to_out(kv_p[1], (depth, B))
    return (xp, xs.reshape(Bd, Td, D), new_cmp_p, st(1), new_sel_p, st(3), st(4), st(5), ret_p, ret_s, st(8), st(9))
```

```python
import functools

import jax
import jax.numpy as jnp
from jax import lax
from jax.experimental import pallas as pl
from jax.experimental.pallas import tpu as pltpu

f32 = jnp.float32
bf16 = jnp.bfloat16

NSA_HEADS = 8
NSA_KV_HEADS = 2
HPG = NSA_HEADS // NSA_KV_HEADS
HD = 64
CMP_STRIDE = 16
CMP_BLOCK = 32
CMP_HIDDEN = 64
SEL_BLOCK = 64
SEL_SHIFT = 6
N_SEL = 16
WINDOW = 512
RET_HEADS = 4
RET_DK = 128
RET_DV = 256
RET_CHUNK = 128
ROPE_BASE = 10000.0
CONV_W = 3
RMS_EPS = 1e-6
GN_EPS = 1e-5
NEG = -1e30
FORCE_BONUS = 1e4

A_Q = NSA_HEADS * HD
KV_ROWS = 2 * NSA_KV_HEADS * HD
A_KV = 3 * KV_ROWS
A_G = 3 * NSA_HEADS
B_QK = RET_HEADS * RET_DK
B_V = RET_HEADS * RET_DV

OFF_RG, OFF_GA, OFF_GB, OFF_RV = 0, 1024, 2048, 3072
OFF_AQ, OFF_RQ, OFF_RK, OFF_AG = 4096, 4608, 5120, 5632
N_TOK = 5760
PROJ_TN = 1920

LANE = 128
VMEM_LIMIT = 56 * 1024 * 1024

TM_PROJ = 512
TM_FFN = 256
TQ_NSA = 256
CK_NSA = 512
RET_CHUNKS_PER_STEP = 4
SAMPLE_ROWS_PER_STEP = 2


def _cparams(sem):
    return pltpu.CompilerParams(dimension_semantics=sem, vmem_limit_bytes=VMEM_LIMIT)


def _dot(a, b):
    return jnp.dot(a, b, preferred_element_type=f32)


def _dot_nt(a, b):
    return lax.dot_general(a, b, (((1,), (1,)), ((), ())), preferred_element_type=f32)


def _dot_tn(a, b):
    return lax.dot_general(a, b, (((0,), (0,)), ((), ())), preferred_element_type=f32)


def _split_dot(a, b_bf16):
    hi = a.astype(bf16)
    lo = (a - hi.astype(f32)).astype(bf16)
    return _dot(hi, b_bf16) + _dot(lo, b_bf16)


def _ada_body(c_ref, w_ref, b_ref, o_ref):
    c = c_ref[...]
    s = c * jax.nn.sigmoid(c)
    o_ref[...] = _dot(s.astype(bf16), w_ref[...].astype(bf16)) + b_ref[...]


def ada_mod(c_all, ada_w, ada_b):
    depth, d, n = ada_w.shape
    r = c_all.shape[0]
    tn = 1024
    return pl.pallas_call(
        _ada_body,
        grid=(depth, n // tn),
        in_specs=[pl.BlockSpec((r, d), lambda l, j: (0, 0)),
                  pl.BlockSpec((None, d, tn), lambda l, j: (l, 0, j)),
                  pl.BlockSpec((None, 1, tn), lambda l, j: (l, 0, j))],
        out_specs=pl.BlockSpec((None, r, tn), lambda l, j: (l, 0, j)),
        out_shape=jax.ShapeDtypeStruct((depth, r, n), f32),
        compiler_params=_cparams(("parallel", "parallel")),
        name="ada_mod",
    )(c_all, ada_w, ada_b.reshape(depth, 1, n))


def _norm_proj_body(x_ref, sc_ref, sh_ref, nw_ref, w_ref, *rest, has_t):
    if has_t:
        wt_ref, o_ref, ot_ref, h_scr = rest
    else:
        o_ref, h_scr = rest

    j = pl.program_id(2)
    tn = o_ref.shape[1]

    @pl.when(j == 0)
    def _():
        h_scr[...] = _norm_mod(x_ref[...], nw_ref, sc_ref, sh_ref)
        if has_t:
            ot_ref[...] = _dot_nt(wt_ref[...], h_scr[...])

    o_ref[...] = _dot(h_scr[...], w_ref[:, pl.ds(pl.multiple_of(j * tn, tn), tn)])


def _norm_mod(x, nw_ref, sc_ref, sh_ref):
    y = x * lax.rsqrt(jnp.mean(x * x, axis=-1, keepdims=True) + RMS_EPS) * nw_ref[...]
    return (y * (1.0 + sc_ref[...]) + sh_ref[...]).astype(bf16)


def _resident(a):
    return pl.BlockSpec(a.shape, lambda *_: (0,) * a.ndim, pipeline_mode=pl.Buffered(1))


def norm_proj(x, sc, sh, nw, w, wt, *, tm, tn):
    G, T, D = x.shape
    N = w.shape[1]
    per_row = sc.shape[1] != 1
    mr = tm if per_row else 1
    mod_spec = pl.BlockSpec((None, mr, D), (lambda g, i, j: (g, i, 0)) if per_row else (lambda g, i, j: (g, 0, 0)))
    in_specs = [pl.BlockSpec((None, tm, D), lambda g, i, j: (g, i, 0)), mod_spec, mod_spec,
                pl.BlockSpec((1, D), lambda g, i, j: (0, 0)),
                _resident(w)]
    out_specs = [pl.BlockSpec((None, tm, tn), lambda g, i, j: (g, i, j))]
    out_shape = [jax.ShapeDtypeStruct((G, T, N), f32)]
    args = [x, sc, sh, nw.reshape(1, D), w]
    if wt is not None:
        NT = wt.shape[0]
        in_specs.append(_resident(wt))
        out_specs.append(pl.BlockSpec((None, NT, tm), lambda g, i, j: (g, 0, i)))
        out_shape.append(jax.ShapeDtypeStruct((G, NT, T), f32))
        args.append(wt)
    res = pl.pallas_call(
        functools.partial(_norm_proj_body, has_t=wt is not None),
        grid=(G, T // tm, N // tn),
        in_specs=in_specs, out_specs=out_specs, out_shape=out_shape,
        scratch_shapes=[pltpu.VMEM((tm, D), bf16)],
        compiler_params=_cparams(("parallel", "parallel", "arbitrary")),
        name="norm_proj_t" if wt is not None else "norm_proj",
    )(*args)
    return res if wt is not None else res[0]


def _norm_proj_layers_body(x_ref, sc_ref, sh_ref, nw_ref, w_ref, wt_ref, *rest):
    o_ref, cmp_ref, sel_ref, win_ref, h_scr = rest[-5:]
    j = pl.program_id(2)
    tn = o_ref.shape[1]

    @pl.when(j == 0)
    def _():
        h_scr[...] = _norm_mod(x_ref[...], nw_ref, sc_ref, sh_ref)
        for b, ref in enumerate((cmp_ref, sel_ref, win_ref)):
            kv = _dot_nt(wt_ref[b * KV_ROWS:(b + 1) * KV_ROWS, :], h_scr[...])
            if len(ref.shape) == 3:
                ref[0] = kv
                if ref.shape[0] > 1:
                    ref[1:] = jnp.zeros((ref.shape[0] - 1,) + kv.shape, f32)
            else:
                ref[...] = kv

    o_ref[...] = _dot(h_scr[...], w_ref[:, pl.ds(pl.multiple_of(j * tn, tn), tn)])


def norm_proj_layers(x, sc, sh, nw, w, wt, layer, depth, prev, *, tm, tn):
    G, T, D = x.shape
    N = w.shape[1]
    mod_spec = pl.BlockSpec((None, 1, D), lambda g, i, j: (g, 0, 0))
    in_specs = [pl.BlockSpec((None, tm, D), lambda g, i, j: (g, i, 0)), mod_spec, mod_spec,
                pl.BlockSpec((1, D), lambda g, i, j: (0, 0)), _resident(w), _resident(wt)]
    args = [x, sc, sh, nw.reshape(1, D), w, wt]
    aliases = {}
    if prev is not None:
        in_specs += [pl.BlockSpec(memory_space=pl.ANY)] * 2
        aliases = {len(args): 1, len(args) + 1: 2}
        args += list(prev)
    if prev is None:
        assert layer == 0
        layered = pl.BlockSpec((depth, None, KV_ROWS, tm), lambda g, i, j: (0, g, 0, i))
    else:
        layered = pl.BlockSpec((None, None, KV_ROWS, tm), lambda g, i, j: (layer, g, 0, i))
    return pl.pallas_call(
        _norm_proj_layers_body,
        grid=(G, T // tm, N // tn),
        in_specs=in_specs,
        out_specs=[pl.BlockSpec((None, tm, tn), lambda g, i, j: (g, i, j)), layered, layered,
                   pl.BlockSpec((None, KV_ROWS, tm), lambda g, i, j: (g, 0, i))],
        out_shape=[jax.ShapeDtypeStruct((G, T, N), f32), jax.ShapeDtypeStruct((depth, G, KV_ROWS, T), f32),
                   jax.ShapeDtypeStruct((depth, G, KV_ROWS, T), f32), jax.ShapeDtypeStruct((G, KV_ROWS, T), f32)],
        input_output_aliases=aliases,
        scratch_shapes=[pltpu.VMEM((tm, D), bf16)],
        compiler_params=_cparams(("parallel", "parallel", "arbitrary")),
        name="norm_proj_layers",
    )(*args)


CPP = LANE // CMP_STRIDE
assert CMP_BLOCK == 2 * CMP_STRIDE


def _compress_pages(get_page, n_pages, perm_ref, pos_ref, w1_ref, w2_ref, o_ref, xs_ref, acc_ref):
    n = o_ref.shape[0]
    perm = perm_ref[...]
    for k in range(n_pages):
        y = _dot_nt(perm, get_page(k).astype(bf16))
        for e in range(2):
            for s in range(CMP_STRIDE):
                xs_ref[e, s, k * CPP:(k + 1) * CPP, :] = y[s * CPP:(s + 1) * CPP, e * LANE:(e + 1) * LANE]
    for e in range(2):
        for s in range(CMP_STRIDE):
            xs_ref[e, s, n:n + 8, :] = jnp.zeros((8, LANE), f32)
            xs_ref[e, s, n + 1:n + 2, :] = pos_ref[e, s:s + 1, :]
            xs_ref[e, s, n + 2:n + 3, :] = pos_ref[e, CMP_STRIDE + s:CMP_STRIDE + s + 1, :]
    out = jnp.zeros((n, 2 * LANE), f32)
    for e in range(2):
        acc = jnp.zeros((n + 8, 2 * LANE), f32)
        for j in range(CMP_STRIDE // 2):
            lhs = jnp.concatenate([xs_ref[e, 2 * j], xs_ref[e, 2 * j + 1]], axis=1).astype(bf16)
            acc = acc + _dot(lhs, w1_ref[e, j])
        acc_ref[...] = acc
        bias = acc_ref[n + 1:n + 2, 0:LANE] + acc_ref[n + 2:n + 3, LANE:2 * LANE]
        hid = acc_ref[0:n, 0:LANE] + acc_ref[pl.ds(1, n), LANE:2 * LANE] + bias
        out = out + _dot(jax.nn.gelu(hid).astype(bf16), w2_ref[e])
    o_ref[...] = out


def _cmp_weight_specs():
    z = lambda nd: (lambda *a: (0,) * nd)
    return [pl.BlockSpec((LANE, LANE), z(2)),
            pl.BlockSpec((2, CMP_BLOCK, LANE), z(3)),
            pl.BlockSpec((2, CMP_STRIDE // 2, 2 * LANE, 2 * LANE), z(4)),
            pl.BlockSpec((2, LANE, 2 * LANE), z(3))]


def _cmp_scratch(n):
    return [pltpu.VMEM((2, CMP_STRIDE, n + 8, LANE), f32), pltpu.VMEM((n + 8, 2 * LANE), f32)]


def _compress_prompt_body(kvT_ref, perm_ref, pos_ref, w1_ref, w2_ref, o_ref, xs_ref, acc_ref):
    n_pages = kvT_ref.shape[1] // LANE
    get = lambda k: kvT_ref[:, k * LANE:(k + 1) * LANE]
    _compress_pages(get, n_pages, perm_ref, pos_ref, w1_ref, w2_ref, o_ref, xs_ref, acc_ref)


def compress_prompt(kvT, layer, cw):
    _, B, _, T = kvT.shape
    assert T % LANE == 0
    n = T // CMP_STRIDE
    return pl.pallas_call(
        _compress_prompt_body,
        grid=(B,),
        in_specs=[pl.BlockSpec((None, None, KV_ROWS, T), lambda b: (layer, b, 0, 0))] + _cmp_weight_specs(),
        out_specs=pl.BlockSpec((None, n, 2 * LANE), lambda b: (b, 0, 0)),
        out_shape=jax.ShapeDtypeStruct((B, n, 2 * LANE), f32),
        scratch_shapes=_cmp_scratch(n),
        compiler_params=_cparams(("parallel",)),
        name="compress_prompt",
    )(kvT, *cw)


def _page_specs(layer, n_pages, page):
    return [pl.BlockSpec((None, None, KV_ROWS, page), lambda b, pt, k=k: (layer, pt[b, k], 0, 0))
            for k in range(n_pages)]


def _compress_paged_body(pt_ref, perm_ref, pos_ref, w1_ref, w2_ref, *rest, n_pages):
    pages, (o_ref, xs_ref, acc_ref) = rest[:n_pages], rest[n_pages:]
    get = lambda k: pages[k][...]
    _compress_pages(get, n_pages, perm_ref, pos_ref, w1_ref, w2_ref, o_ref, xs_ref, acc_ref)


def compress_paged(cacheT, layer, page_table, cw):
    Bd, n_pages = page_table.shape
    page = cacheT.shape[-1]
    assert page == LANE
    L = n_pages * page
    n = L // CMP_STRIDE
    return pl.pallas_call(
        functools.partial(_compress_paged_body, n_pages=n_pages),
        grid_spec=pltpu.PrefetchScalarGridSpec(
            num_scalar_prefetch=1,
            grid=(Bd,),
            in_specs=_cmp_weight_specs() + _page_specs(layer, n_pages, page),
            out_specs=pl.BlockSpec((None, n, 2 * LANE), lambda b, pt: (b, 0, 0)),
            scratch_shapes=_cmp_scratch(n),
        ),
        out_shape=jax.ShapeDtypeStruct((Bd, n, 2 * LANE), f32),
        compiler_params=_cparams(("parallel",)),
        name="compress_paged",
    )(page_table, *cw, *([cacheT] * n_pages))


def _select_blocks(imp, s_ok, forced, n_sel, jj):
    k = min(N_SEL, n_sel)
    score = jnp.where(s_ok, imp + jnp.where(forced, FORCE_BONUS, 0.0), NEG)
    rank = jnp.zeros(score.shape, f32)
    for j2 in range(n_sel):
        col = score[:, j2:j2 + 1]
        ge = jnp.where(col >= score, 1.0, 0.0)
        gt = jnp.where(col > score, 1.0, 0.0)
        rank = rank + jnp.where(jj > j2, ge, gt)
    return jnp.where((rank < k) & s_ok, 1.0, 0.0)


def _select_blocks_t(imp, t0, n_sel):
    tq, sw = imp.shape
    rows = -(-n_sel // 8) * 8
    k = min(N_SEL, n_sel)
    jj = lax.broadcasted_iota(jnp.int32, (rows, tq), 0)
    qpos = t0 + lax.broadcasted_iota(jnp.int32, (rows, tq), 1)
    cur = jnp.right_shift(qpos, SEL_SHIFT)
    s_ok = (jj * SEL_BLOCK <= qpos) & (jj < n_sel)
    forced = (jj == 0) | (jj == cur) | (jj == cur - 1)
    imp_t = jnp.concatenate([imp[c * LANE:(c + 1) * LANE].T for c in range(tq // LANE)], axis=1)[:rows]
    score = jnp.where(s_ok, imp_t + jnp.where(forced, FORCE_BONUS, 0.0), NEG)
    rank = jnp.zeros(score.shape, f32)
    for j2 in range(n_sel):
        row = score[j2:j2 + 1, :]
        ge = jnp.where(row >= score, 1.0, 0.0)
        gt = jnp.where(row > score, 1.0, 0.0)
        rank = rank + jnp.where(jj > j2, ge, gt)
    sel_t = jnp.where((rank < k) & s_ok, 1.0, 0.0)
    sel_t = jnp.concatenate([sel_t, jnp.zeros((sw - rows, tq), f32)], axis=0)
    return jnp.concatenate([sel_t[:, c * LANE:(c + 1) * LANE].T for c in range(tq // LANE)], axis=0)


def _softmax_rows(sm, ok):
    m = jnp.max(sm, axis=-1, keepdims=True)
    e = jnp.where(ok, jnp.exp(sm - m), 0.0)
    den = jnp.sum(e, axis=-1, keepdims=True)
    return e / jnp.maximum(den, 1e-30)


def _pv_and_rowsum(e, vT):
    v_aug = jnp.concatenate([vT, jnp.ones(vT.shape, vT.dtype)], axis=0)
    return _dot_nt(e.astype(bf16), v_aug)


def _normalise(acc):
    return (acc / pltpu.roll(acc, HD, axis=1))[:, :HD]


def _nsa_prompt_body(q_ref, ag_ref, comp_ref, ksT_ref, vsT_ref, kwT_ref, vwT_ref, ov_ref, ex_ref, o_ref,
                     *, tq, n_cmp, n_sel, wk, ck):
    i = pl.program_id(1)
    t0 = i * tq
    scale = HD ** -0.5
    NC = comp_ref.shape[0]
    SW = ov_ref.shape[1]
    qpos = t0 + lax.broadcasted_iota(jnp.int32, (tq, 1), 0)
    col_n = lax.broadcasted_iota(jnp.int32, (tq, NC), 1)
    c_ok = ((col_n * CMP_STRIDE + (CMP_BLOCK - 1)) <= qpos) & (col_n < n_cmp)
    jj = lax.broadcasted_iota(jnp.int32, (tq, SW), 1)
    s_ok = (jj * SEL_BLOCK <= qpos) & (jj < n_sel)
    wstart = pl.multiple_of(jnp.maximum(t0 + tq - wk, 0), LANE)
    kposw = wstart + lax.broadcasted_iota(jnp.int32, (tq, wk), 1)
    rel = qpos - kposw
    w_bias4 = jnp.concatenate([jnp.where((rel >= 0) & (rel < WINDOW), 0.0, NEG)] * HPG, axis=0)
    c_ok4 = jnp.concatenate([c_ok] * HPG, axis=0)
    sg = jax.nn.sigmoid(ag_ref[...])
    n_chunks = (t0 + tq + ck - 1) // ck
    kcol = lax.broadcasted_iota(jnp.int32, (tq, ck), 1)
    all_valid_selected = t0 + tq <= min(N_SEL, n_sel) * SEL_BLOCK

    GS = range(NSA_KV_HEADS)
    rows = lambda g: slice(g * HD, (g + 1) * HD)
    kc = [comp_ref[:, g * LANE:g * LANE + HD].astype(bf16) for g in GS]
    vc = [comp_ref[:, g * LANE + HD:(g + 1) * LANE].astype(bf16) for g in GS]
    q4 = [jnp.concatenate([(q_ref[:, (g * HPG + h) * HD:(g * HPG + h + 1) * HD] * scale).astype(bf16)
                           for h in range(HPG)], axis=0) for g in GS]
    kw = [kwT_ref[rows(g), pl.ds(wstart, wk)].astype(bf16) for g in GS]
    vw = [vwT_ref[rows(g), pl.ds(wstart, wk)].astype(bf16) for g in GS]
    s_c = [_dot_nt(q4[g], kc[g]) for g in GS]
    s_w = [_dot(q4[g], kw[g]) + w_bias4 for g in GS]
    p_c = [_softmax_rows(jnp.where(c_ok4, s_c[g], NEG), c_ok4) for g in GS]
    e_w = [jnp.exp(s_w[g] - jnp.max(s_w[g], axis=-1, keepdims=True)) for g in GS]
    o_c = [_dot(p_c[g].astype(bf16), vc[g]) for g in GS]
    o_w = [_normalise(_pv_and_rowsum(e_w[g], vw[g])) for g in GS]
    imp = []
    for g in GS:
        psum = p_c[g][0:tq]
        for h in range(1, HPG):
            psum = psum + p_c[g][h * tq:(h + 1) * tq]
        imp.append(_split_dot(psum, ov_ref[...]))
    sel = lax.cond(all_valid_selected,
                   lambda: tuple(jnp.where(s_ok, 1.0, 0.0) for _ in GS),
                   lambda: tuple(_select_blocks_t(imp[g], t0, n_sel) for g in GS))
    selb = [s.astype(bf16) for s in sel]

    def chunk_step(c, carry):
        k0 = pl.multiple_of(c * ck, ck)
        causal = k0 + kcol <= qpos
        ex = ex_ref[:, pl.ds(k0, ck)]
        kT = [ksT_ref[rows(g), pl.ds(k0, ck)].astype(bf16) for g in GS]
        vT = [vsT_ref[rows(g), pl.ds(k0, ck)].astype(bf16) for g in GS]
        bias = [jnp.where((_dot(selb[g], ex) > 0.5) & causal, 0.0, NEG) for g in GS]
        s = [_dot(q4[g], kT[g]) + jnp.concatenate([bias[g]] * HPG, axis=0) for g in GS]
        m_new = [jnp.maximum(carry[g][0], jnp.max(s[g], axis=-1, keepdims=True)) for g in GS]
        e = [jnp.exp(s[g] - m_new[g]) for g in GS]
        pv = [_pv_and_rowsum(e[g], vT[g]) for g in GS]
        return tuple((m_new[g], jnp.exp(carry[g][0] - m_new[g]) * carry[g][1] + pv[g]) for g in GS)

    init = tuple((jnp.full((HPG * tq, 1), NEG, f32), jnp.zeros((HPG * tq, 2 * HD), f32)) for _ in GS)
    state = lax.fori_loop(0, n_chunks, chunk_step, init)

    for g in GS:
        gate = lambda br: jnp.concatenate(
            [sg[:, 3 * (g * HPG + h) + br:3 * (g * HPG + h) + br + 1] for h in range(HPG)], axis=0)
        o = gate(0) * o_c[g] + gate(1) * _normalise(state[g][1]) + gate(2) * o_w[g]
        for h in range(HPG):
            o_ref[:, (g * HPG + h) * HD:(g * HPG + h + 1) * HD] = o[h * tq:(h + 1) * tq]


def _overlap_matrix(nc_rows, n_cmp, sw_cols, n_sel):
    cs = jnp.arange(nc_rows) * CMP_STRIDE
    ss = jnp.arange(sw_cols) * SEL_BLOCK
    ok = (cs[:, None] < ss[None, :] + SEL_BLOCK) & (cs[:, None] + CMP_BLOCK > ss[None, :])
    ok = ok & (jnp.arange(nc_rows)[:, None] < n_cmp) & (jnp.arange(sw_cols)[None, :] < n_sel)
    return ok.astype(bf16)


def nsa_prompt(proj, selT, layer, winT, comp):
    B, T, _ = proj.shape
    tq = min(TQ_NSA, T)
    n_cmp = T // CMP_STRIDE - (CMP_BLOCK // CMP_STRIDE) + 1
    n_sel = -(-T // SEL_BLOCK)
    NC = comp.shape[1]
    SW = LANE
    assert n_sel <= SW and T % tq == 0
    wk = min(WINDOW + tq, T)
    ck = min(CK_NSA, T)
    assert T % ck == 0
    ov = _overlap_matrix(NC, n_cmp, SW, n_sel)
    ex = (jnp.arange(SW)[:, None] == (jnp.arange(T)[None, :] // SEL_BLOCK)).astype(bf16)
    sel_spec = lambda blk: pl.BlockSpec((None, None, LANE, T), lambda b, i: (layer, b, blk, 0))
    win_spec = lambda blk: pl.BlockSpec((None, LANE, T), lambda b, i: (b, blk, 0))
    return pl.pallas_call(
        functools.partial(_nsa_prompt_body, tq=tq, n_cmp=n_cmp, n_sel=n_sel, wk=wk, ck=ck),
        grid=(B, T // tq),
        in_specs=[pl.BlockSpec((None, tq, A_Q), lambda b, i: (b, i, OFF_AQ // A_Q)),
                  pl.BlockSpec((None, tq, LANE), lambda b, i: (b, i, OFF_AG // LANE)),
                  pl.BlockSpec((None, NC, 2 * LANE), lambda b, i: (b, 0, 0)),
                  sel_spec(0), sel_spec(1), win_spec(0), win_spec(1),
                  pl.BlockSpec((NC, SW), lambda b, i: (0, 0)),
                  pl.BlockSpec((SW, T), lambda b, i: (0, 0))],
        out_specs=pl.BlockSpec((None, tq, A_Q), lambda b, i: (b, i, 0)),
        out_shape=jax.ShapeDtypeStruct((B, T, A_Q), f32),
        compiler_params=_cparams(("parallel", "arbitrary")),
        name="nsa_prompt",
    )(proj, proj, comp, selT, selT, winT, winT, ov, ex)


def _nsa_sample_body(pt_ref, q_ref, ag_ref, comp_ref, kwT_ref, newT_ref, ov_ref, rsum_ref, rexp_ref, ex_ref, *rest,
                     tn, past, n_cmp, n_sel, n_pages):
    RB = q_ref.shape[0]
    pages, o_ref = rest[:RB * n_pages], rest[RB * n_pages]
    scale = HD ** -0.5
    R = HPG * tn
    NC = comp_ref.shape[1]
    SW = ov_ref.shape[1]
    tok = lax.rem(lax.broadcasted_iota(jnp.int32, (R, 1), 0), tn)
    qpos = past + tok
    col_n = lax.broadcasted_iota(jnp.int32, (R, NC), 1)
    c_ok = ((col_n * CMP_STRIDE + (CMP_BLOCK - 1)) <= qpos) & (col_n < n_cmp)
    jj = lax.broadcasted_iota(jnp.int32, (tn, SW), 1)
    qpos_t = past + lax.broadcasted_iota(jnp.int32, (tn, 1), 0)
    cur = jnp.right_shift(qpos_t, SEL_SHIFT)
    s_ok = (jj * SEL_BLOCK <= qpos_t) & (jj < n_sel)
    forced = (jj == 0) | (jj == cur) | (jj == cur - 1)
    wlen = kwT_ref.shape[2]
    kposw = (past - wlen) + lax.broadcasted_iota(jnp.int32, (R, wlen), 1)
    relw = qpos - kposw
    w_ok = (relw >= 0) & (relw < WINDOW)
    nl = newT_ref.shape[2]
    u = lax.broadcasted_iota(jnp.int32, (R, nl), 1)
    reln = tok - u
    n_ok = (reln >= 0) & (reln < WINDOW) & (u < tn)
    nb = past // SEL_BLOCK
    chains = [(r, g) for r in range(RB) for g in range(NSA_KV_HEADS)]
    each = lambda f: [f(r, g) for r, g in chains]
    krow = lambda g: slice(g * HD, (g + 1) * HD)
    vrow = lambda g: slice(2 * HD + g * HD, 2 * HD + (g + 1) * HD)
    new = lambda r, branch, rows: newT_ref[r, branch * KV_ROWS + rows.start:branch * KV_ROWS + rows.stop, :]

    def masked_attention(qs, parts_of):
        s = [[jnp.where(ok, _dot(q, kT), NEG) for kT, _, ok in parts] for q, parts in zip(qs, parts_of)]
        m = [functools.reduce(jnp.maximum, [jnp.max(x, axis=-1, keepdims=True) for x in si]) for si in s]
        e = [[jnp.exp(x - mi) for x in si] for si, mi in zip(s, m)]
        den = [sum(jnp.sum(x, axis=-1, keepdims=True) for x in ei) for ei in e]
        return [sum(_dot_nt(x.astype(bf16), vT) for x, (_, vT, _) in zip(ei, parts)) / di
                for ei, parts, di in zip(e, parts_of, den)]

    q = each(lambda r, g: (q_ref[r, g * R:(g + 1) * R, :] * scale).astype(bf16))
    kc = each(lambda r, g: comp_ref[r, :, g * LANE:g * LANE + HD].astype(bf16))
    vc = each(lambda r, g: comp_ref[r, :, g * LANE + HD:(g + 1) * LANE].astype(bf16))
    n = range(len(chains))
    pc = [_softmax_rows(jnp.where(c_ok, _dot_nt(q[i], kc[i]), NEG), c_ok) for i in n]
    o_c = [_dot(pc[i].astype(bf16), vc[i]) for i in n]
    imp = [_split_dot(_rsum_exact(rsum_ref, pc[i]), ov_ref[...]) for i in n]
    sel = [_select_blocks(imp[i], s_ok, forced, n_sel, jj) for i in n]
    selr = [_dot(rexp_ref[...], sel[i].astype(bf16)) for i in n]
    o_w = masked_attention(q, [[(kwT_ref[r, krow(g), :].astype(bf16), kwT_ref[r, vrow(g), :].astype(bf16), w_ok),
                                (new(r, 2, krow(g)).astype(bf16), new(r, 2, vrow(g)).astype(bf16), n_ok)]
                               for r, g in chains])
    parts_of = []
    for i, (r, g) in enumerate(chains):
        pgs = pages[r * n_pages:(r + 1) * n_pages]
        ksp = jnp.concatenate([pg[krow(g), :] for pg in pgs], axis=1).astype(bf16)
        vsp = jnp.concatenate([pg[vrow(g), :] for pg in pgs], axis=1).astype(bf16)
        okp = _dot(selr[i].astype(bf16), ex_ref[...]) > 0.5
        okn = (u <= tok) & (u < tn) & (selr[i][:, nb:nb + 1] > 0.5)
        parts_of.append([(ksp, vsp, okp), (new(r, 1, krow(g)).astype(bf16), new(r, 1, vrow(g)).astype(bf16), okn)])
    o_s = masked_attention(q, parts_of)
    for i, (r, g) in enumerate(chains):
        gg = jax.nn.sigmoid(ag_ref[r, g * R:(g + 1) * R, :])
        o_ref[r, g * R:(g + 1) * R, :] = gg[:, 0:1] * o_c[i] + gg[:, 1:2] * o_s[i] + gg[:, 2:3] * o_w[i]


def _rsum_exact(rsum_ref, pc):
    r = rsum_ref[...]
    hi = pc.astype(bf16)
    lo = (pc - hi.astype(f32)).astype(bf16)
    return _dot(r, hi) + _dot(r, lo)


def nsa_sample(q_s, ag_s, comp, selT, layer, page_table, winT, newT, *, tn, past):
    Bd, n_pages = page_table.shape
    page = selT.shape[-1]
    R = HPG * tn
    L = past + tn
    n_cmp = L // CMP_STRIDE - (CMP_BLOCK // CMP_STRIDE) + 1
    n_sel = -(-L // SEL_BLOCK)
    NC = comp.shape[1]
    SW = -(-n_sel // LANE) * LANE
    assert past % page == 0 and page % SEL_BLOCK == 0 and tn <= SEL_BLOCK and n_cmp <= NC
    wlen = winT.shape[-1]
    ov = _overlap_matrix(NC, n_cmp, SW, n_sel)
    rsum = (jnp.arange(tn)[:, None] == (jnp.arange(R)[None, :] % tn)).astype(bf16)
    ex = (jnp.arange(SW)[:, None] == (jnp.arange(past)[None, :] // SEL_BLOCK)).astype(bf16)
    const = lambda *shape: pl.BlockSpec(shape, lambda b, pt: (0,) * len(shape))
    RB = SAMPLE_ROWS_PER_STEP if Bd % SAMPLE_ROWS_PER_STEP == 0 else 1
    page_specs = [pl.BlockSpec((None, None, KV_ROWS, page), lambda b, pt, r=r, k=k: (layer, pt[b * RB + r, k], 0, 0))
                  for r in range(RB) for k in range(n_pages)]
    return pl.pallas_call(
        functools.partial(_nsa_sample_body, tn=tn, past=past, n_cmp=n_cmp, n_sel=n_sel, n_pages=n_pages),
        grid_spec=pltpu.PrefetchScalarGridSpec(
            num_scalar_prefetch=1,
            grid=(Bd // RB,),
            in_specs=[pl.BlockSpec((RB, 2 * R, HD), lambda b, pt: (b, 0, 0)),
                      pl.BlockSpec((RB, 2 * R, 3), lambda b, pt: (b, 0, 0)),
                      pl.BlockSpec((RB, NC, 2 * LANE), lambda b, pt: (b, 0, 0)),
                      pl.BlockSpec((None, RB, KV_ROWS, wlen), lambda b, pt: (layer, b, 0, 0)),
                      pl.BlockSpec((RB, A_KV, LANE), lambda b, pt: (b, 0, 0)),
                      const(NC, SW), const(tn, R), const(R, tn), _resident(ex)]
            + page_specs,
            out_specs=pl.BlockSpec((RB, 2 * R, HD), lambda b, pt: (b, 0, 0)),
        ),
        out_shape=jax.ShapeDtypeStruct((Bd, 2 * R, HD), f32),
        compiler_params=_cparams(("parallel",)),
        name="nsa_sample",
    )(page_table, q_s, ag_s, comp, winT, newT, ov, rsum, rsum.T, ex, *([selT] * (RB * n_pages)))


def _retention_body(*refs):
    _retention_step(*refs[:10], *refs[-3:])


def _retention_mix_body(*refs):
    o_ref, sn_ref, s_scr, ro_scr = refs[-4:]
    _retention_step(*refs[:10], ro_scr, sn_ref, s_scr)
    rg_ref, ga_ref, gb_ref, attn_ref, x_ref, g1_ref, woa_ref, gnw_ref, wob_ref, wo_ref = refs[10:20]
    _mix_out_body(rg_ref, ga_ref, gb_ref, attn_ref, ro_scr, x_ref, g1_ref, woa_ref, gnw_ref, wob_ref, wo_ref, o_ref)


def _retention_step(q_ref, k_ref, v_ref, cos_ref, sin_ref, dm_ref, qd_ref, kd_ref, sd_ref, s0_ref, o_ref, sn_ref, s_scr):
    c = pl.program_id(1)

    @pl.when(c == 0)
    def _():
        s_scr[...] = s0_ref[...]

    C = dm_ref.shape[1]
    for cc in range(q_ref.shape[0] // C):
        r = slice(cc * C, (cc + 1) * C)
        cos = cos_ref[r, :]
        sin = sin_ref[r, :]

        def rot(x):
            return x * cos + pltpu.roll(x, RET_DK // 2, axis=1) * sin

        for h in range(RET_HEADS):
            q = rot(q_ref[r, h * RET_DK:(h + 1) * RET_DK])
            k = rot(k_ref[r, h * RET_DK:(h + 1) * RET_DK]) * (RET_DK ** -0.5)
            v = v_ref[r, h * RET_DV:(h + 1) * RET_DV].astype(bf16)
            S = s_scr[h]
            inner = _dot_nt(q.astype(bf16), k.astype(bf16)) * dm_ref[h]
            o_ref[r, h * RET_DV:(h + 1) * RET_DV] = (_dot(inner.astype(bf16), v)
                                                     + _dot((q * qd_ref[h]).astype(bf16), S.astype(bf16)))
            s_scr[h] = S * sd_ref[h] + _dot_tn((k * kd_ref[h]).astype(bf16), v)

    @pl.when(c == pl.num_programs(1) - 1)
    def _():
        if len(sn_ref.shape) == 4:
            sn_ref[0] = s_scr[...]
            if sn_ref.shape[0] > 1:
                sn_ref[1:] = jnp.zeros((sn_ref.shape[0] - 1,) + s_scr.shape, f32)
        else:
            sn_ref[...] = s_scr[...]


def retention(proj, pos0, states, layer, out_layer, depth, prev_new):
    B, T, _ = proj.shape
    R, in_specs, args = _retention_inputs(proj, pos0, states, layer)
    st, alias_spec = _layer_slot_spec(depth, out_layer, prev_new)
    aliases = {}
    if prev_new is not None:
        in_specs.append(alias_spec)
        aliases = {len(args): 1}
        args.append(prev_new)
    return pl.pallas_call(
        _retention_body,
        grid=(B, T // R),
        in_specs=in_specs,
        out_specs=[pl.BlockSpec((None, R, B_V), lambda b, c: (b, c, 0)), st],
        out_shape=[jax.ShapeDtypeStruct((B, T, B_V), f32),
                   jax.ShapeDtypeStruct((depth, B, RET_HEADS, RET_DK, RET_DV), f32)],
        input_output_aliases=aliases,
        scratch_shapes=[pltpu.VMEM((RET_HEADS, RET_DK, RET_DV), f32)],
        compiler_params=_cparams(("parallel", "arbitrary")),
        name="retention",
    )(*args)


def _retention_inputs(proj, pos0, states, layer):
    B, T, _ = proj.shape
    C = RET_CHUNK if T % RET_CHUNK == 0 else T
    n = T // C
    H = RET_HEADS
    half = RET_DK // 2
    inv = jnp.exp(-jnp.log(ROPE_BASE) * jnp.arange(half, dtype=f32) / half)
    ang = (pos0 + jnp.arange(T)).astype(f32)[:, None] * inv[None, :]
    cos = jnp.concatenate([jnp.cos(ang), jnp.cos(ang)], axis=-1)
    sin = jnp.concatenate([-jnp.sin(ang), jnp.sin(ang)], axis=-1)
    log_g = jnp.log(1.0 - jnp.exp2(-5.0 - jnp.arange(H, dtype=f32)))
    i = jnp.arange(C, dtype=f32)
    diff = i[:, None] - i[None, :]
    dm = jnp.where(diff >= 0, jnp.exp(jnp.maximum(diff, 0.0)[None] * log_g[:, None, None]), 0.0)
    qd = jnp.exp((i + 1.0)[None, :] * log_g[:, None])[..., None]
    kd = jnp.exp((C - 1.0 - i)[None, :] * log_g[:, None])[..., None]
    sd = jnp.exp(C * log_g)[:, None, None]
    tab = lambda a: pl.BlockSpec(a.shape, lambda b, c: (0, 0, 0))
    cps = RET_CHUNKS_PER_STEP if n % RET_CHUNKS_PER_STEP == 0 else 1
    R = cps * C
    in_specs = [pl.BlockSpec((None, R, B_QK), lambda b, c: (b, c, OFF_RQ // B_QK)),
                pl.BlockSpec((None, R, B_QK), lambda b, c: (b, c, OFF_RK // B_QK)),
                pl.BlockSpec((None, R, B_V), lambda b, c: (b, c, OFF_RV // B_V)),
                pl.BlockSpec((R, RET_DK), lambda b, c: (c, 0)),
                pl.BlockSpec((R, RET_DK), lambda b, c: (c, 0)),
                tab(dm), tab(qd), tab(kd), tab(sd),
                pl.BlockSpec((None, None, H, RET_DK, RET_DV), lambda b, c: (layer, b, 0, 0, 0))]
    return R, in_specs, [proj, proj, proj, cos, sin, dm, qd, kd, sd, states]


def _layer_slot_spec(depth, out_layer, prev_new):
    tail = (RET_HEADS, RET_DK, RET_DV)
    if prev_new is None:
        assert out_layer == 0
        return pl.BlockSpec((depth, None) + tail, lambda b, c: (0, b, 0, 0, 0)), None
    return (pl.BlockSpec((None, None) + tail, lambda b, c: (out_layer, b, 0, 0, 0)),
            pl.BlockSpec(memory_space=pl.ANY))


def retention_mix(proj, attn, x, g1, w_oa, gn_w, w_ob, w_o, pos0, states, layer, out_layer, depth, prev_new):
    B, T, D = x.shape
    R, in_specs, args = _retention_inputs(proj, pos0, states, layer)
    gn_w = gn_w.reshape(1, B_V)
    wide = lambda blk: pl.BlockSpec((None, R, B_V), lambda b, c: (b, c, blk))
    in_specs += [wide(OFF_RG // B_V), wide(OFF_GA // B_V), wide(OFF_GB // B_V),
                 pl.BlockSpec((None, R, A_Q), lambda b, c: (b, c, 0)),
                 pl.BlockSpec((None, R, D), lambda b, c: (b, c, 0)),
                 _mod_spec(g1, R, D), _resident(w_oa), _resident(gn_w), _resident(w_ob), _resident(w_o)]
    args += [proj, proj, proj, attn, x, g1, w_oa, gn_w, w_ob, w_o]
    st, alias_spec = _layer_slot_spec(depth, out_layer, prev_new)
    aliases = {}
    if prev_new is not None:
        in_specs.append(alias_spec)
        aliases = {len(args): 1}
        args.append(prev_new)
    return pl.pallas_call(
        _retention_mix_body,
        grid=(B, T // R),
        in_specs=in_specs,
        out_specs=[pl.BlockSpec((None, R, D), lambda b, c: (b, c, 0)), st],
        out_shape=[jax.ShapeDtypeStruct((B, T, D), f32),
                   jax.ShapeDtypeStruct((depth, B, RET_HEADS, RET_DK, RET_DV), f32)],
        input_output_aliases=aliases,
        scratch_shapes=[pltpu.VMEM((RET_HEADS, RET_DK, RET_DV), f32), pltpu.VMEM((R, B_V), f32)],
        compiler_params=_cparams(("parallel", "arbitrary")),
        name="retention_mix",
    )(*args)


def _mix_out_body(rg_ref, ga_ref, gb_ref, attn_ref, ro_ref, x_ref, g1_ref, woa_ref, gnw_ref, wob_ref, wo_ref, o_ref):
    ya = _dot(attn_ref[...].astype(bf16), woa_ref[...])
    parts = []
    for h in range(RET_HEADS):
        r = ro_ref[:, h * RET_DV:(h + 1) * RET_DV]
        d = r - jnp.mean(r, axis=-1, keepdims=True)
        var = jnp.mean(d * d, axis=-1, keepdims=True)
        parts.append(d * lax.rsqrt(var + GN_EPS) * gnw_ref[:, h * RET_DV:(h + 1) * RET_DV])
    ron = jnp.concatenate(parts, axis=-1)
    rg = rg_ref[...]
    yb = _dot((rg * jax.nn.sigmoid(rg) * ron).astype(bf16), wob_ref[...])
    m = jax.nn.sigmoid(ga_ref[...]) * ya + jax.nn.sigmoid(gb_ref[...]) * yb
    o_ref[...] = x_ref[...] + g1_ref[...] * _dot(m.astype(bf16), wo_ref[...])


def _mod_spec(mod, tm, D):
    if mod.shape[1] != 1:
        return pl.BlockSpec((None, tm, D), lambda g, i: (g, i, 0))
    return pl.BlockSpec((None, 1, D), lambda g, i: (g, 0, 0))


def mix_out(proj, attn, ro, x, g1, w_oa, gn_w, w_ob, w_o, *, tm):
    G, T, D = x.shape
    full = lambda a: pl.BlockSpec(a.shape, lambda g, i: (0,) * a.ndim)
    wide = lambda blk: pl.BlockSpec((None, tm, B_V), lambda g, i: (g, i, blk))
    gn_w = gn_w.reshape(1, B_V)
    return pl.pallas_call(
        _mix_out_body,
        grid=(G, T // tm),
        in_specs=[wide(OFF_RG // B_V), wide(OFF_GA // B_V), wide(OFF_GB // B_V),
                  pl.BlockSpec((None, tm, A_Q), lambda g, i: (g, i, 0)),
                  wide(0),
                  pl.BlockSpec((None, tm, D), lambda g, i: (g, i, 0)),
                  _mod_spec(g1, tm, D), full(w_oa), full(gn_w), full(w_ob), full(w_o)],
        out_specs=pl.BlockSpec((None, tm, D), lambda g, i: (g, i, 0)),
        out_shape=jax.ShapeDtypeStruct((G, T, D), f32),
        compiler_params=_cparams(("parallel", "parallel")),
        name="mix_out",
    )(proj, proj, proj, attn, ro, x, g1, w_oa, gn_w, w_ob, w_o)


def _ffn_tail(a, am1, am2, b_ref, cw_ref, cb_ref, wout_ref, x_ref, g2_ref, nf_ref, o_ref, final_norm):
    u = cb_ref[...] + am2 * cw_ref[0:1, :] + am1 * cw_ref[1:2, :] + a * cw_ref[2:3, :]
    y = _dot((jax.nn.gelu(u) * b_ref[...]).astype(bf16), wout_ref[...])
    xo = x_ref[...] + g2_ref[...] * y
    if final_norm:
        xo = xo * lax.rsqrt(jnp.mean(xo * xo, axis=-1, keepdims=True) + RMS_EPS) * nf_ref[...]
    o_ref[...] = xo


def _ffn_seq_body(x_ref, sc_ref, sh_ref, nw_ref, win_ref, prev_ref, cw_ref, cb_ref, wout_ref, g2_ref, nf_ref,
                  o_ref, conv_ref, scr, tail_scr, *, final_norm):
    tm = x_ref.shape[0]
    F = wout_ref.shape[0]
    fc = scr.shape[1]
    h = _norm_mod(x_ref[...], nw_ref, sc_ref, sh_ref)
    first = pl.program_id(1) == 0
    y = jnp.zeros(o_ref.shape, f32)
    for c0 in range(0, F, fc):
        cols = slice(c0, c0 + fc)
        a = _dot(h, win_ref[:, cols])
        scr[8:8 + tm, :] = a
        scr[6:8, :] = jnp.where(first, prev_ref[:, cols], tail_scr[:, cols])
        last2 = scr[pl.ds(tm + 6, 2), :]
        tail_scr[:, cols] = last2
        conv_ref[:, cols] = last2
        b = _dot(h, win_ref[:, F + c0:F + c0 + fc])
        u = (cb_ref[:, cols] + scr[pl.ds(6, tm), :] * cw_ref[0:1, cols] + scr[pl.ds(7, tm), :] * cw_ref[1:2, cols]
             + a * cw_ref[2:3, cols])
        y = y + _dot((jax.nn.gelu(u) * b).astype(bf16), wout_ref[cols, :])
    xo = x_ref[...] + g2_ref[...] * y
    if final_norm:
        xo = xo * lax.rsqrt(jnp.mean(xo * xo, axis=-1, keepdims=True) + RMS_EPS) * nf_ref[...]
    o_ref[...] = xo


def ffn_seq(x, sc, sh, nw, w_in, prev, g2, conv_w, conv_b, w_out, normf_w, *, tm, final_norm):
    B, T, D = x.shape
    F = w_out.shape[0]
    assert T % tm == 0 and tm >= CONV_W - 1
    fc = F
    full = lambda a: pl.BlockSpec(a.shape, lambda g, i: (0,) * a.ndim)
    conv_b = conv_b.reshape(1, F)
    normf_w = normf_w.reshape(1, D)
    nw = nw.reshape(1, D)
    rows = pl.BlockSpec((None, tm, D), lambda g, i: (g, i, 0))
    state = pl.BlockSpec((None, CONV_W - 1, F), lambda g, i: (g, 0, 0))
    return pl.pallas_call(
        functools.partial(_ffn_seq_body, final_norm=final_norm),
        grid=(B, T // tm),
        in_specs=[rows, _mod_spec(sc, tm, D), _mod_spec(sh, tm, D), full(nw), _resident(w_in), state,
                  full(conv_w), full(conv_b), _resident(w_out), _mod_spec(g2, tm, D), full(normf_w)],
        out_specs=[rows, state],
        out_shape=[jax.ShapeDtypeStruct((B, T, D), f32), jax.ShapeDtypeStruct((B, CONV_W - 1, F), f32)],
        scratch_shapes=[pltpu.VMEM((tm + 8, fc), f32), pltpu.VMEM((CONV_W - 1, F), f32)],
        compiler_params=_cparams(("parallel", "arbitrary")),
        name="ffn_seq",
    )(x, sc, sh, nw, w_in, prev, conv_w, conv_b, w_out, g2, normf_w)


def _ffn_out_rows_body(a_ref, am1_ref, am2_ref, b_ref, cw_ref, cb_ref, wout_ref, x_ref, g2_ref, nf_ref, o_ref,
                       *, final_norm):
    _ffn_tail(a_ref[...], am1_ref[...], am2_ref[...], b_ref, cw_ref, cb_ref, wout_ref, x_ref, g2_ref, nf_ref, o_ref,
              final_norm)


def ffn_out_rows(ab, am1, am2, x, g2, conv_w, conv_b, w_out, normf_w, *, tm, final_norm):
    G, T, D = x.shape
    F = w_out.shape[0]
    full = lambda a: pl.BlockSpec(a.shape, lambda g, i: (0,) * a.ndim)
    conv_b = conv_b.reshape(1, F)
    normf_w = normf_w.reshape(1, D)
    rowsF = lambda blk: pl.BlockSpec((None, tm, F), lambda g, i: (g, i, blk))
    return pl.pallas_call(
        functools.partial(_ffn_out_rows_body, final_norm=final_norm),
        grid=(G, T // tm),
        in_specs=[rowsF(0), rowsF(0), rowsF(0), rowsF(1),
                  full(conv_w), full(conv_b), full(w_out),
                  pl.BlockSpec((None, tm, D), lambda g, i: (g, i, 0)),
                  _mod_spec(g2, tm, D), full(normf_w)],
        out_specs=pl.BlockSpec((None, tm, D), lambda g, i: (g, i, 0)),
        out_shape=jax.ShapeDtypeStruct((G, T, D), f32),
        compiler_params=_cparams(("parallel", "parallel")),
        name="ffn_out_rows",
    )(ab, am1, am2, ab, conv_w, conv_b, w_out, x, g2, normf_w)


def _prep_w_in(w, w_src):
    D = w.shape[0]
    o = 0
    parts = {}
    for name, n in (("aq", A_Q), ("akv", A_KV), ("ag", A_G), ("rq", B_QK), ("rk", B_QK), ("rv", B_V), ("rg", B_V),
                    ("ga", D), ("gb", D)):
        parts[name] = w[:, o:o + n]
        o += n
    pad = jnp.zeros((D, N_TOK - OFF_AG - A_G), w.dtype)
    tok = jnp.concatenate([parts["rg"], parts["ga"], parts["gb"], parts["rv"], parts["aq"], parts["rq"], parts["rk"],
                           parts["ag"], pad], axis=1)
    return tok.astype(bf16), w_src[:, A_Q:A_Q + A_KV].T.astype(bf16)


def _prep_cmp(cmp_pos, cmp_w1, cmp_w2):
    G = NSA_KV_HEADS
    r = jnp.arange(LANE)
    perm = ((r % CPP)[:, None] * CMP_STRIDE + (r // CPP)[:, None] == r[None, :]).astype(bf16)
    pos = jnp.concatenate([cmp_pos] * G, axis=-1)
    w1 = cmp_w1.reshape(2, CMP_BLOCK, HD, CMP_HIDDEN)
    eye = jnp.eye(G, dtype=w1.dtype)
    w1bd = jnp.einsum("eldh,gk->elgdkh", w1, eye).reshape(2, CMP_BLOCK, G * HD, G * CMP_HIDDEN)
    w1cat = jnp.concatenate([w1bd[:, :CMP_STRIDE], w1bd[:, CMP_STRIDE:]], axis=-1)
    w1cat = w1cat.reshape(2, CMP_STRIDE // 2, 2 * G * HD, 2 * G * CMP_HIDDEN)
    eye_e = jnp.eye(2, dtype=w1.dtype)
    w2bd = jnp.einsum("ehd,gk,ef->eghkfd", cmp_w2, eye, eye_e).reshape(2, G * CMP_HIDDEN, G * 2 * HD)
    return perm, pos, w1cat.astype(bf16), w2bd.astype(bf16)


def _kv_rows_to_out(rowsT, lead):
    t = rowsT.shape[-1]
    r = rowsT.reshape(*lead, 2, NSA_KV_HEADS, HD, t)
    n = len(lead)
    return r.transpose(*range(n), n + 3, n, n + 1, n + 2)


def _keep_last_lanes(a, n):
    t = a.shape[-1]
    if t >= n:
        return a[..., t - n:]
    return jnp.pad(a, ((0, 0),) * (a.ndim - 1) + ((n - t, 0),))


def kernel(x_prompt, x_sample, c_prompt, c_sample, cache_cmp_kv, cache_sel_kv, state_win_kv, state_ret, state_conv,
           page_table, norm1_w, ada_w, ada_b, w_in, cmp_pos, cmp_w1, cmp_w2, w_oa, ret_gn_w, w_ob, w_o, norm2_w,
           ffn_w_in, ffn_conv_w, ffn_conv_b, ffn_w_out, normf_w):
    B, T, D = x_prompt.shape
    Bd, Td, _ = x_sample.shape
    depth = w_in.shape[0]
    n_phys, page = cache_cmp_kv.shape[1], cache_cmp_kv.shape[2]
    past = page_table.shape[1] * page
    wlen = state_win_kv.shape[2]
    F = ffn_w_out.shape[1]
    Rs = Bd * Td
    tm_p = min(TM_PROJ, T)

    mod = ada_mod(jnp.concatenate([c_prompt, c_sample], axis=0), ada_w, ada_b)
    cmpT = cache_cmp_kv.transpose(0, 1, 3, 4, 5, 2).reshape(depth, n_phys, KV_ROWS, page)
    selT = cache_sel_kv.transpose(0, 1, 3, 4, 5, 2).reshape(depth, n_phys, KV_ROWS, page)
    winT = state_win_kv.transpose(0, 1, 3, 4, 5, 2).reshape(depth, Bd, KV_ROWS, wlen)

    w_in_all, w_oa_all, w_ob_all, w_o_all, ffn_in_all, ffn_out_all = (
        w.astype(bf16) for w in (w_in, w_oa, w_ob, w_o, ffn_w_in, ffn_w_out))
    ret_zero = jnp.zeros((1, B, RET_HEADS, RET_DK, RET_DV), f32)

    xp = x_prompt
    xs = x_sample.reshape(1, Rs, D)
    kv_p = None
    ret_p = ret_s = None
    win_p_out, conv_p_out, cmp_s_out, sel_s_out, win_s_out, conv_s_out = [], [], [], [], [], []
    for l in range(depth):
        w_tok, w_kvT = _prep_w_in(w_in_all[l], w_in[l])
        cw = _prep_cmp(cmp_pos[l], cmp_w1[l], cmp_w2[l])
        w_oa_b, w_ob_b, w_o_b = w_oa_all[l], w_ob_all[l], w_o_all[l]
        ffn_in_b, ffn_out_b = ffn_in_all[l], ffn_out_all[l]
        last = l == depth - 1
        mp = [mod[l, :B, k * D:(k + 1) * D].reshape(B, 1, D) for k in range(6)]
        ms = [jnp.repeat(mod[l, B:, k * D:(k + 1) * D], Td, axis=0).reshape(1, Rs, D) for k in range(6)]

        proj, cmp_p, sel_p, win_p = norm_proj_layers(xp, mp[1], mp[0], norm1_w[l], w_tok, w_kvT, l, depth, kv_p,
                                                     tm=tm_p, tn=PROJ_TN)
        kv_p = (cmp_p, sel_p)
        comp = compress_prompt(cmp_p, l, cw)
        attn = nsa_prompt(proj, sel_p, l, win_p, comp)
        x1, ret_p = retention_mix(proj, attn, xp, mp[2], w_oa_b, ret_gn_w[l], w_ob_b, w_o_b, 0, ret_zero, 0, l, depth,
                                  ret_p)
        xp, conv_p = ffn_seq(x1, mp[4], mp[3], norm2_w[l], ffn_in_b, jnp.zeros((B, CONV_W - 1, F), f32), mp[5],
                             ffn_conv_w[l], ffn_conv_b[l], ffn_out_b, normf_w, tm=min(TM_FFN, T), final_norm=last)
        win_p_out.append(_kv_rows_to_out(_keep_last_lanes(win_p, wlen), (B,)))
        conv_p_out.append(conv_p)

        proj_s, kvT_s = norm_proj(xs, ms[1], ms[0], norm1_w[l], w_tok, w_kvT, tm=Rs, tn=PROJ_TN)
        comp_s = compress_paged(cmpT, l, page_table, cw)
        q_s = proj_s[0, :, OFF_AQ:OFF_AQ + A_Q].reshape(Bd, Td, NSA_KV_HEADS, HPG, HD)
        q_s = q_s.transpose(0, 2, 3, 1, 4).reshape(Bd, NSA_HEADS * Td, HD)
        ag_s = proj_s[0, :, OFF_AG:OFF_AG + A_G].reshape(Bd, Td, NSA_KV_HEADS, HPG, 3)
        ag_s = ag_s.transpose(0, 2, 3, 1, 4).reshape(Bd, NSA_HEADS * Td, 3)
        newT = kvT_s[0].reshape(A_KV, Bd, Td).transpose(1, 0, 2)
        newT_pad = jnp.pad(newT, ((0, 0), (0, 0), (0, LANE - Td)))
        attn_s = nsa_sample(q_s, ag_s, comp_s, selT, l, page_table, winT, newT_pad, tn=Td, past=past)
        attn_s = attn_s.reshape(Bd, NSA_KV_HEADS, HPG, Td, HD).transpose(0, 3, 1, 2, 4).reshape(1, Rs, A_Q)
        ro_s, ret_s = retention(proj_s.reshape(Bd, Td, N_TOK), past, state_ret, l, l, depth, ret_s)
        x1s = mix_out(proj_s, attn_s, ro_s.reshape(1, Rs, B_V), xs, ms[2], w_oa_b, ret_gn_w[l], w_ob_b, w_o_b, tm=Rs)
        ab_s = norm_proj(x1s, ms[4], ms[3], norm2_w[l], ffn_in_b, None, tm=Rs, tn=F)
        a_ext = jnp.concatenate([state_conv[l], ab_s[0, :, :F].reshape(Bd, Td, F)], axis=1)
        am1 = a_ext[:, 1:1 + Td].reshape(1, Rs, F)
        am2 = a_ext[:, 0:Td].reshape(1, Rs, F)
        xs = ffn_out_rows(ab_s, am1, am2, x1s, ms[5], ffn_conv_w[l], ffn_conv_b[l], ffn_out_b, normf_w, tm=Rs,
                          final_norm=last)
        cmp_s_out.append(_kv_rows_to_out(newT[:, 0:KV_ROWS], (Bd,)))
        sel_s_out.append(_kv_rows_to_out(newT[:, KV_ROWS:2 * KV_ROWS], (Bd,)))
        if wlen >= Td:
            win_new = jnp.concatenate([winT[l, :, :, Td:], newT[:, 2 * KV_ROWS:]], axis=-1)
        else:
            win_new = newT[:, 2 * KV_ROWS:, Td - wlen:]
        win_s_out.append(_kv_rows_to_out(win_new, (Bd,)))
        conv_s_out.append(a_ext[:, Td:])

    new_cmp_p = _kv_rows_to_out(kv_p[0], (depth, B))
    new_sel_p = _kv_rows_to_out(kv_p[1], (depth, B))
    st = jnp.stack
    return (xp, xs.reshape(Bd, Td, D), new_cmp_p, st(cmp_s_out), new_sel_p, st(sel_s_out), st(win_p_out),
            st(win_s_out), ret_p, ret_s, st(conv_p_out), st(conv_s_out))
```

```python
import functools

import jax
import jax.numpy as jnp
from jax import lax
from jax.experimental import pallas as pl
from jax.experimental.pallas import tpu as pltpu

f32 = jnp.float32
bf16 = jnp.bfloat16

NSA_HEADS = 8
NSA_KV_HEADS = 2
HPG = NSA_HEADS // NSA_KV_HEADS
HD = 64
CMP_STRIDE = 16
CMP_BLOCK = 32
CMP_HIDDEN = 64
SEL_BLOCK = 64
SEL_SHIFT = 6
N_SEL = 16
WINDOW = 512
RET_HEADS = 4
RET_DK = 128
RET_DV = 256
RET_CHUNK = 128
ROPE_BASE = 10000.0
CONV_W = 3
RMS_EPS = 1e-6
GN_EPS = 1e-5
NEG = -1e30
FORCE_BONUS = 1e4

A_Q = NSA_HEADS * HD
KV_ROWS = 2 * NSA_KV_HEADS * HD
A_KV = 3 * KV_ROWS
A_G = 3 * NSA_HEADS
B_QK = RET_HEADS * RET_DK
B_V = RET_HEADS * RET_DV

OFF_RG, OFF_GA, OFF_GB, OFF_RV = 0, 1024, 2048, 3072
OFF_AQ, OFF_RQ, OFF_RK, OFF_AG = 4096, 4608, 5120, 5632
N_TOK = 5760
PROJ_TN = 1920

LANE = 128
VMEM_LIMIT = 56 * 1024 * 1024

TM_PROJ = 512
TM_FFN = 256
TQ_NSA = 256
CK_NSA = 512
RET_CHUNKS_PER_STEP = 4
SAMPLE_ROWS_PER_STEP = 2


def _cparams(sem):
    return pltpu.CompilerParams(dimension_semantics=sem, vmem_limit_bytes=VMEM_LIMIT)


def _dot(a, b):
    return jnp.dot(a, b, preferred_element_type=f32)


def _dot_nt(a, b):
    return lax.dot_general(a, b, (((1,), (1,)), ((), ())), preferred_element_type=f32)


def _dot_tn(a, b):
    return lax.dot_general(a, b, (((0,), (0,)), ((), ())), preferred_element_type=f32)


def _split_dot(a, b_bf16):
    hi = a.astype(bf16)
    lo = (a - hi.astype(f32)).astype(bf16)
    return _dot(hi, b_bf16) + _dot(lo, b_bf16)


def _ada_body(c_ref, w_ref, b_ref, o_ref):
    c = c_ref[...]
    s = c * jax.nn.sigmoid(c)
    o_ref[...] = _dot(s.astype(bf16), w_ref[...].astype(bf16)) + b_ref[...]


def ada_mod(c_all, ada_w, ada_b):
    depth, d, n = ada_w.shape
    r = c_all.shape[0]
    tn = 1024
    return pl.pallas_call(
        _ada_body,
        grid=(depth, n // tn),
        in_specs=[pl.BlockSpec((r, d), lambda l, j: (0, 0)),
                  pl.BlockSpec((None, d, tn), lambda l, j: (l, 0, j)),
                  pl.BlockSpec((None, 1, tn), lambda l, j: (l, 0, j))],
        out_specs=pl.BlockSpec((None, r, tn), lambda l, j: (l, 0, j)),
        out_shape=jax.ShapeDtypeStruct((depth, r, n), f32),
        compiler_params=_cparams(("parallel", "parallel")),
        name="ada_mod",
    )(c_all, ada_w, ada_b.reshape(depth, 1, n))


def _norm_proj_body(x_ref, sc_ref, sh_ref, nw_ref, w_ref, *rest, has_t):
    if has_t:
        wt_ref, o_ref, ot_ref, h_scr = rest
    else:
        o_ref, h_scr = rest

    j = pl.program_id(2)
    tn = o_ref.shape[1]

    @pl.when(j == 0)
    def _():
        h_scr[...] = _norm_mod(x_ref[...], nw_ref, sc_ref, sh_ref)
        if has_t:
            ot_ref[...] = _dot_nt(wt_ref[...], h_scr[...])

    o_ref[...] = _dot(h_scr[...], w_ref[:, pl.ds(pl.multiple_of(j * tn, tn), tn)])


def _norm_mod(x, nw_ref, sc_ref, sh_ref):
    y = x * lax.rsqrt(jnp.mean(x * x, axis=-1, keepdims=True) + RMS_EPS) * nw_ref[...]
    return (y * (1.0 + sc_ref[...]) + sh_ref[...]).astype(bf16)


def _resident(a):
    return pl.BlockSpec(a.shape, lambda *_: (0,) * a.ndim, pipeline_mode=pl.Buffered(1))


def norm_proj(x, sc, sh, nw, w, wt, *, tm, tn):
    G, T, D = x.shape
    N = w.shape[1]
    per_row = sc.shape[1] != 1
    mr = tm if per_row else 1
    mod_spec = pl.BlockSpec((None, mr, D), (lambda g, i, j: (g, i, 0)) if per_row else (lambda g, i, j: (g, 0, 0)))
    in_specs = [pl.BlockSpec((None, tm, D), lambda g, i, j: (g, i, 0)), mod_spec, mod_spec,
                pl.BlockSpec((1, D), lambda g, i, j: (0, 0)),
                _resident(w)]
    out_specs = [pl.BlockSpec((None, tm, tn), lambda g, i, j: (g, i, j))]
    out_shape = [jax.ShapeDtypeStruct((G, T, N), f32)]
    args = [x, sc, sh, nw.reshape(1, D), w]
    if wt is not None:
        NT = wt.shape[0]
        in_specs.append(_resident(wt))
        out_specs.append(pl.BlockSpec((None, NT, tm), lambda g, i, j: (g, 0, i)))
        out_shape.append(jax.ShapeDtypeStruct((G, NT, T), f32))
        args.append(wt)
    res = pl.pallas_call(
        functools.partial(_norm_proj_body, has_t=wt is not None),
        grid=(G, T // tm, N // tn),
        in_specs=in_specs, out_specs=out_specs, out_shape=out_shape,
        scratch_shapes=[pltpu.VMEM((tm, D), bf16)],
        compiler_params=_cparams(("parallel", "parallel", "arbitrary")),
        name="norm_proj_t" if wt is not None else "norm_proj",
    )(*args)
    return res if wt is not None else res[0]


def _norm_proj_layers_body(x_ref, sc_ref, sh_ref, nw_ref, w_ref, wt_ref, *rest):
    o_ref, cmp_ref, sel_ref, win_ref, h_scr = rest[-5:]
    j = pl.program_id(2)
    tn = o_ref.shape[1]

    @pl.when(j == 0)
    def _():
        h_scr[...] = _norm_mod(x_ref[...], nw_ref, sc_ref, sh_ref)
        for b, ref in enumerate((cmp_ref, sel_ref, win_ref)):
            kv = _dot_nt(wt_ref[b * KV_ROWS:(b + 1) * KV_ROWS, :], h_scr[...])
            if len(ref.shape) == 3:
                ref[0] = kv
                if ref.shape[0] > 1:
                    ref[1:] = jnp.zeros((ref.shape[0] - 1,) + kv.shape, f32)
            else:
                ref[...] = kv

    o_ref[...] = _dot(h_scr[...], w_ref[:, pl.ds(pl.multiple_of(j * tn, tn), tn)])


def norm_proj_layers(x, sc, sh, nw, w, wt, layer, depth, prev, *, tm, tn):
    G, T, D = x.shape
    N = w.shape[1]
    mod_spec = pl.BlockSpec((None, 1, D), lambda g, i, j: (g, 0, 0))
    in_specs = [pl.BlockSpec((None, tm, D), lambda g, i, j: (g, i, 0)), mod_spec, mod_spec,
                pl.BlockSpec((1, D), lambda g, i, j: (0, 0)), _resident(w), _resident(wt)]
    args = [x, sc, sh, nw.reshape(1, D), w, wt]
    aliases = {}
    if prev is not None:
        in_specs += [pl.BlockSpec(memory_space=pl.ANY)] * 2
        aliases = {len(args): 1, len(args) + 1: 2}
        args += list(prev)
    if prev is None:
        assert layer == 0
        layered = pl.BlockSpec((depth, None, KV_ROWS, tm), lambda g, i, j: (0, g, 0, i))
    else:
        layered = pl.BlockSpec((None, None, KV_ROWS, tm), lambda g, i, j: (layer, g, 0, i))
    return pl.pallas_call(
        _norm_proj_layers_body,
        grid=(G, T // tm, N // tn),
        in_specs=in_specs,
        out_specs=[pl.BlockSpec((None, tm, tn), lambda g, i, j: (g, i, j)), layered, layered,
                   pl.BlockSpec((None, KV_ROWS, tm), lambda g, i, j: (g, 0, i))],
        out_shape=[jax.ShapeDtypeStruct((G, T, N), f32), jax.ShapeDtypeStruct((depth, G, KV_ROWS, T), f32),
                   jax.ShapeDtypeStruct((depth, G, KV_ROWS, T), f32), jax.ShapeDtypeStruct((G, KV_ROWS, T), f32)],
        input_output_aliases=aliases,
        scratch_shapes=[pltpu.VMEM((tm, D), bf16)],
        compiler_params=_cparams(("parallel", "parallel", "arbitrary")),
        name="norm_proj_layers",
    )(*args)


CPP = LANE // CMP_STRIDE
assert CMP_BLOCK == 2 * CMP_STRIDE


def _compress_pages(get_page, n_pages, perm_ref, pos_ref, w1_ref, w2_ref, o_ref, xs_ref, acc_ref):
    n = o_ref.shape[0]
    perm = perm_ref[...]
    for k in range(n_pages):
        y = _dot_nt(perm, get_page(k).astype(bf16))
        for e in range(2):
            for s in range(CMP_STRIDE):
                xs_ref[e, s, k * CPP:(k + 1) * CPP, :] = y[s * CPP:(s + 1) * CPP, e * LANE:(e + 1) * LANE]
    for e in range(2):
        for s in range(CMP_STRIDE):
            xs_ref[e, s, n:n + 8, :] = jnp.zeros((8, LANE), f32)
            xs_ref[e, s, n + 1:n + 2, :] = pos_ref[e, s:s + 1, :]
            xs_ref[e, s, n + 2:n + 3, :] = pos_ref[e, CMP_STRIDE + s:CMP_STRIDE + s + 1, :]
    out = jnp.zeros((n, 2 * LANE), f32)
    for e in range(2):
        acc = jnp.zeros((n + 8, 2 * LANE), f32)
        for j in range(CMP_STRIDE // 2):
            lhs = jnp.concatenate([xs_ref[e, 2 * j], xs_ref[e, 2 * j + 1]], axis=1).astype(bf16)
            acc = acc + _dot(lhs, w1_ref[e, j])
        acc_ref[...] = acc
        bias = acc_ref[n + 1:n + 2, 0:LANE] + acc_ref[n + 2:n + 3, LANE:2 * LANE]
        hid = acc_ref[0:n, 0:LANE] + acc_ref[pl.ds(1, n), LANE:2 * LANE] + bias
        out = out + _dot(jax.nn.gelu(hid).astype(bf16), w2_ref[e])
    o_ref[...] = out


def _cmp_weight_specs():
    z = lambda nd: (lambda *a: (0,) * nd)
    return [pl.BlockSpec((LANE, LANE), z(2)),
            pl.BlockSpec((2, CMP_BLOCK, LANE), z(3)),
            pl.BlockSpec((2, CMP_STRIDE // 2, 2 * LANE, 2 * LANE), z(4)),
            pl.BlockSpec((2, LANE, 2 * LANE), z(3))]


def _cmp_scratch(n):
    return [pltpu.VMEM((2, CMP_STRIDE, n + 8, LANE), f32), pltpu.VMEM((n + 8, 2 * LANE), f32)]


def _compress_prompt_body(kvT_ref, perm_ref, pos_ref, w1_ref, w2_ref, o_ref, xs_ref, acc_ref):
    n_pages = kvT_ref.shape[1] // LANE
    get = lambda k: kvT_ref[:, k * LANE:(k + 1) * LANE]
    _compress_pages(get, n_pages, perm_ref, pos_ref, w1_ref, w2_ref, o_ref, xs_ref, acc_ref)


def compress_prompt(kvT, layer, cw):
    _, B, _, T = kvT.shape
    assert T % LANE == 0
    n = T // CMP_STRIDE
    return pl.pallas_call(
        _compress_prompt_body,
        grid=(B,),
        in_specs=[pl.BlockSpec((None, None, KV_ROWS, T), lambda b: (layer, b, 0, 0))] + _cmp_weight_specs(),
        out_specs=pl.BlockSpec((None, n, 2 * LANE), lambda b: (b, 0, 0)),
        out_shape=jax.ShapeDtypeStruct((B, n, 2 * LANE), f32),
        scratch_shapes=_cmp_scratch(n),
        compiler_params=_cparams(("parallel",)),
        name="compress_prompt",
    )(kvT, *cw)


def _page_specs(layer, n_pages, page):
    return [pl.BlockSpec((None, None, KV_ROWS, page), lambda b, pt, k=k: (layer, pt[b, k], 0, 0))
            for k in range(n_pages)]


def _compress_paged_body(pt_ref, perm_ref, pos_ref, w1_ref, w2_ref, *rest, n_pages):
    pages, (o_ref, xs_ref, acc_ref) = rest[:n_pages], rest[n_pages:]
    get = lambda k: pages[k][...]
    _compress_pages(get, n_pages, perm_ref, pos_ref, w1_ref, w2_ref, o_ref, xs_ref, acc_ref)


def compress_paged(cacheT, layer, page_table, cw):
    Bd, n_pages = page_table.shape
    page = cacheT.shape[-1]
    assert page == LANE
    L = n_pages * page
    n = L // CMP_STRIDE
    return pl.pallas_call(
        functools.partial(_compress_paged_body, n_pages=n_pages),
        grid_spec=pltpu.PrefetchScalarGridSpec(
            num_scalar_prefetch=1,
            grid=(Bd,),
            in_specs=_cmp_weight_specs() + _page_specs(layer, n_pages, page),
            out_specs=pl.BlockSpec((None, n, 2 * LANE), lambda b, pt: (b, 0, 0)),
            scratch_shapes=_cmp_scratch(n),
        ),
        out_shape=jax.ShapeDtypeStruct((Bd, n, 2 * LANE), f32),
        compiler_params=_cparams(("parallel",)),
        name="compress_paged",
    )(page_table, *cw, *([cacheT] * n_pages))


def _select_blocks(imp, s_ok, forced, n_sel, jj):
    k = min(N_SEL, n_sel)
    score = jnp.where(s_ok, imp + jnp.where(forced, FORCE_BONUS, 0.0), NEG)
    rank = jnp.zeros(score.shape, f32)
    for j2 in range(n_sel):
        col = score[:, j2:j2 + 1]
        ge = jnp.where(col >= score, 1.0, 0.0)
        gt = jnp.where(col > score, 1.0, 0.0)
        rank = rank + jnp.where(jj > j2, ge, gt)
    return jnp.where((rank < k) & s_ok, 1.0, 0.0)


def _select_blocks_t(imp, t0, n_sel):
    tq, sw = imp.shape
    rows = -(-n_sel // 8) * 8
    k = min(N_SEL, n_sel)
    jj = lax.broadcasted_iota(jnp.int32, (rows, tq), 0)
    qpos = t0 + lax.broadcasted_iota(jnp.int32, (rows, tq), 1)
    cur = jnp.right_shift(qpos, SEL_SHIFT)
    s_ok = (jj * SEL_BLOCK <= qpos) & (jj < n_sel)
    forced = (jj == 0) | (jj == cur) | (jj == cur - 1)
    imp_t = jnp.concatenate([imp[c * LANE:(c + 1) * LANE].T for c in range(tq // LANE)], axis=1)[:rows]
    score = jnp.where(s_ok, imp_t + jnp.where(forced, FORCE_BONUS, 0.0), NEG)
    rank = jnp.zeros(score.shape, f32)
    for j2 in range(n_sel):
        row = score[j2:j2 + 1, :]
        ge = jnp.where(row >= score, 1.0, 0.0)
        gt = jnp.where(row > score, 1.0, 0.0)
        rank = rank + jnp.where(jj > j2, ge, gt)
    sel_t = jnp.where((rank < k) & s_ok, 1.0, 0.0)
    sel_t = jnp.concatenate([sel_t, jnp.zeros((sw - rows, tq), f32)], axis=0)
    return jnp.concatenate([sel_t[:, c * LANE:(c + 1) * LANE].T for c in range(tq // LANE)], axis=0)


def _softmax_rows(sm, ok):
    m = jnp.max(sm, axis=-1, keepdims=True)
    e = jnp.where(ok, jnp.exp(sm - m), 0.0)
    den = jnp.sum(e, axis=-1, keepdims=True)
    return e / jnp.maximum(den, 1e-30)


def _pv_and_rowsum(e, vT):
    v_aug = jnp.concatenate([vT, jnp.ones(vT.shape, vT.dtype)], axis=0)
    return _dot_nt(e.astype(bf16), v_aug)


def _normalise(acc):
    return (acc / pltpu.roll(acc, HD, axis=1))[:, :HD]


def _nsa_prompt_body(q_ref, ag_ref, comp_ref, ksT_ref, vsT_ref, kwT_ref, vwT_ref, ov_ref, ex_ref, o_ref,
                     *, tq, n_cmp, n_sel, wk, ck):
    i = pl.program_id(1)
    t0 = i * tq
    scale = HD ** -0.5
    NC = comp_ref.shape[0]
    SW = ov_ref.shape[1]
    qpos = t0 + lax.broadcasted_iota(jnp.int32, (tq, 1), 0)
    col_n = lax.broadcasted_iota(jnp.int32, (tq, NC), 1)
    c_ok = ((col_n * CMP_STRIDE + (CMP_BLOCK - 1)) <= qpos) & (col_n < n_cmp)
    jj = lax.broadcasted_iota(jnp.int32, (tq, SW), 1)
    s_ok = (jj * SEL_BLOCK <= qpos) & (jj < n_sel)
    wstart = pl.multiple_of(jnp.maximum(t0 + tq - wk, 0), LANE)
    kposw = wstart + lax.broadcasted_iota(jnp.int32, (tq, wk), 1)
    rel = qpos - kposw
    w_bias4 = jnp.concatenate([jnp.where((rel >= 0) & (rel < WINDOW), 0.0, NEG)] * HPG, axis=0)
    c_ok4 = jnp.concatenate([c_ok] * HPG, axis=0)
    sg = jax.nn.sigmoid(ag_ref[...])
    n_chunks = (t0 + tq + ck - 1) // ck
    kcol = lax.broadcasted_iota(jnp.int32, (tq, ck), 1)
    all_valid_selected = t0 + tq <= min(N_SEL, n_sel) * SEL_BLOCK

    GS = range(NSA_KV_HEADS)
    rows = lambda g: slice(g * HD, (g + 1) * HD)
    kc = [comp_ref[:, g * LANE:g * LANE + HD].astype(bf16) for g in GS]
    vc = [comp_ref[:, g * LANE + HD:(g + 1) * LANE].astype(bf16) for g in GS]
    q4 = [jnp.concatenate([(q_ref[:, (g * HPG + h) * HD:(g * HPG + h + 1) * HD] * scale).astype(bf16)
                           for h in range(HPG)], axis=0) for g in GS]
    kw = [kwT_ref[rows(g), pl.ds(wstart, wk)].astype(bf16) for g in GS]
    vw = [vwT_ref[rows(g), pl.ds(wstart, wk)].astype(bf16) for g in GS]
    s_c = [_dot_nt(q4[g], kc[g]) for g in GS]
    s_w = [_dot(q4[g], kw[g]) + w_bias4 for g in GS]
    p_c = [_softmax_rows(jnp.where(c_ok4, s_c[g], NEG), c_ok4) for g in GS]
    e_w = [jnp.exp(s_w[g] - jnp.max(s_w[g], axis=-1, keepdims=True)) for g in GS]
    o_c = [_dot(p_c[g].astype(bf16), vc[g]) for g in GS]
    o_w = [_normalise(_pv_and_rowsum(e_w[g], vw[g])) for g in GS]
    imp = []
    for g in GS:
        psum = p_c[g][0:tq]
        for h in range(1, HPG):
            psum = psum + p_c[g][h * tq:(h + 1) * tq]
        imp.append(_split_dot(psum, ov_ref[...]))
    sel = lax.cond(all_valid_selected,
                   lambda: tuple(jnp.where(s_ok, 1.0, 0.0) for _ in GS),
                   lambda: tuple(_select_blocks_t(imp[g], t0, n_sel) for g in GS))
    zq = jnp.zeros((HPG * tq, SW - HD), bf16)
    q_aug = [jnp.concatenate([jnp.concatenate([jnp.where(sel[g] > 0.5, 0.0, NEG).astype(bf16)] * HPG, axis=0),
                              q4[g], zq], axis=1) for g in GS]
    zk = jnp.zeros((SW - HD, ck), bf16)

    def chunk_step(c, carry):
        k0 = pl.multiple_of(c * ck, ck)
        cbias = jnp.concatenate([jnp.where(k0 + kcol <= qpos, 0.0, NEG)] * HPG, axis=0)
        ex = ex_ref[:, pl.ds(k0, ck)]
        k_aug = [jnp.concatenate([ex, ksT_ref[rows(g), pl.ds(k0, ck)].astype(bf16), zk], axis=0) for g in GS]
        vT = [vsT_ref[rows(g), pl.ds(k0, ck)].astype(bf16) for g in GS]
        s = [_dot(q_aug[g], k_aug[g]) + cbias for g in GS]
        m_new = [jnp.maximum(carry[g][0], jnp.max(s[g], axis=-1, keepdims=True)) for g in GS]
        e = [jnp.exp(s[g] - m_new[g]) for g in GS]
        pv = [_pv_and_rowsum(e[g], vT[g]) for g in GS]
        return tuple((m_new[g], jnp.exp(carry[g][0] - m_new[g]) * carry[g][1] + pv[g]) for g in GS)

    init = tuple((jnp.full((HPG * tq, 1), NEG, f32), jnp.zeros((HPG * tq, 2 * HD), f32)) for _ in GS)
    state = lax.fori_loop(0, n_chunks, chunk_step, init)

    for g in GS:
        gate = lambda br: jnp.concatenate(
            [sg[:, 3 * (g * HPG + h) + br:3 * (g * HPG + h) + br + 1] for h in range(HPG)], axis=0)
        o = gate(0) * o_c[g] + gate(1) * _normalise(state[g][1]) + gate(2) * o_w[g]
        for h in range(HPG):
            o_ref[:, (g * HPG + h) * HD:(g * HPG + h + 1) * HD] = o[h * tq:(h + 1) * tq]


def _overlap_matrix(nc_rows, n_cmp, sw_cols, n_sel):
    cs = jnp.arange(nc_rows) * CMP_STRIDE
    ss = jnp.arange(sw_cols) * SEL_BLOCK
    ok = (cs[:, None] < ss[None, :] + SEL_BLOCK) & (cs[:, None] + CMP_BLOCK > ss[None, :])
    ok = ok & (jnp.arange(nc_rows)[:, None] < n_cmp) & (jnp.arange(sw_cols)[None, :] < n_sel)
    return ok.astype(bf16)


def nsa_prompt(proj, selT, layer, winT, comp):
    B, T, _ = proj.shape
    tq = min(TQ_NSA, T)
    n_cmp = T // CMP_STRIDE - (CMP_BLOCK // CMP_STRIDE) + 1
    n_sel = -(-T // SEL_BLOCK)
    NC = comp.shape[1]
    SW = LANE
    assert n_sel <= SW and T % tq == 0
    wk = min(WINDOW + tq, T)
    ck = min(CK_NSA, T)
    assert T % ck == 0
    ov = _overlap_matrix(NC, n_cmp, SW, n_sel)
    ex = (jnp.arange(SW)[:, None] == (jnp.arange(T)[None, :] // SEL_BLOCK)).astype(bf16)
    sel_spec = lambda blk: pl.BlockSpec((None, None, LANE, T), lambda b, i: (layer, b, blk, 0))
    win_spec = lambda blk: pl.BlockSpec((None, LANE, T), lambda b, i: (b, blk, 0))
    return pl.pallas_call(
        functools.partial(_nsa_prompt_body, tq=tq, n_cmp=n_cmp, n_sel=n_sel, wk=wk, ck=ck),
        grid=(B, T // tq),
        in_specs=[pl.BlockSpec((None, tq, A_Q), lambda b, i: (b, i, OFF_AQ // A_Q)),
                  pl.BlockSpec((None, tq, LANE), lambda b, i: (b, i, OFF_AG // LANE)),
                  pl.BlockSpec((None, NC, 2 * LANE), lambda b, i: (b, 0, 0)),
                  sel_spec(0), sel_spec(1), win_spec(0), win_spec(1),
                  pl.BlockSpec((NC, SW), lambda b, i: (0, 0)),
                  pl.BlockSpec((SW, T), lambda b, i: (0, 0))],
        out_specs=pl.BlockSpec((None, tq, A_Q), lambda b, i: (b, i, 0)),
        out_shape=jax.ShapeDtypeStruct((B, T, A_Q), f32),
        compiler_params=_cparams(("parallel", "arbitrary")),
        name="nsa_prompt",
    )(proj, proj, comp, selT, selT, winT, winT, ov, ex)


def _nsa_sample_body(pt_ref, q_ref, ag_ref, comp_ref, kwT_ref, newT_ref, ov_ref, rsum_ref, rexp_ref, ex_ref, *rest,
                     tn, past, n_cmp, n_sel, n_pages):
    RB = q_ref.shape[0]
    pages, o_ref = rest[:RB * n_pages], rest[RB * n_pages]
    scale = HD ** -0.5
    R = HPG * tn
    NC = comp_ref.shape[1]
    SW = ov_ref.shape[1]
    tok = lax.rem(lax.broadcasted_iota(jnp.int32, (R, 1), 0), tn)
    qpos = past + tok
    col_n = lax.broadcasted_iota(jnp.int32, (R, NC), 1)
    c_ok = ((col_n * CMP_STRIDE + (CMP_BLOCK - 1)) <= qpos) & (col_n < n_cmp)
    jj = lax.broadcasted_iota(jnp.int32, (tn, SW), 1)
    qpos_t = past + lax.broadcasted_iota(jnp.int32, (tn, 1), 0)
    cur = jnp.right_shift(qpos_t, SEL_SHIFT)
    s_ok = (jj * SEL_BLOCK <= qpos_t) & (jj < n_sel)
    forced = (jj == 0) | (jj == cur) | (jj == cur - 1)
    wlen = kwT_ref.shape[2]
    kposw = (past - wlen) + lax.broadcasted_iota(jnp.int32, (R, wlen), 1)
    relw = qpos - kposw
    w_ok = (relw >= 0) & (relw < WINDOW)
    nl = newT_ref.shape[2]
    u = lax.broadcasted_iota(jnp.int32, (R, nl), 1)
    reln = tok - u
    n_ok = (reln >= 0) & (reln < WINDOW) & (u < tn)
    nb = past // SEL_BLOCK
    chains = [(r, g) for r in range(RB) for g in range(NSA_KV_HEADS)]
    each = lambda f: [f(r, g) for r, g in chains]
    krow = lambda g: slice(g * HD, (g + 1) * HD)
    vrow = lambda g: slice(2 * HD + g * HD, 2 * HD + (g + 1) * HD)
    new = lambda r, branch, rows: newT_ref[r, branch * KV_ROWS + rows.start:branch * KV_ROWS + rows.stop, :]

    def masked_attention(qs, parts_of):
        s = [[jnp.where(ok, _dot(q, kT), NEG) for kT, _, ok in parts] for q, parts in zip(qs, parts_of)]
        m = [functools.reduce(jnp.maximum, [jnp.max(x, axis=-1, keepdims=True) for x in si]) for si in s]
        e = [[jnp.exp(x - mi) for x in si] for si, mi in zip(s, m)]
        den = [sum(jnp.sum(x, axis=-1, keepdims=True) for x in ei) for ei in e]
        return [sum(_dot_nt(x.astype(bf16), vT) for x, (_, vT, _) in zip(ei, parts)) / di
                for ei, parts, di in zip(e, parts_of, den)]

    q = each(lambda r, g: (q_ref[r, g * R:(g + 1) * R, :] * scale).astype(bf16))
    kc = each(lambda r, g: comp_ref[r, :, g * LANE:g * LANE + HD].astype(bf16))
    vc = each(lambda r, g: comp_ref[r, :, g * LANE + HD:(g + 1) * LANE].astype(bf16))
    n = range(len(chains))
    pc = [_softmax_rows(jnp.where(c_ok, _dot_nt(q[i], kc[i]), NEG), c_ok) for i in n]
    o_c = [_dot(pc[i].astype(bf16), vc[i]) for i in n]
    imp = [_split_dot(_rsum_exact(rsum_ref, pc[i]), ov_ref[...]) for i in n]
    sel = [_select_blocks(imp[i], s_ok, forced, n_sel, jj) for i in n]
    selr = [_dot(rexp_ref[...], sel[i].astype(bf16)) for i in n]
    o_w = masked_attention(q, [[(kwT_ref[r, krow(g), :].astype(bf16), kwT_ref[r, vrow(g), :].astype(bf16), w_ok),
                                (new(r, 2, krow(g)).astype(bf16), new(r, 2, vrow(g)).astype(bf16), n_ok)]
                               for r, g in chains])
    parts_of = []
    for i, (r, g) in enumerate(chains):
        pgs = pages[r * n_pages:(r + 1) * n_pages]
        ksp = jnp.concatenate([pg[krow(g), :] for pg in pgs], axis=1).astype(bf16)
        vsp = jnp.concatenate([pg[vrow(g), :] for pg in pgs], axis=1).astype(bf16)
        okp = _dot(selr[i].astype(bf16), ex_ref[...]) > 0.5
        okn = (u <= tok) & (u < tn) & (selr[i][:, nb:nb + 1] > 0.5)
        parts_of.append([(ksp, vsp, okp), (new(r, 1, krow(g)).astype(bf16), new(r, 1, vrow(g)).astype(bf16), okn)])
    o_s = masked_attention(q, parts_of)
    for i, (r, g) in enumerate(chains):
        gg = jax.nn.sigmoid(ag_ref[r, g * R:(g + 1) * R, :])
        o_ref[r, g * R:(g + 1) * R, :] = gg[:, 0:1] * o_c[i] + gg[:, 1:2] * o_s[i] + gg[:, 2:3] * o_w[i]


def _rsum_exact(rsum_ref, pc):
    r = rsum_ref[...]
    hi = pc.astype(bf16)
    lo = (pc - hi.astype(f32)).astype(bf16)
    return _dot(r, hi) + _dot(r, lo)


def nsa_sample(q_s, ag_s, comp, selT, layer, page_table, winT, newT, *, tn, past):
    Bd, n_pages = page_table.shape
    page = selT.shape[-1]
    R = HPG * tn
    L = past + tn
    n_cmp = L // CMP_STRIDE - (CMP_BLOCK // CMP_STRIDE) + 1
    n_sel = -(-L // SEL_BLOCK)
    NC = comp.shape[1]
    SW = -(-n_sel // LANE) * LANE
    assert past % page == 0 and page % SEL_BLOCK == 0 and tn <= SEL_BLOCK and n_cmp <= NC
    wlen = winT.shape[-1]
    ov = _overlap_matrix(NC, n_cmp, SW, n_sel)
    rsum = (jnp.arange(tn)[:, None] == (jnp.arange(R)[None, :] % tn)).astype(bf16)
    ex = (jnp.arange(SW)[:, None] == (jnp.arange(past)[None, :] // SEL_BLOCK)).astype(bf16)
    const = lambda *shape: pl.BlockSpec(shape, lambda b, pt: (0,) * len(shape))
    RB = SAMPLE_ROWS_PER_STEP if Bd % SAMPLE_ROWS_PER_STEP == 0 else 1
    page_specs = [pl.BlockSpec((None, None, KV_ROWS, page), lambda b, pt, r=r, k=k: (layer, pt[b * RB + r, k], 0, 0))
                  for r in range(RB) for k in range(n_pages)]
    return pl.pallas_call(
        functools.partial(_nsa_sample_body, tn=tn, past=past, n_cmp=n_cmp, n_sel=n_sel, n_pages=n_pages),
        grid_spec=pltpu.PrefetchScalarGridSpec(
            num_scalar_prefetch=1,
            grid=(Bd // RB,),
            in_specs=[pl.BlockSpec((RB, 2 * R, HD), lambda b, pt: (b, 0, 0)),
                      pl.BlockSpec((RB, 2 * R, 3), lambda b, pt: (b, 0, 0)),
                      pl.BlockSpec((RB, NC, 2 * LANE), lambda b, pt: (b, 0, 0)),
                      pl.BlockSpec((None, RB, KV_ROWS, wlen), lambda b, pt: (layer, b, 0, 0)),
                      pl.BlockSpec((RB, A_KV, LANE), lambda b, pt: (b, 0, 0)),
                      const(NC, SW), const(tn, R), const(R, tn), _resident(ex)]
            + page_specs,
            out_specs=pl.BlockSpec((RB, 2 * R, HD), lambda b, pt: (b, 0, 0)),
        ),
        out_shape=jax.ShapeDtypeStruct((Bd, 2 * R, HD), f32),
        compiler_params=_cparams(("parallel",)),
        name="nsa_sample",
    )(page_table, q_s, ag_s, comp, winT, newT, ov, rsum, rsum.T, ex, *([selT] * (RB * n_pages)))


def _retention_body(*refs):
    _retention_step(*refs[:10], *refs[-3:])


def _retention_mix_body(*refs):
    o_ref, sn_ref, s_scr, ro_scr = refs[-4:]
    _retention_step(*refs[:10], ro_scr, sn_ref, s_scr)
    rg_ref, ga_ref, gb_ref, attn_ref, x_ref, g1_ref, woa_ref, gnw_ref, wob_ref, wo_ref = refs[10:20]
    _mix_out_body(rg_ref, ga_ref, gb_ref, attn_ref, ro_scr, x_ref, g1_ref, woa_ref, gnw_ref, wob_ref, wo_ref, o_ref)


def _retention_step(q_ref, k_ref, v_ref, cos_ref, sin_ref, dm_ref, qd_ref, kd_ref, sd_ref, s0_ref, o_ref, sn_ref, s_scr):
    c = pl.program_id(1)

    @pl.when(c == 0)
    def _():
        s_scr[...] = s0_ref[...]

    C = dm_ref.shape[1]
    for cc in range(q_ref.shape[0] // C):
        r = slice(cc * C, (cc + 1) * C)
        cos = cos_ref[r, :]
        sin = sin_ref[r, :]

        def rot(x):
            return x * cos + pltpu.roll(x, RET_DK // 2, axis=1) * sin

        for h in range(RET_HEADS):
            q = rot(q_ref[r, h * RET_DK:(h + 1) * RET_DK])
            k = rot(k_ref[r, h * RET_DK:(h + 1) * RET_DK]) * (RET_DK ** -0.5)
            v = v_ref[r, h * RET_DV:(h + 1) * RET_DV].astype(bf16)
            S = s_scr[h]
            inner = _dot_nt(q.astype(bf16), k.astype(bf16)) * dm_ref[h]
            o_ref[r, h * RET_DV:(h + 1) * RET_DV] = (_dot(inner.astype(bf16), v)
                                                     + _dot((q * qd_ref[h]).astype(bf16), S.astype(bf16)))
            s_scr[h] = S * sd_ref[h] + _dot_tn((k * kd_ref[h]).astype(bf16), v)

    @pl.when(c == pl.num_programs(1) - 1)
    def _():
        if len(sn_ref.shape) == 4:
            sn_ref[0] = s_scr[...]
            if sn_ref.shape[0] > 1:
                sn_ref[1:] = jnp.zeros((sn_ref.shape[0] - 1,) + s_scr.shape, f32)
        else:
            sn_ref[...] = s_scr[...]


def retention(proj, pos0, states, layer, out_layer, depth, prev_new):
    B, T, _ = proj.shape
    R, in_specs, args = _retention_inputs(proj, pos0, states, layer)
    st, alias_spec = _layer_slot_spec(depth, out_layer, prev_new)
    aliases = {}
    if prev_new is not None:
        in_specs.append(alias_spec)
        aliases = {len(args): 1}
        args.append(prev_new)
    return pl.pallas_call(
        _retention_body,
        grid=(B, T // R),
        in_specs=in_specs,
        out_specs=[pl.BlockSpec((None, R, B_V), lambda b, c: (b, c, 0)), st],
        out_shape=[jax.ShapeDtypeStruct((B, T, B_V), f32),
                   jax.ShapeDtypeStruct((depth, B, RET_HEADS, RET_DK, RET_DV), f32)],
        input_output_aliases=aliases,
        scratch_shapes=[pltpu.VMEM((RET_HEADS, RET_DK, RET_DV), f32)],
        compiler_params=_cparams(("parallel", "arbitrary")),
        name="retention",
    )(*args)


def _retention_inputs(proj, pos0, states, layer):
    B, T, _ = proj.shape
    C = RET_CHUNK if T % RET_CHUNK == 0 else T
    n = T // C
    H = RET_HEADS
    half = RET_DK // 2
    inv = jnp.exp(-jnp.log(ROPE_BASE) * jnp.arange(half, dtype=f32) / half)
    ang = (pos0 + jnp.arange(T)).astype(f32)[:, None] * inv[None, :]
    cos = jnp.concatenate([jnp.cos(ang), jnp.cos(ang)], axis=-1)
    sin = jnp.concatenate([-jnp.sin(ang), jnp.sin(ang)], axis=-1)
    log_g = jnp.log(1.0 - jnp.exp2(-5.0 - jnp.arange(H, dtype=f32)))
    i = jnp.arange(C, dtype=f32)
    diff = i[:, None] - i[None, :]
    dm = jnp.where(diff >= 0, jnp.exp(jnp.maximum(diff, 0.0)[None] * log_g[:, None, None]), 0.0)
    qd = jnp.exp((i + 1.0)[None, :] * log_g[:, None])[..., None]
    kd = jnp.exp((C - 1.0 - i)[None, :] * log_g[:, None])[..., None]
    sd = jnp.exp(C * log_g)[:, None, None]
    tab = lambda a: pl.BlockSpec(a.shape, lambda b, c: (0, 0, 0))
    cps = RET_CHUNKS_PER_STEP if n % RET_CHUNKS_PER_STEP == 0 else 1
    R = cps * C
    in_specs = [pl.BlockSpec((None, R, B_QK), lambda b, c: (b, c, OFF_RQ // B_QK)),
                pl.BlockSpec((None, R, B_QK), lambda b, c: (b, c, OFF_RK // B_QK)),
                pl.BlockSpec((None, R, B_V), lambda b, c: (b, c, OFF_RV // B_V)),
                pl.BlockSpec((R, RET_DK), lambda b, c: (c, 0)),
                pl.BlockSpec((R, RET_DK), lambda b, c: (c, 0)),
                tab(dm), tab(qd), tab(kd), tab(sd),
                pl.BlockSpec((None, None, H, RET_DK, RET_DV), lambda b, c: (layer, b, 0, 0, 0))]
    return R, in_specs, [proj, proj, proj, cos, sin, dm, qd, kd, sd, states]


def _layer_slot_spec(depth, out_layer, prev_new):
    tail = (RET_HEADS, RET_DK, RET_DV)
    if prev_new is None:
        assert out_layer == 0
        return pl.BlockSpec((depth, None) + tail, lambda b, c: (0, b, 0, 0, 0)), None
    return (pl.BlockSpec((None, None) + tail, lambda b, c: (out_layer, b, 0, 0, 0)),
            pl.BlockSpec(memory_space=pl.ANY))


def retention_mix(proj, attn, x, g1, w_oa, gn_w, w_ob, w_o, pos0, states, layer, out_layer, depth, prev_new):
    B, T, D = x.shape
    R, in_specs, args = _retention_inputs(proj, pos0, states, layer)
    gn_w = gn_w.reshape(1, B_V)
    wide = lambda blk: pl.BlockSpec((None, R, B_V), lambda b, c: (b, c, blk))
    in_specs += [wide(OFF_RG // B_V), wide(OFF_GA // B_V), wide(OFF_GB // B_V),
                 pl.BlockSpec((None, R, A_Q), lambda b, c: (b, c, 0)),
                 pl.BlockSpec((None, R, D), lambda b, c: (b, c, 0)),
                 _mod_spec(g1, R, D), _resident(w_oa), _resident(gn_w), _resident(w_ob), _resident(w_o)]
    args += [proj, proj, proj, attn, x, g1, w_oa, gn_w, w_ob, w_o]
    st, alias_spec = _layer_slot_spec(depth, out_layer, prev_new)
    aliases = {}
    if prev_new is not None:
        in_specs.append(alias_spec)
        aliases = {len(args): 1}
        args.append(prev_new)
    return pl.pallas_call(
        _retention_mix_body,
        grid=(B, T // R),
        in_specs=in_specs,
        out_specs=[pl.BlockSpec((None, R, D), lambda b, c: (b, c, 0)), st],
        out_shape=[jax.ShapeDtypeStruct((B, T, D), f32),
                   jax.ShapeDtypeStruct((depth, B, RET_HEADS, RET_DK, RET_DV), f32)],
        input_output_aliases=aliases,
        scratch_shapes=[pltpu.VMEM((RET_HEADS, RET_DK, RET_DV), f32), pltpu.VMEM((R, B_V), f32)],
        compiler_params=_cparams(("parallel", "arbitrary")),
        name="retention_mix",
    )(*args)


def _mix_out_body(rg_ref, ga_ref, gb_ref, attn_ref, ro_ref, x_ref, g1_ref, woa_ref, gnw_ref, wob_ref, wo_ref, o_ref):
    ya = _dot(attn_ref[...].astype(bf16), woa_ref[...])
    parts = []
    for h in range(RET_HEADS):
        r = ro_ref[:, h * RET_DV:(h + 1) * RET_DV]
        d = r - jnp.mean(r, axis=-1, keepdims=True)
        var = jnp.mean(d * d, axis=-1, keepdims=True)
        parts.append(d * lax.rsqrt(var + GN_EPS) * gnw_ref[:, h * RET_DV:(h + 1) * RET_DV])
    ron = jnp.concatenate(parts, axis=-1)
    rg = rg_ref[...]
    yb = _dot((rg * jax.nn.sigmoid(rg) * ron).astype(bf16), wob_ref[...])
    m = jax.nn.sigmoid(ga_ref[...]) * ya + jax.nn.sigmoid(gb_ref[...]) * yb
    o_ref[...] = x_ref[...] + g1_ref[...] * _dot(m.astype(bf16), wo_ref[...])


def _mod_spec(mod, tm, D):
    if mod.shape[1] != 1:
        return pl.BlockSpec((None, tm, D), lambda g, i: (g, i, 0))
    return pl.BlockSpec((None, 1, D), lambda g, i: (g, 0, 0))


def mix_out(proj, attn, ro, x, g1, w_oa, gn_w, w_ob, w_o, *, tm):
    G, T, D = x.shape
    full = lambda a: pl.BlockSpec(a.shape, lambda g, i: (0,) * a.ndim)
    wide = lambda blk: pl.BlockSpec((None, tm, B_V), lambda g, i: (g, i, blk))
    gn_w = gn_w.reshape(1, B_V)
    return pl.pallas_call(
        _mix_out_body,
        grid=(G, T // tm),
        in_specs=[wide(OFF_RG // B_V), wide(OFF_GA // B_V), wide(OFF_GB // B_V),
                  pl.BlockSpec((None, tm, A_Q), lambda g, i: (g, i, 0)),
                  wide(0),
                  pl.BlockSpec((None, tm, D), lambda g, i: (g, i, 0)),
                  _mod_spec(g1, tm, D), full(w_oa), full(gn_w), full(w_ob), full(w_o)],
        out_specs=pl.BlockSpec((None, tm, D), lambda g, i: (g, i, 0)),
        out_shape=jax.ShapeDtypeStruct((G, T, D), f32),
        compiler_params=_cparams(("parallel", "parallel")),
        name="mix_out",
    )(proj, proj, proj, attn, ro, x, g1, w_oa, gn_w, w_ob, w_o)


def _ffn_tail(a, am1, am2, b_ref, cw_ref, cb_ref, wout_ref, x_ref, g2_ref, nf_ref, o_ref, final_norm):
    u = cb_ref[...] + am2 * cw_ref[0:1, :] + am1 * cw_ref[1:2, :] + a * cw_ref[2:3, :]
    y = _dot((jax.nn.gelu(u) * b_ref[...]).astype(bf16), wout_ref[...])
    xo = x_ref[...] + g2_ref[...] * y
    if final_norm:
        xo = xo * lax.rsqrt(jnp.mean(xo * xo, axis=-1, keepdims=True) + RMS_EPS) * nf_ref[...]
    o_ref[...] = xo


def _ffn_seq_body(x_ref, sc_ref, sh_ref, nw_ref, win_ref, prev_ref, cw_ref, cb_ref, wout_ref, g2_ref, nf_ref,
                  o_ref, conv_ref, scr, tail_scr, *, final_norm):
    tm = x_ref.shape[0]
    F = wout_ref.shape[0]
    fc = scr.shape[1]
    h = _norm_mod(x_ref[...], nw_ref, sc_ref, sh_ref)
    first = pl.program_id(1) == 0
    y = jnp.zeros(o_ref.shape, f32)
    for c0 in range(0, F, fc):
        cols = slice(c0, c0 + fc)
        a = _dot(h, win_ref[:, cols])
        scr[8:8 + tm, :] = a
        scr[6:8, :] = jnp.where(first, prev_ref[:, cols], tail_scr[:, cols])
        last2 = scr[pl.ds(tm + 6, 2), :]
        tail_scr[:, cols] = last2
        conv_ref[:, cols] = last2
        b = _dot(h, win_ref[:, F + c0:F + c0 + fc])
        u = (cb_ref[:, cols] + scr[pl.ds(6, tm), :] * cw_ref[0:1, cols] + scr[pl.ds(7, tm), :] * cw_ref[1:2, cols]
             + a * cw_ref[2:3, cols])
        y = y + _dot((jax.nn.gelu(u) * b).astype(bf16), wout_ref[cols, :])
    xo = x_ref[...] + g2_ref[...] * y
    if final_norm:
        xo = xo * lax.rsqrt(jnp.mean(xo * xo, axis=-1, keepdims=True) + RMS_EPS) * nf_ref[...]
    o_ref[...] = xo


def ffn_seq(x, sc, sh, nw, w_in, prev, g2, conv_w, conv_b, w_out, normf_w, *, tm, final_norm):
    B, T, D = x.shape
    F = w_out.shape[0]
    assert T % tm == 0 and tm >= CONV_W - 1
    fc = F
    full = lambda a: pl.BlockSpec(a.shape, lambda g, i: (0,) * a.ndim)
    conv_b = conv_b.reshape(1, F)
    normf_w = normf_w.reshape(1, D)
    nw = nw.reshape(1, D)
    rows = pl.BlockSpec((None, tm, D), lambda g, i: (g, i, 0))
    state = pl.BlockSpec((None, CONV_W - 1, F), lambda g, i: (g, 0, 0))
    return pl.pallas_call(
        functools.partial(_ffn_seq_body, final_norm=final_norm),
        grid=(B, T // tm),
        in_specs=[rows, _mod_spec(sc, tm, D), _mod_spec(sh, tm, D), full(nw), _resident(w_in), state,
                  full(conv_w), full(conv_b), _resident(w_out), _mod_spec(g2, tm, D), full(normf_w)],
        out_specs=[rows, state],
        out_shape=[jax.ShapeDtypeStruct((B, T, D), f32), jax.ShapeDtypeStruct((B, CONV_W - 1, F), f32)],
        scratch_shapes=[pltpu.VMEM((tm + 8, fc), f32), pltpu.VMEM((CONV_W - 1, F), f32)],
        compiler_params=_cparams(("parallel", "arbitrary")),
        name="ffn_seq",
    )(x, sc, sh, nw, w_in, prev, conv_w, conv_b, w_out, g2, normf_w)


def _ffn_out_rows_body(a_ref, am1_ref, am2_ref, b_ref, cw_ref, cb_ref, wout_ref, x_ref, g2_ref, nf_ref, o_ref,
                       *, final_norm):
    _ffn_tail(a_ref[...], am1_ref[...], am2_ref[...], b_ref, cw_ref, cb_ref, wout_ref, x_ref, g2_ref, nf_ref, o_ref,
              final_norm)


def ffn_out_rows(ab, am1, am2, x, g2, conv_w, conv_b, w_out, normf_w, *, tm, final_norm):
    G, T, D = x.shape
    F = w_out.shape[0]
    full = lambda a: pl.BlockSpec(a.shape, lambda g, i: (0,) * a.ndim)
    conv_b = conv_b.reshape(1, F)
    normf_w = normf_w.reshape(1, D)
    rowsF = lambda blk: pl.BlockSpec((None, tm, F), lambda g, i: (g, i, blk))
    return pl.pallas_call(
        functools.partial(_ffn_out_rows_body, final_norm=final_norm),
        grid=(G, T // tm),
        in_specs=[rowsF(0), rowsF(0), rowsF(0), rowsF(1),
                  full(conv_w), full(conv_b), full(w_out),
                  pl.BlockSpec((None, tm, D), lambda g, i: (g, i, 0)),
                  _mod_spec(g2, tm, D), full(normf_w)],
        out_specs=pl.BlockSpec((None, tm, D), lambda g, i: (g, i, 0)),
        out_shape=jax.ShapeDtypeStruct((G, T, D), f32),
        compiler_params=_cparams(("parallel", "parallel")),
        name="ffn_out_rows",
    )(ab, am1, am2, ab, conv_w, conv_b, w_out, x, g2, normf_w)


def _prep_w_in(w, w_src):
    D = w.shape[0]
    o = 0
    parts = {}
    for name, n in (("aq", A_Q), ("akv", A_KV), ("ag", A_G), ("rq", B_QK), ("rk", B_QK), ("rv", B_V), ("rg", B_V),
                    ("ga", D), ("gb", D)):
        parts[name] = w[:, o:o + n]
        o += n
    pad = jnp.zeros((D, N_TOK - OFF_AG - A_G), w.dtype)
    tok = jnp.concatenate([parts["rg"], parts["ga"], parts["gb"], parts["rv"], parts["aq"], parts["rq"], parts["rk"],
                           parts["ag"], pad], axis=1)
    return tok.astype(bf16), w_src[:, A_Q:A_Q + A_KV].T.astype(bf16)


def _prep_cmp(cmp_pos, cmp_w1, cmp_w2):
    G = NSA_KV_HEADS
    r = jnp.arange(LANE)
    perm = ((r % CPP)[:, None] * CMP_STRIDE + (r // CPP)[:, None] == r[None, :]).astype(bf16)
    pos = jnp.concatenate([cmp_pos] * G, axis=-1)
    w1 = cmp_w1.reshape(2, CMP_BLOCK, HD, CMP_HIDDEN)
    eye = jnp.eye(G, dtype=w1.dtype)
    w1bd = jnp.einsum("eldh,gk->elgdkh", w1, eye).reshape(2, CMP_BLOCK, G * HD, G * CMP_HIDDEN)
    w1cat = jnp.concatenate([w1bd[:, :CMP_STRIDE], w1bd[:, CMP_STRIDE:]], axis=-1)
    w1cat = w1cat.reshape(2, CMP_STRIDE // 2, 2 * G * HD, 2 * G * CMP_HIDDEN)
    eye_e = jnp.eye(2, dtype=w1.dtype)
    w2bd = jnp.einsum("ehd,gk,ef->eghkfd", cmp_w2, eye, eye_e).reshape(2, G * CMP_HIDDEN, G * 2 * HD)
    return perm, pos, w1cat.astype(bf16), w2bd.astype(bf16)


def _kv_rows_to_out(rowsT, lead):
    t = rowsT.shape[-1]
    r = rowsT.reshape(*lead, 2, NSA_KV_HEADS, HD, t)
    n = len(lead)
    return r.transpose(*range(n), n + 3, n, n + 1, n + 2)


def _keep_last_lanes(a, n):
    t = a.shape[-1]
    if t >= n:
        return a[..., t - n:]
    return jnp.pad(a, ((0, 0),) * (a.ndim - 1) + ((n - t, 0),))


def kernel(x_prompt, x_sample, c_prompt, c_sample, cache_cmp_kv, cache_sel_kv, state_win_kv, state_ret, state_conv,
           page_table, norm1_w, ada_w, ada_b, w_in, cmp_pos, cmp_w1, cmp_w2, w_oa, ret_gn_w, w_ob, w_o, norm2_w,
           ffn_w_in, ffn_conv_w, ffn_conv_b, ffn_w_out, normf_w):
    B, T, D = x_prompt.shape
    Bd, Td, _ = x_sample.shape
    depth = w_in.shape[0]
    n_phys, page = cache_cmp_kv.shape[1], cache_cmp_kv.shape[2]
    past = page_table.shape[1] * page
    wlen = state_win_kv.shape[2]
    F = ffn_w_out.shape[1]
    Rs = Bd * Td
    tm_p = min(TM_PROJ, T)

    mod = ada_mod(jnp.concatenate([c_prompt, c_sample], axis=0), ada_w, ada_b)
    cmpT = cache_cmp_kv.transpose(0, 1, 3, 4, 5, 2).reshape(depth, n_phys, KV_ROWS, page)
    selT = cache_sel_kv.transpose(0, 1, 3, 4, 5, 2).reshape(depth, n_phys, KV_ROWS, page)
    winT = state_win_kv.transpose(0, 1, 3, 4, 5, 2).reshape(depth, Bd, KV_ROWS, wlen)

    w_in_all, w_oa_all, w_ob_all, w_o_all, ffn_in_all, ffn_out_all = (
        w.astype(bf16) for w in (w_in, w_oa, w_ob, w_o, ffn_w_in, ffn_w_out))
    ret_zero = jnp.zeros((1, B, RET_HEADS, RET_DK, RET_DV), f32)

    xp = x_prompt
    xs = x_sample.reshape(1, Rs, D)
    kv_p = None
    ret_p = ret_s = None
    win_p_out, conv_p_out, cmp_s_out, sel_s_out, win_s_out, conv_s_out = [], [], [], [], [], []
    for l in range(depth):
        w_tok, w_kvT = _prep_w_in(w_in_all[l], w_in[l])
        cw = _prep_cmp(cmp_pos[l], cmp_w1[l], cmp_w2[l])
        w_oa_b, w_ob_b, w_o_b = w_oa_all[l], w_ob_all[l], w_o_all[l]
        ffn_in_b, ffn_out_b = ffn_in_all[l], ffn_out_all[l]
        last = l == depth - 1
        mp = [mod[l, :B, k * D:(k + 1) * D].reshape(B, 1, D) for k in range(6)]
        ms = [jnp.repeat(mod[l, B:, k * D:(k + 1) * D], Td, axis=0).reshape(1, Rs, D) for k in range(6)]

        proj, cmp_p, sel_p, win_p = norm_proj_layers(xp, mp[1], mp[0], norm1_w[l], w_tok, w_kvT, l, depth, kv_p,
                                                     tm=tm_p, tn=PROJ_TN)
        kv_p = (cmp_p, sel_p)
        comp = compress_prompt(cmp_p, l, cw)
        attn = nsa_prompt(proj, sel_p, l, win_p, comp)
        x1, ret_p = retention_mix(proj, attn, xp, mp[2], w_oa_b, ret_gn_w[l], w_ob_b, w_o_b, 0, ret_zero, 0, l, depth,
                                  ret_p)
        xp, conv_p = ffn_seq(x1, mp[4], mp[3], norm2_w[l], ffn_in_b, jnp.zeros((B, CONV_W - 1, F), f32), mp[5],
                             ffn_conv_w[l], ffn_conv_b[l], ffn_out_b, normf_w, tm=min(TM_FFN, T), final_norm=last)
        win_p_out.append(_kv_rows_to_out(_keep_last_lanes(win_p, wlen), (B,)))
        conv_p_out.append(conv_p)

        proj_s, kvT_s = norm_proj(xs, ms[1], ms[0], norm1_w[l], w_tok, w_kvT, tm=Rs, tn=PROJ_TN)
        comp_s = compress_paged(cmpT, l, page_table, cw)
        q_s = proj_s[0, :, OFF_AQ:OFF_AQ + A_Q].reshape(Bd, Td, NSA_KV_HEADS, HPG, HD)
        q_s = q_s.transpose(0, 2, 3, 1, 4).reshape(Bd, NSA_HEADS * Td, HD)
        ag_s = proj_s[0, :, OFF_AG:OFF_AG + A_G].reshape(Bd, Td, NSA_KV_HEADS, HPG, 3)
        ag_s = ag_s.transpose(0, 2, 3, 1, 4).reshape(Bd, NSA_HEADS * Td, 3)
        newT = kvT_s[0].reshape(A_KV, Bd, Td).transpose(1, 0, 2)
        newT_pad = jnp.pad(newT, ((0, 0), (0, 0), (0, LANE - Td)))
        attn_s = nsa_sample(q_s, ag_s, comp_s, selT, l, page_table, winT, newT_pad, tn=Td, past=past)
        attn_s = attn_s.reshape(Bd, NSA_KV_HEADS, HPG, Td, HD).transpose(0, 3, 1, 2, 4).reshape(1, Rs, A_Q)
        ro_s, ret_s = retention(proj_s.reshape(Bd, Td, N_TOK), past, state_ret, l, l, depth, ret_s)
        x1s = mix_out(proj_s, attn_s, ro_s.reshape(1, Rs, B_V), xs, ms[2], w_oa_b, ret_gn_w[l], w_ob_b, w_o_b, tm=Rs)
        ab_s = norm_proj(x1s, ms[4], ms[3], norm2_w[l], ffn_in_b, None, tm=Rs, tn=F)
        a_ext = jnp.concatenate([state_conv[l], ab_s[0, :, :F].reshape(Bd, Td, F)], axis=1)
        am1 = a_ext[:, 1:1 + Td].reshape(1, Rs, F)
        am2 = a_ext[:, 0:Td].reshape(1, Rs, F)
        xs = ffn_out_rows(ab_s, am1, am2, x1s, ms[5], ffn_conv_w[l], ffn_conv_b[l], ffn_out_b, normf_w, tm=Rs,
                          final_norm=last)
        cmp_s_out.append(_kv_rows_to_out(newT[:, 0:KV_ROWS], (Bd,)))
        sel_s_out.append(_kv_rows_to_out(newT[:, KV_ROWS:2 * KV_ROWS], (Bd,)))
        if wlen >= Td:
            win_new = jnp.concatenate([winT[l, :, :, Td:], newT[:, 2 * KV_ROWS:]], axis=-1)
        else:
            win_new = newT[:, 2 * KV_ROWS:, Td - wlen:]
        win_s_out.append(_kv_rows_to_out(win_new, (Bd,)))
        conv_s_out.append(a_ext[:, Td:])

    new_cmp_p = _kv_rows_to_out(kv_p[0], (depth, B))
    new_sel_p = _kv_rows_to_out(kv_p[1], (depth, B))
    st = jnp.stack
    return (xp, xs.reshape(Bd, Td, D), new_cmp_p, st(cmp_s_out), new_sel_p, st(sel_s_out), st(win_p_out),
            st(win_s_out), ret_p, ret_s, st(conv_p_out), st(conv_s_out))
```
